```python
import math
import jax, jax.numpy as jnp
from jax import lax
import numpy as np

D_MODEL = 1024
BATCH = 8
SEQ = 2048
DEPTH = 4

HEAD_DIM = 64
A_GROUPS = ((128, 1), (512, 4), (2048, 16))
A_HEADS_PER_GROUP = 4
A_HEADS = A_HEADS_PER_GROUP * len(A_GROUPS)
A_WIDTH = A_HEADS * HEAD_DIM
B_HEADS = D_MODEL // HEAD_DIM
B_WIDTH = B_HEADS * HEAD_DIM
N_A = DEPTH // 2
N_B = DEPTH - N_A
D_FF = 2816
CONV_W = 3
ROPE_DIM = HEAD_DIM // 4
ROPE_THETA = 500000.0
BLK = 128
EPS = 1e-6
NEG = -1e30

kernel_name = "yoco_dilated_fox_convffn_trunk"


def rms_norm(x, g):
    x32 = x.astype(jnp.float32)
    y = x32 * lax.rsqrt(jnp.mean(x32 * x32, axis=-1, keepdims=True) + EPS)
    return (y * g.astype(jnp.float32)).astype(x.dtype)


def rope_tables(T):
    pos = jnp.arange(T, dtype=jnp.float32)
    inv = ROPE_THETA ** (-jnp.arange(0, ROPE_DIM, 2, dtype=jnp.float32) / ROPE_DIM)
    ang = pos[:, None] * inv[None, :]
    return jnp.cos(ang), jnp.sin(ang)


def apply_rope(t, cos, sin):
    half = ROPE_DIM // 2
    c = cos[None, :, None, :].astype(t.dtype)
    s = sin[None, :, None, :].astype(t.dtype)
    x1 = t[..., :half]
    x2 = t[..., half:ROPE_DIM]
    return jnp.concatenate([x1 * c - x2 * s, x2 * c + x1 * s, t[..., ROPE_DIM:]], axis=-1)


def banded_attention(q, k, v, n_back):
    N, L, H, D = q.shape
    nb = L // BLK
    qb = q.reshape(N, nb, BLK, H, D)
    kb = k.reshape(N, nb, BLK, H, D)
    vb = v.reshape(N, nb, BLK, H, D)

    def with_prev(t):
        prev = jnp.pad(t[:, :-1], ((0, 0), (1, 0), (0, 0), (0, 0), (0, 0)))
        return jnp.concatenate([prev, t], axis=2)

    kc, vc = with_prev(kb), with_prev(vb)
    s = jnp.einsum('nbqhd,nbkhd->nbhqk', qb, kc).astype(jnp.float32) * (D ** -0.5)
    rel = (jnp.arange(BLK)[:, None] + BLK) - jnp.arange(2 * BLK)[None, :]
    band = (rel >= 0) & (rel <= n_back)
    has_prev = (jnp.arange(nb)[:, None, None] > 0) | (jnp.arange(2 * BLK)[None, None, :] >= BLK)
    mask = band[None] & has_prev
    s = jnp.where(mask[None, :, None], s, NEG)
    m = jnp.max(s, axis=-1, keepdims=True)
    p = jnp.exp(s - m)
    l = jnp.sum(p, axis=-1, keepdims=True)
    o = jnp.einsum('nbhqk,nbkhd->nbqhd', (p / l).astype(v.dtype), vc)
    lse = (m + jnp.log(l))[..., 0]
    return o.reshape(N, L, H, D), lse.transpose(0, 1, 3, 2).reshape(N, L, H)


def dilated_mixer(xn, w_qkv, w_o, cos, sin):
    B, T, _ = xn.shape
    qkv = xn @ w_qkv
    q, k, v = jnp.split(qkv, 3, axis=-1)
    q = apply_rope(q.reshape(B, T, A_HEADS, HEAD_DIM), cos, sin)
    k = apply_rope(k.reshape(B, T, A_HEADS, HEAD_DIM), cos, sin)
    v = v.reshape(B, T, A_HEADS, HEAD_DIM)
    G = A_HEADS_PER_GROUP
    outs, lses = [], []
    for g, (window, r) in enumerate(A_GROUPS):
        L = T // r
        Lp = -(-L // BLK) * BLK

        def gather(t):
            t = t[:, :, g * G:(g + 1) * G].reshape(B, L, r, G, HEAD_DIM)
            t = t.transpose(0, 2, 1, 3, 4).reshape(B * r, L, G, HEAD_DIM)
            return jnp.pad(t, ((0, 0), (0, Lp - L), (0, 0), (0, 0)))

        o, lse = banded_attention(gather(q), gather(k), gather(v), window // r)
        o = o[:, :L].reshape(B, r, L, G, HEAD_DIM).transpose(0, 2, 1, 3, 4).reshape(B, T, G, HEAD_DIM)
        lse = lse[:, :L].reshape(B, r, L, G).transpose(0, 2, 1, 3).reshape(B, T, G)
        outs.append(o)
        lses.append(lse)
    alpha = jax.nn.softmax(jnp.stack(lses, axis=0), axis=0)
    o = jnp.concatenate([outs[g] * alpha[g][..., None].astype(outs[g].dtype)
                         for g in range(len(A_GROUPS))], axis=2)
    return o.reshape(B, T, A_WIDTH) @ w_o


def fox_mixer(xn, w_q, w_o, k, v, c):
    B, T, _ = xn.shape
    q = (xn @ w_q).reshape(B, T, B_HEADS, HEAD_DIM)
    c_t = c.transpose(0, 2, 1)
    scale = HEAD_DIM ** -0.5
    outs = []
    for i in range(T // BLK):
        q0, q1 = i * BLK, (i + 1) * BLK
        s = jnp.einsum('bqhd,bkhd->bhqk', q[:, q0:q1], k[:, :q1]).astype(jnp.float32) * scale
        s = s + (c_t[:, :, q0:q1, None] - c_t[:, :, None, :q1])
        causal = jnp.arange(q0, q1)[:, None] >= jnp.arange(q1)[None, :]
        s = jnp.where(causal, s, NEG)
        p = jax.nn.softmax(s, axis=-1).astype(v.dtype)
        outs.append(jnp.einsum('bhqk,bkhd->bqhd', p, v[:, :q1]))
    o = jnp.concatenate(outs, axis=1)
    return o.reshape(B, T, B_WIDTH) @ w_o


def conv_ffn(xn, w_up, cw, cb, w_down):
    a = xn @ w_up
    T = a.shape[1]
    ap = jnp.pad(a, ((0, 0), (CONV_W - 1, 0), (0, 0)))
    a = sum(ap[:, j:j + T] * cw[j] for j in range(CONV_W)) + cb
    gate, val = jnp.split(a, 2, axis=-1)
    return (jax.nn.gelu(gate, approximate=True) * val) @ w_down


def _fwd_setup_inputs(seed: int = 0) -> dict:
    key = jax.random.key(seed)
    ks = jax.random.split(key, 14)

    def nrm(k, shape, fan_in):
        return jax.random.normal(k, shape, jnp.float32) * fan_in ** -0.5

    return {
        "x": jax.random.normal(ks[0], (BATCH, SEQ, D_MODEL), jnp.float32),
        "norm_gains": 1.0 + 0.05 * jax.random.normal(ks[1], (DEPTH, 4, D_MODEL), jnp.float32),
        "w_qkv_a": nrm(ks[2], (N_A, D_MODEL, 3 * A_WIDTH), D_MODEL),
        "w_o_a": nrm(ks[3], (N_A, A_WIDTH, D_MODEL), A_WIDTH),
        "w_q_b": nrm(ks[4], (N_B, D_MODEL, B_WIDTH), D_MODEL),
        "w_o_b": nrm(ks[5], (N_B, B_WIDTH, D_MODEL), B_WIDTH),
        "kv_norm": 1.0 + 0.05 * jax.random.normal(ks[6], (D_MODEL,), jnp.float32),
        "w_kvf": nrm(ks[7], (D_MODEL, 2 * B_WIDTH + B_HEADS), D_MODEL),
        "b_f": 3.0 + 0.5 * jax.random.normal(ks[8], (B_HEADS,), jnp.float32),
        "w_up": nrm(ks[9], (DEPTH, D_MODEL, 2 * D_FF), D_MODEL),
        "conv_w": nrm(ks[10], (DEPTH, CONV_W, 2 * D_FF), CONV_W),
        "conv_b": 0.01 * jax.random.normal(ks[11], (DEPTH, 2 * D_FF), jnp.float32),
        "w_down": nrm(ks[12], (DEPTH, D_FF, D_MODEL), D_FF),
    }


def _fwd_reference(x, norm_gains, w_qkv_a, w_o_a, w_q_b, w_o_b, kv_norm, w_kvf, b_f,
              w_up, conv_w, conv_b, w_down):
    B, T, _ = x.shape
    cos, sin = rope_tables(T)
    h = x
    k_sh = v_sh = c_sh = None
    for l in range(DEPTH):
        g = norm_gains[l]
        if l < N_A:
            mix = dilated_mixer(rms_norm(h, g[0]), w_qkv_a[l], w_o_a[l], cos, sin)
        else:
            if l == N_A:
                kvf = rms_norm(h, kv_norm) @ w_kvf
                k_sh = kvf[..., :B_WIDTH].reshape(B, T, B_HEADS, HEAD_DIM)
                v_sh = kvf[..., B_WIDTH:2 * B_WIDTH].reshape(B, T, B_HEADS, HEAD_DIM)
                log_f = jax.nn.log_sigmoid((kvf[..., 2 * B_WIDTH:] + b_f).astype(jnp.float32))
                c_sh = jnp.cumsum(log_f, axis=1)
            j = l - N_A
            mix = fox_mixer(rms_norm(h, g[0]), w_q_b[j], w_o_b[j], k_sh, v_sh, c_sh)
        h = h + rms_norm(mix, g[1])
        f = conv_ffn(rms_norm(h, g[2]), w_up[l], conv_w[l], conv_b[l], w_down[l])
        h = h + rms_norm(f, g[3])
    return h


import jax as _jax
import jax.numpy as _jnp

TWIN_FORMAT = 'train_step'
FWD_PARAMS = ['x', 'norm_gains', 'w_qkv_a', 'w_o_a', 'w_q_b', 'w_o_b', 'kv_norm', 'w_kvf', 'b_f', 'w_up', 'conv_w', 'conv_b', 'w_down']
TWIN_WEIGHTS = ['norm_gains', 'w_qkv_a', 'w_o_a', 'w_q_b', 'w_o_b', 'kv_norm', 'w_kvf', 'b_f', 'w_up', 'conv_w', 'conv_b', 'w_down']
TWIN_DIFF_INPUT = 'x'
TWIN_INPUTS = ['x', 'norm_gains', 'w_qkv_a', 'w_o_a', 'w_q_b', 'w_o_b', 'kv_norm', 'w_kvf', 'b_f', 'w_up', 'conv_w', 'conv_b', 'w_down', 'loss_target', 'm_norm_gains', 'm_w_qkv_a', 'm_w_o_a', 'm_w_q_b', 'm_w_o_b', 'm_kv_norm', 'm_w_kvf', 'm_b_f', 'm_w_up', 'm_conv_w', 'm_conv_b', 'm_w_down', 'v_norm_gains', 'v_w_qkv_a', 'v_w_o_a', 'v_w_q_b', 'v_w_o_b', 'v_kv_norm', 'v_w_kvf', 'v_b_f', 'v_w_up', 'v_conv_w', 'v_conv_b', 'v_w_down']
TWIN_OUTPUTS = ['loss', 'grad_x', 'grad_norm_gains', 'grad_w_qkv_a', 'grad_w_o_a', 'grad_w_q_b', 'grad_w_o_b', 'grad_kv_norm', 'grad_w_kvf', 'grad_b_f', 'grad_w_up', 'grad_conv_w', 'grad_conv_b', 'grad_w_down', 'delta_norm_gains', 'delta_w_qkv_a', 'delta_w_o_a', 'delta_w_q_b', 'delta_w_o_b', 'delta_kv_norm', 'delta_w_kvf', 'delta_b_f', 'delta_w_up', 'delta_conv_w', 'delta_conv_b', 'delta_w_down', 'new_m_norm_gains', 'new_m_w_qkv_a', 'new_m_w_o_a', 'new_m_w_q_b', 'new_m_w_o_b', 'new_m_kv_norm', 'new_m_w_kvf', 'new_m_b_f', 'new_m_w_up', 'new_m_conv_w', 'new_m_conv_b', 'new_m_w_down', 'new_v_norm_gains', 'new_v_w_qkv_a', 'new_v_w_o_a', 'new_v_w_q_b', 'new_v_w_o_b', 'new_v_kv_norm', 'new_v_w_kvf', 'new_v_b_f', 'new_v_w_up', 'new_v_conv_w', 'new_v_conv_b', 'new_v_w_down']
TWIN_LEAF_KINDS = {'loss': 'loss', 'grad_x': 'grad_x', 'grad_norm_gains': 'grad_w', 'grad_w_qkv_a': 'grad_w', 'grad_w_o_a': 'grad_w', 'grad_w_q_b': 'grad_w', 'grad_w_o_b': 'grad_w', 'grad_kv_norm': 'grad_w', 'grad_w_kvf': 'grad_w', 'grad_b_f': 'grad_w', 'grad_w_up': 'grad_w', 'grad_conv_w': 'grad_w', 'grad_conv_b': 'grad_w', 'grad_w_down': 'grad_w', 'delta_norm_gains': 'delta_w', 'delta_w_qkv_a': 'delta_w', 'delta_w_o_a': 'delta_w', 'delta_w_q_b': 'delta_w', 'delta_w_o_b': 'delta_w', 'delta_kv_norm': 'delta_w', 'delta_w_kvf': 'delta_w', 'delta_b_f': 'delta_w', 'delta_w_up': 'delta_w', 'delta_conv_w': 'delta_w', 'delta_conv_b': 'delta_w', 'delta_w_down': 'delta_w', 'new_m_norm_gains': 'new_m', 'new_m_w_qkv_a': 'new_m', 'new_m_w_o_a': 'new_m', 'new_m_w_q_b': 'new_m', 'new_m_w_o_b': 'new_m', 'new_m_kv_norm': 'new_m', 'new_m_w_kvf': 'new_m', 'new_m_b_f': 'new_m', 'new_m_w_up': 'new_m', 'new_m_conv_w': 'new_m', 'new_m_conv_b': 'new_m', 'new_m_w_down': 'new_m', 'new_v_norm_gains': 'new_v', 'new_v_w_qkv_a': 'new_v', 'new_v_w_o_a': 'new_v', 'new_v_w_q_b': 'new_v', 'new_v_w_o_b': 'new_v', 'new_v_kv_norm': 'new_v', 'new_v_w_kvf': 'new_v', 'new_v_b_f': 'new_v', 'new_v_w_up': 'new_v', 'new_v_conv_w': 'new_v', 'new_v_conv_b': 'new_v', 'new_v_w_down': 'new_v'}


def _forward(args):
    return _fwd_reference(*[args[k] for k in FWD_PARAMS])


def _output_shape():
    out = _jax.eval_shape(lambda: _forward(_fwd_setup_inputs(0)))
    return out.shape, out.dtype

N_MICROBATCH = 1
ADAM_LR = 0.001
ADAM_B1 = 0.9
ADAM_B2 = 0.999
ADAM_EPS = 1e-08
ADAM_WD = 0.01
ADAM_STEP = 10
PER_EXAMPLE_BATCH_AXIS = {'x': 0, 'loss_target': 0}
SHARED_INPUTS = []
_WEIGHT_DTYPES = {'norm_gains': _jnp.float32, 'w_qkv_a': _jnp.float32, 'w_o_a': _jnp.float32, 'w_q_b': _jnp.float32, 'w_o_b': _jnp.float32, 'kv_norm': _jnp.float32, 'w_kvf': _jnp.float32, 'b_f': _jnp.float32, 'w_up': _jnp.float32, 'conv_w': _jnp.float32, 'conv_b': _jnp.float32, 'w_down': _jnp.float32}
MOMENT_SCALE = {'norm_gains': 1.021599e+01, 'w_qkv_a': 2.767649e+00, 'w_o_a': 3.815036e+00, 'w_q_b': 2.912457e-01, 'w_o_b': 2.207073e+00, 'kv_norm': 3.219069e+00, 'w_kvf': 2.277236e+00, 'b_f': 3.222822e+00, 'w_up': 7.006511e-01, 'conv_w': 7.471425e-01, 'conv_b': 1.844558e+00, 'w_down': 1.276930e+00}


def _to_microbatches(a, axis):
    t = _jnp.moveaxis(a, axis, 0)
    t = t.reshape((N_MICROBATCH, t.shape[0] // N_MICROBATCH) + t.shape[1:])
    return _jnp.moveaxis(t, 1, axis + 1)


def setup_inputs(seed: int = 0) -> dict:
    inp = _fwd_setup_inputs(seed)
    key = _jax.random.fold_in(_jax.random.key(seed), 7919)
    shape, _ = _output_shape()
    out = dict(inp)
    out["loss_target"] = _jax.random.normal(_jax.random.fold_in(key, 0), shape, _jnp.float32)
    for i, name in enumerate(TWIN_WEIGHTS):
        w = inp[name].astype(_jnp.float32)
        if MOMENT_SCALE is None:
            s = _jnp.sqrt(_jnp.mean(_jnp.square(w)) + 1e-30)
        else:
            s = MOMENT_SCALE[name]
        km, kv = _jax.random.split(_jax.random.fold_in(key, i + 1))
        out[name] = w
        out["m_" + name] = s * _jax.random.normal(km, w.shape, _jnp.float32)
        out["v_" + name] = (s * s) * _jax.random.uniform(kv, w.shape, _jnp.float32, 0.5, 1.5)
    if N_MICROBATCH > 1:
        for name, axis in PER_EXAMPLE_BATCH_AXIS.items():
            out[name] = _to_microbatches(out[name], axis)
    return {'x': out['x'], 'norm_gains': out['norm_gains'], 'w_qkv_a': out['w_qkv_a'], 'w_o_a': out['w_o_a'], 'w_q_b': out['w_q_b'], 'w_o_b': out['w_o_b'], 'kv_norm': out['kv_norm'], 'w_kvf': out['w_kvf'], 'b_f': out['b_f'], 'w_up': out['w_up'], 'conv_w': out['conv_w'], 'conv_b': out['conv_b'], 'w_down': out['w_down'], 'loss_target': out['loss_target'], 'm_norm_gains': out['m_norm_gains'], 'm_w_qkv_a': out['m_w_qkv_a'], 'm_w_o_a': out['m_w_o_a'], 'm_w_q_b': out['m_w_q_b'], 'm_w_o_b': out['m_w_o_b'], 'm_kv_norm': out['m_kv_norm'], 'm_w_kvf': out['m_w_kvf'], 'm_b_f': out['m_b_f'], 'm_w_up': out['m_w_up'], 'm_conv_w': out['m_conv_w'], 'm_conv_b': out['m_conv_b'], 'm_w_down': out['m_w_down'], 'v_norm_gains': out['v_norm_gains'], 'v_w_qkv_a': out['v_w_qkv_a'], 'v_w_o_a': out['v_w_o_a'], 'v_w_q_b': out['v_w_q_b'], 'v_w_o_b': out['v_w_o_b'], 'v_kv_norm': out['v_kv_norm'], 'v_w_kvf': out['v_w_kvf'], 'v_b_f': out['v_b_f'], 'v_w_up': out['v_w_up'], 'v_conv_w': out['v_conv_w'], 'v_conv_b': out['v_conv_b'], 'v_w_down': out['v_w_down']}


def _loss(weights, diff, rest, loss_target):
    with _jax.named_scope("forward"):
        args = {**rest, TWIN_DIFF_INPUT: diff, **{k: w.astype(_WEIGHT_DTYPES[k]) for k, w in weights.items()}}
        y = _forward(args)
    with _jax.named_scope("loss_head"):
        err = _jnp.square(y.astype(_jnp.float32) - loss_target)
        return 0.5 * _jnp.sum(_jnp.mean(err, axis=-1)) if err.ndim else 0.5 * err


def _adamw(w, g, m, v):
    m = ADAM_B1 * m + (1.0 - ADAM_B1) * g
    v = ADAM_B2 * v + (1.0 - ADAM_B2) * _jnp.square(g)
    m_hat = m / (1.0 - ADAM_B1 ** ADAM_STEP)
    v_hat = v / (1.0 - ADAM_B2 ** ADAM_STEP)
    delta = -ADAM_LR * (m_hat / (_jnp.sqrt(v_hat) + ADAM_EPS) + ADAM_WD * w)
    return delta, m, v


def reference(x, norm_gains, w_qkv_a, w_o_a, w_q_b, w_o_b, kv_norm, w_kvf, b_f, w_up, conv_w, conv_b, w_down, loss_target, m_norm_gains, m_w_qkv_a, m_w_o_a, m_w_q_b, m_w_o_b, m_kv_norm, m_w_kvf, m_b_f, m_w_up, m_conv_w, m_conv_b, m_w_down, v_norm_gains, v_w_qkv_a, v_w_o_a, v_w_q_b, v_w_o_b, v_kv_norm, v_w_kvf, v_b_f, v_w_up, v_conv_w, v_conv_b, v_w_down):
    given = dict(x=x, norm_gains=norm_gains, w_qkv_a=w_qkv_a, w_o_a=w_o_a, w_q_b=w_q_b, w_o_b=w_o_b, kv_norm=kv_norm, w_kvf=w_kvf, b_f=b_f, w_up=w_up, conv_w=conv_w, conv_b=conv_b, w_down=w_down, loss_target=loss_target, m_norm_gains=m_norm_gains, m_w_qkv_a=m_w_qkv_a, m_w_o_a=m_w_o_a, m_w_q_b=m_w_q_b, m_w_o_b=m_w_o_b, m_kv_norm=m_kv_norm, m_w_kvf=m_w_kvf, m_b_f=m_b_f, m_w_up=m_w_up, m_conv_w=m_conv_w, m_conv_b=m_conv_b, m_w_down=m_w_down, v_norm_gains=v_norm_gains, v_w_qkv_a=v_w_qkv_a, v_w_o_a=v_w_o_a, v_w_q_b=v_w_q_b, v_w_o_b=v_w_o_b, v_kv_norm=v_kv_norm, v_w_kvf=v_w_kvf, v_b_f=v_b_f, v_w_up=v_w_up, v_conv_w=v_conv_w, v_conv_b=v_conv_b, v_w_down=v_w_down)
    weights = {n: given[n] for n in TWIN_WEIGHTS}
    shared = {n: given[n] for n in SHARED_INPUTS}
    per_example = {n: given[n] for n in ['x']}
    grad_fn = _jax.value_and_grad(_loss, argnums=(0, 1))

    def one_microbatch(ex, loss_target):
        ex = dict(ex)
        diff = ex.pop(TWIN_DIFF_INPUT)
        return grad_fn(weights, diff, {**shared, **ex}, loss_target)

    if N_MICROBATCH == 1:
        loss, (grad_w, grad_x) = one_microbatch(per_example, given["loss_target"])
    else:
        def body(carry, xs):
            loss_sum, grad_sum = carry
            l_k, (gw_k, gx_k) = one_microbatch(xs[0], xs[1])
            with _jax.named_scope("update"):
                return (loss_sum + l_k, _jax.tree.map(_jnp.add, grad_sum, gw_k)), gx_k

        init = (_jnp.zeros((), _jnp.float32), _jax.tree.map(_jnp.zeros_like, weights))
        (loss, grad_w), grad_x = _jax.lax.scan(body, init, (per_example, given["loss_target"]))
    with _jax.named_scope("update"):
        delta_w, new_m, new_v = {}, {}, {}
        for n in TWIN_WEIGHTS:
            delta_w[n], new_m[n], new_v[n] = _adamw(weights[n], grad_w[n], given["m_" + n], given["v_" + n])
    return (loss, grad_x, *[grad_w[n] for n in TWIN_WEIGHTS], *[delta_w[n] for n in TWIN_WEIGHTS],
            *[new_m[n] for n in TWIN_WEIGHTS], *[new_v[n] for n in TWIN_WEIGHTS])
```

```python
import functools
import math

import jax
import jax.numpy as jnp
from jax import lax
from jax.experimental import pallas as pl
from jax.experimental.pallas import tpu as pltpu

F32 = jnp.float32
BF16 = jnp.bfloat16

T = 2048
D = 1024
DEPTH = 4
N_A = 2
HD = 64
A_GROUPS = ((128, 1), (512, 4), (2048, 16))
A_W = 768
B_HEADS = 16
DFF = 2816
NDEV = 8
SLAB = 2 * DFF // NDEV
KVF = 2 * D + B_HEADS
KVF_PAD = 2304
ROPE_DIM = 16
ROPE_THETA = 500000.0
EPS = 1e-6
NEG = -1e30
BLK = 128
LANES = 128
SCALE = HD ** -0.5
VMEM_LIMIT = 56 * 1024 * 1024

ADAM_LR = 0.001
ADAM_B1 = 0.9
ADAM_B2 = 0.999
ADAM_EPS = 1e-08
ADAM_WD = 0.01
ADAM_STEP = 10
HIGHEST = lax.Precision.HIGHEST


def _params(*sem):
    return pltpu.CompilerParams(dimension_semantics=sem or None, vmem_limit_bytes=VMEM_LIMIT)


def _bf(x):
    return x if x.dtype == BF16 else x.astype(BF16)


def matmul(name, a, b, *, ta=False, tb=False, out_dtype=F32, tm=512, tn=512, batch=None):
    a_b = batch in ("a_out", "reduce")
    b_b = batch in ("b_out", "reduce")
    o_b = batch in ("a_out", "b_out")
    nb = a.shape[0] if a_b else (b.shape[0] if b_b else 1)
    ash = a.shape[1:] if a_b else a.shape
    bsh = b.shape[1:] if b_b else b.shape
    m, k = (ash[1], ash[0]) if ta else ash
    k2, n = (bsh[1], bsh[0]) if tb else bsh
    assert k == k2, (name, a.shape, b.shape)
    tm, tn = min(tm, m), min(tn, n)
    assert m % tm == 0 and n % tn == 0, (name, m, n, tm, tn)
    nbr = nb if batch == "reduce" else 1
    grid = (nb if o_b else 1, n // tn, m // tm, nbr)

    def bidx(bo, br):
        return bo if o_b else br

    def spec(batched, block, idx):
        if batched:
            return pl.BlockSpec((None,) + block, lambda bo, j, i, br: (bidx(bo, br),) + idx(i, j))
        return pl.BlockSpec(block, lambda bo, j, i, br: idx(i, j))

    a_spec = spec(a_b, (k, tm) if ta else (tm, k), (lambda i, j: (0, i)) if ta else (lambda i, j: (i, 0)))
    b_spec = spec(b_b, (tn, k) if tb else (k, tn), (lambda i, j: (j, 0)) if tb else (lambda i, j: (0, j)))
    o_spec = spec(o_b, (tm, tn), lambda i, j: (i, j))
    dims = (((0 if ta else 1,), (1 if tb else 0,)), ((), ()))

    def body(a_ref, b_ref, o_ref, *acc):
        p = lax.dot_general(_bf(a_ref[...]), _bf(b_ref[...]), dims, preferred_element_type=F32)
        if nbr == 1:
            o_ref[...] = p.astype(out_dtype)
        else:
            r = pl.program_id(3)

            @pl.when(r == 0)
            def _():
                acc[0][...] = p

            @pl.when(r > 0)
            def _():
                acc[0][...] += p

            @pl.when(r == nbr - 1)
            def _():
                o_ref[...] = acc[0][...].astype(out_dtype)

    out_shape = ((nb,) if o_b else ()) + (m, n)
    return pl.pallas_call(
        body, name=name, grid=grid, in_specs=[a_spec, b_spec], out_specs=o_spec,
        out_shape=jax.ShapeDtypeStruct(out_shape, out_dtype),
        scratch_shapes=[pltpu.VMEM((tm, tn), F32)] if nbr > 1 else [],
        compiler_params=_params("parallel", "parallel", "parallel", "arbitrary"),
    )(a, b)


def _rms(x, g):
    return x * lax.rsqrt(jnp.mean(x * x, axis=-1, keepdims=True) + EPS) * g


def resid_norm(name, h, y, gy, gains, tb=256):
    n_g = len(gains)
    has_y = y is not None
    row = pl.BlockSpec((tb, D), lambda i: (i, 0))
    vec = pl.BlockSpec((1, D), lambda i: (0, 0))

    def body(*refs):
        h_ref = refs[0]
        pos = 1
        hn = h_ref[...]
        if has_y:
            hn = hn + _rms(refs[1][...], refs[2][...])
            pos = 3
        g_refs = refs[pos:pos + n_g]
        outs = refs[pos + n_g:]
        if has_y:
            outs[0][...] = hn
            outs = outs[1:]
        for g_ref, o_ref in zip(g_refs, outs):
            o_ref[...] = _rms(hn, g_ref[...]).astype(BF16)

    ins = [h] + ([y, gy.reshape(1, D)] if has_y else []) + [g.reshape(1, D) for g in gains]
    in_specs = [row] + ([row, vec] if has_y else []) + [vec] * n_g
    out_shape = ([jax.ShapeDtypeStruct((T, D), F32)] if has_y else []) + [jax.ShapeDtypeStruct((T, D), BF16)] * n_g
    res = pl.pallas_call(
        body, name=name, grid=(T // tb,), in_specs=in_specs, out_specs=[row] * len(out_shape),
        out_shape=out_shape, compiler_params=_params("parallel"),
    )(*ins)
    return (res[0], list(res[1:])) if has_y else (h, list(res))


def rms_bwd(name, x, g, dy, add=None, out_dtype=F32, tb=256):
    has_add = add is not None
    row = pl.BlockSpec((tb, D), lambda i: (i, 0))
    vec = pl.BlockSpec((1, D), lambda i: (0, 0))

    def body(*refs):
        x_ref, g_ref, dy_ref = refs[:3]
        dx_ref, dg_ref = refs[-2:]
        xv = x_ref[...]
        dyv = dy_ref[...].astype(F32)
        r = lax.rsqrt(jnp.mean(xv * xv, axis=-1, keepdims=True) + EPS)
        gdy = dyv * g_ref[...]
        dx = r * gdy - xv * (r * r * r * jnp.mean(xv * gdy, axis=-1, keepdims=True))
        if has_add:
            dx = dx + refs[3][...]
        dx_ref[...] = dx.astype(out_dtype)
        part = jnp.sum(dyv * xv * r, axis=0, keepdims=True)

        @pl.when(pl.program_id(0) == 0)
        def _():
            dg_ref[...] = part

        @pl.when(pl.program_id(0) > 0)
        def _():
            dg_ref[...] += part

    ins = [x, g.reshape(1, D), dy] + ([add] if has_add else [])
    return pl.pallas_call(
        body, name=name, grid=(T // tb,), in_specs=[row, vec, row] + ([row] if has_add else []),
        out_specs=[row, vec],
        out_shape=[jax.ShapeDtypeStruct((T, D), out_dtype), jax.ShapeDtypeStruct((1, D), F32)],
        compiler_params=_params("arbitrary"),
    )(*ins)


def rope_tables():
    pos = jnp.arange(T, dtype=F32)
    inv = ROPE_THETA ** (-jnp.arange(0, ROPE_DIM, 2, dtype=F32) / ROPE_DIM)
    ang = pos[:, None] * inv[None, :]
    cos, sin = jnp.cos(ang), jnp.sin(ang)
    half = ROPE_DIM // 2
    one = jnp.ones((T, HD - ROPE_DIM), F32)
    zero = jnp.zeros((T, HD - ROPE_DIM), F32)
    zh = jnp.zeros((T, half), F32)
    c = jnp.concatenate([cos, cos, one], axis=1)
    s_up = jnp.concatenate([zh, sin, zero], axis=1)
    s_dn = jnp.concatenate([-sin, zh, zero], axis=1)
    rep = LANES // HD
    return tuple(jnp.tile(t, (1, rep)) for t in (c, s_up, s_dn))


def rope(name, t, tables, inverse, out_dtype=BF16, tb=512):
    c, s_up, s_dn = tables
    n_rot = 2 * A_W // LANES
    half = ROPE_DIM // 2
    blk = pl.BlockSpec((tb, LANES), lambda i, j: (i, j))
    tab = pl.BlockSpec((tb, LANES), lambda i, j: (i, 0))

    def body(t_ref, c_ref, su_ref, sd_ref, o_ref):
        x = t_ref[...].astype(F32)
        sgn = -1.0 if inverse else 1.0
        rot = (x * c_ref[...] + pltpu.roll(x, half, 1) * (sgn * su_ref[...])
               + pltpu.roll(x, LANES - half, 1) * (sgn * sd_ref[...]))
        o_ref[...] = jnp.where(pl.program_id(1) < n_rot, rot, x).astype(out_dtype)

    return pl.pallas_call(
        body, name=name, grid=(T // tb, 3 * A_W // LANES), in_specs=[blk, tab, tab, tab], out_specs=blk,
        out_shape=jax.ShapeDtypeStruct((T, 3 * A_W), out_dtype), compiler_params=_params("parallel", "parallel"),
    )(t, c, s_up, s_dn)


GW = 4 * HD


def _band_mask(b):
    qi = lax.broadcasted_iota(jnp.int32, (BLK, 2 * BLK), 0)
    kj = lax.broadcasted_iota(jnp.int32, (BLK, 2 * BLK), 1)
    return (kj <= qi + BLK) & (kj >= qi) & ((kj >= BLK) | (b > 0))


def attn_a_fwd(name, qkvr, g, r):
    length = T // r
    nblk = length // BLK
    view = qkvr.reshape(length, r * 3 * A_W)
    ncol = 3 * A_W // GW

    def col(section, prev):
        def idx(j, b):
            return (jnp.maximum(b - 1, 0) if prev else b, j * ncol + 3 * section + g)
        return pl.BlockSpec((BLK, GW), idx)

    out = pl.BlockSpec((BLK, GW), lambda j, b: (b, j))

    def body(q_ref, kp_ref, kc_ref, vp_ref, vc_ref, o_ref, lse_ref):
        mask = _band_mask(pl.program_id(1))
        k2 = jnp.concatenate([kp_ref[...], kc_ref[...]], axis=0)
        v2 = jnp.concatenate([vp_ref[...], vc_ref[...]], axis=0)
        q = q_ref[...]
        for h in range(4):
            sl = slice(h * HD, (h + 1) * HD)
            s = lax.dot_general(q[:, sl], k2[:, sl], (((1,), (1,)), ((), ())), preferred_element_type=F32) * SCALE
            s = jnp.where(mask, s, NEG)
            m = jnp.max(s, axis=1, keepdims=True)
            p = jnp.exp(s - m)
            l = jnp.sum(p, axis=1, keepdims=True)
            o_ref[:, sl] = jnp.dot((p / l).astype(BF16), v2[:, sl], preferred_element_type=F32)
            lse_ref[:, sl] = jnp.broadcast_to(m + jnp.log(l), (BLK, HD))

    shape = jax.ShapeDtypeStruct((length, r * GW), F32)
    o, lse = pl.pallas_call(
        body, name=name, grid=(r, nblk),
        in_specs=[col(0, False), col(1, True), col(1, False), col(2, True), col(2, False)],
        out_specs=[out, out], out_shape=[shape, shape], compiler_params=_params("parallel", "parallel"),
    )(view, view, view, view, view)
    return o.reshape(T, GW), lse.reshape(T, GW)


def attn_a_bwd(name, qkvr, do, lse, dd, g, r):
    length = T // r
    nblk = length // BLK
    view = qkvr.reshape(length, r * 3 * A_W)
    ncol = 3 * A_W // GW

    def col(section, shift):
        def idx(j, b):
            return (jnp.clip(b + shift, 0, nblk - 1), j * ncol + 3 * section + g)
        return pl.BlockSpec((BLK, GW), idx)

    def tok(shift):
        return pl.BlockSpec((BLK, GW), lambda j, b: (jnp.clip(b + shift, 0, nblk - 1), j))

    def body(q_ref, qn_ref, kp_ref, kc_ref, vp_ref, vc_ref, do_ref, don_ref, lse_ref, lsen_ref, dd_ref, ddn_ref,
             dq_ref, dk_ref, dv_ref):
        b = pl.program_id(1)
        mask = _band_mask(b)
        qi = lax.broadcasted_iota(jnp.int32, (2 * BLK, BLK), 0)
        kj = lax.broadcasted_iota(jnp.int32, (2 * BLK, BLK), 1)
        kmask = ((qi < BLK) & (kj <= qi)) | ((qi >= BLK) & (kj >= qi - BLK) & (b + 1 < nblk))
        k2 = jnp.concatenate([kp_ref[...], kc_ref[...]], axis=0)
        v2 = jnp.concatenate([vp_ref[...], vc_ref[...]], axis=0)
        q2 = jnp.concatenate([q_ref[...], qn_ref[...]], axis=0)
        do2 = jnp.concatenate([do_ref[...], don_ref[...]], axis=0)
        lse2 = jnp.concatenate([lse_ref[...], lsen_ref[...]], axis=0)
        dd2 = jnp.concatenate([dd_ref[...], ddn_ref[...]], axis=0)
        nt = (((1,), (1,)), ((), ()))
        tn = (((0,), (0,)), ((), ()))
        for h in range(4):
            sl = slice(h * HD, (h + 1) * HD)
            one = slice(h * HD, h * HD + 1)
            qh, kh, vh, doh = q2[:, sl], k2[:, sl], v2[:, sl], do2[:, sl]
            s = lax.dot_general(qh[:BLK], kh, nt, preferred_element_type=F32) * SCALE
            p = jnp.where(mask, jnp.exp(s - lse2[:BLK, one]), 0.0)
            dp = lax.dot_general(doh[:BLK], vh, nt, preferred_element_type=F32)
            ds = p * (dp + dd2[:BLK, one])
            dq_ref[:, sl] = jnp.dot(ds.astype(BF16), kh, preferred_element_type=F32) * SCALE
            kc, vc = kh[BLK:], vh[BLK:]
            s = lax.dot_general(qh, kc, nt, preferred_element_type=F32) * SCALE
            p = jnp.where(kmask, jnp.exp(s - lse2[:, one]), 0.0)
            dp = lax.dot_general(doh, vc, nt, preferred_element_type=F32)
            ds = p * (dp + dd2[:, one])
            dk_ref[:, sl] = lax.dot_general(ds.astype(BF16), qh, tn, preferred_element_type=F32) * SCALE
            dv_ref[:, sl] = lax.dot_general(p.astype(BF16), doh, tn, preferred_element_type=F32)

    dov = do.reshape(length, r * GW)
    lsev = lse.reshape(length, r * GW)
    ddv = dd.reshape(length, r * GW)
    shape = jax.ShapeDtypeStruct((length, r * GW), F32)
    dq, dk, dv = pl.pallas_call(
        body, name=name, grid=(r, nblk),
        in_specs=[col(0, 0), col(0, 1), col(1, -1), col(1, 0), col(2, -1), col(2, 0),
                  tok(0), tok(1), tok(0), tok(1), tok(0), tok(1)],
        out_specs=[tok(0)] * 3, out_shape=[shape] * 3, compiler_params=_params("parallel", "parallel"),
    )(view, view, view, view, view, view, dov, dov, lsev, lsev, ddv, ddv)
    return dq.reshape(T, GW), dk.reshape(T, GW), dv.reshape(T, GW)


def _head_sum(x):
    i = lax.div(lax.broadcasted_iota(jnp.int32, (GW, GW), 0), jnp.int32(HD))
    j = lax.div(lax.broadcasted_iota(jnp.int32, (GW, GW), 1), jnp.int32(HD))
    return jnp.dot(x, (i == j).astype(F32), precision=HIGHEST, preferred_element_type=F32)


def _alphas(lses):
    m = jnp.maximum(jnp.maximum(lses[0], lses[1]), lses[2])
    e = [jnp.exp(l - m) for l in lses]
    z = e[0] + e[1] + e[2]
    return [x / z for x in e]


def combine_fwd(name, os_, lses, tb=256):
    blk = pl.BlockSpec((tb, GW), lambda i: (i, 0))

    def body(o0, o1, o2, l0, l1, l2, oc_ref):
        al = _alphas([l0[...], l1[...], l2[...]])
        for g, o_ref in enumerate((o0, o1, o2)):
            oc_ref[:, g * GW:(g + 1) * GW] = (o_ref[...] * al[g]).astype(BF16)

    return pl.pallas_call(
        body, name=name, grid=(T // tb,), in_specs=[blk] * 6, out_specs=pl.BlockSpec((tb, A_W), lambda i: (i, 0)),
        out_shape=jax.ShapeDtypeStruct((T, A_W), BF16), compiler_params=_params("parallel"),
    )(*os_, *lses)


def combine_bwd(name, doc, os_, lses, tb=256):
    blk = pl.BlockSpec((tb, GW), lambda i: (i, 0))

    def body(doc_ref, o0, o1, o2, l0, l1, l2, d0, d1, d2, e0, e1, e2):
        al = _alphas([l0[...], l1[...], l2[...]])
        dal = [_head_sum(doc_ref[:, g * GW:(g + 1) * GW] * o_ref[...]) for g, o_ref in enumerate((o0, o1, o2))]
        mean = al[0] * dal[0] + al[1] * dal[1] + al[2] * dal[2]
        for g, (do_ref, dd_ref) in enumerate(((d0, e0), (d1, e1), (d2, e2))):
            do_ref[...] = (doc_ref[:, g * GW:(g + 1) * GW] * al[g]).astype(BF16)
            dd_ref[...] = al[g] * (dal[g] - mean) - al[g] * dal[g]

    res = pl.pallas_call(
        body, name=name, grid=(T // tb,), in_specs=[pl.BlockSpec((tb, A_W), lambda i: (i, 0))] + [blk] * 6,
        out_specs=[blk] * 6,
        out_shape=[jax.ShapeDtypeStruct((T, GW), BF16)] * 3 + [jax.ShapeDtypeStruct((T, GW), F32)] * 3,
        compiler_params=_params("parallel"),
    )(doc, *os_, *lses)
    return res[:3], res[3:]


TQ = 128
TK = 256
NT = (((1,), (1,)), ((), ()))
TN = (((0,), (0,)), ((), ()))


def _fox_scores(q, kb, cj, row0, col0, masked):
    s = lax.dot_general(q, kb, NT, preferred_element_type=F32) * SCALE - cj
    if masked:
        qi = row0 + lax.broadcasted_iota(jnp.int32, s.shape, 0)
        kj = col0 + lax.broadcasted_iota(jnp.int32, s.shape, 1)
        s = jnp.where(kj <= qi, s, NEG)
    return s


def fox_fwd(name, q, k, v, ct):
    def body(q_ref, k_ref, v_ref, c_ref, o_ref, lse_ref):
        i = pl.program_id(1)
        qv = q_ref[...]
        n_full = lax.div(i, jnp.int32(2))

        def step(jb, carry, masked):
            m, l, acc = carry
            start = pl.multiple_of(jb * TK, TK)
            kb = k_ref[pl.ds(start, TK), :]
            vb = v_ref[pl.ds(start, TK), :]
            s = _fox_scores(qv, kb, c_ref[pl.ds(jb, 1), :], i * TQ, jb * TK, masked)
            m_new = jnp.maximum(m, jnp.max(s, axis=1, keepdims=True))
            a = jnp.exp(m - m_new)
            p = jnp.exp(s - m_new)
            l = a * l + jnp.sum(p, axis=1, keepdims=True)
            acc = a * acc + jnp.dot(p.astype(BF16), vb, preferred_element_type=F32)
            return m_new, l, acc

        init = (jnp.full((TQ, 1), NEG, F32), jnp.zeros((TQ, 1), F32), jnp.zeros((TQ, HD), F32))
        carry = lax.fori_loop(0, n_full, lambda jb, c: step(jb, c, False), init)
        m, l, acc = step(n_full, carry, True)
        o_ref[...] = (acc / l).astype(BF16)
        lse_ref[...] = jnp.broadcast_to(m + jnp.log(l), (TQ, LANES))

    head = lambda h, i: (h, 0, 0)
    return pl.pallas_call(
        body, name=name, grid=(B_HEADS, T // TQ),
        in_specs=[pl.BlockSpec((None, TQ, HD), lambda h, i: (h, i, 0)), pl.BlockSpec((None, T, HD), head),
                  pl.BlockSpec((None, T, HD), head), pl.BlockSpec((None, T // TK, TK), head)],
        out_specs=[pl.BlockSpec((None, TQ, HD), lambda h, i: (h, i, 0)),
                   pl.BlockSpec((None, TQ, LANES), lambda h, i: (h, i, 0))],
        out_shape=[jax.ShapeDtypeStruct((B_HEADS, T, HD), BF16), jax.ShapeDtypeStruct((B_HEADS, T, LANES), F32)],
        compiler_params=_params("parallel", "parallel"),
    )(q, k, v, ct)


def fox_bwd_q(name, q, k, v, ct, o, do, lse):
    def body(q_ref, k_ref, v_ref, c_ref, o_ref, do_ref, lse_ref, dq_ref, drow_ref):
        i = pl.program_id(1)
        qv = q_ref[...]
        dov = do_ref[...]
        lsev = lse_ref[:, 0:1]
        delta = jnp.sum(dov.astype(F32) * o_ref[...].astype(F32), axis=1, keepdims=True)
        n_full = lax.div(i, jnp.int32(2))

        def step(jb, carry, masked):
            dq, drow = carry
            start = pl.multiple_of(jb * TK, TK)
            kb = k_ref[pl.ds(start, TK), :]
            vb = v_ref[pl.ds(start, TK), :]
            s = _fox_scores(qv, kb, c_ref[pl.ds(jb, 1), :], i * TQ, jb * TK, masked)
            p = jnp.exp(s - lsev)
            dp = lax.dot_general(dov, vb, NT, preferred_element_type=F32)
            ds = p * (dp - delta)
            return (dq + jnp.dot(ds.astype(BF16), kb, preferred_element_type=F32),
                    drow + jnp.sum(ds, axis=1, keepdims=True))

        init = (jnp.zeros((TQ, HD), F32), jnp.zeros((TQ, 1), F32))
        carry = lax.fori_loop(0, n_full, lambda jb, c: step(jb, c, False), init)
        dq, drow = step(n_full, carry, True)
        dq_ref[...] = (dq * SCALE).astype(BF16)
        drow_ref[...] = jnp.broadcast_to(drow, (TQ, LANES))

    head = lambda h, i: (h, 0, 0)
    blk = pl.BlockSpec((None, TQ, HD), lambda h, i: (h, i, 0))
    stat = pl.BlockSpec((None, TQ, LANES), lambda h, i: (h, i, 0))
    return pl.pallas_call(
        body, name=name, grid=(B_HEADS, T // TQ),
        in_specs=[blk, pl.BlockSpec((None, T, HD), head), pl.BlockSpec((None, T, HD), head),
                  pl.BlockSpec((None, T // TK, TK), head), blk, blk, stat],
        out_specs=[blk, stat],
        out_shape=[jax.ShapeDtypeStruct((B_HEADS, T, HD), BF16), jax.ShapeDtypeStruct((B_HEADS, T, LANES), F32)],
        compiler_params=_params("parallel", "parallel"),
    )(q, k, v, ct, o, do, lse)


def fox_bwd_kv(name, q, k, v, ct, o, do, lse, prev):
    has_prev = prev is not None
    nq = T // TQ

    def body(*refs):
        q_ref, k_ref, v_ref, c_ref, o_ref, do_ref, lse_ref = refs[:7]
        dk_ref, dv_ref, dc_ref = refs[-3:]
        jb = pl.program_id(1)
        kb = k_ref[...]
        vb = v_ref[...]
        cj = c_ref[pl.ds(jb, 1), :]

        def step(i, carry, masked):
            dk, dv, dc = carry
            start = pl.multiple_of(i * TQ, TQ)
            qv = q_ref[pl.ds(start, TQ), :]
            dov = do_ref[pl.ds(start, TQ), :]
            ov = o_ref[pl.ds(start, TQ), :]
            lsev = lse_ref[pl.ds(start, TQ), 0:1]
            delta = jnp.sum(dov.astype(F32) * ov.astype(F32), axis=1, keepdims=True)
            s = _fox_scores(qv, kb, cj, i * TQ, jb * TK, masked)
            p = jnp.exp(s - lsev)
            dp = lax.dot_general(dov, vb, NT, preferred_element_type=F32)
            ds = p * (dp - delta)
            dv = dv + lax.dot_general(p.astype(BF16), dov, TN, preferred_element_type=F32)
            dk = dk + lax.dot_general(ds.astype(BF16), qv, TN, preferred_element_type=F32)
            dc = dc - jnp.sum(ds, axis=0, keepdims=True)
            return dk, dv, dc

        first = 2 * jb
        carry = (jnp.zeros((TK, HD), F32), jnp.zeros((TK, HD), F32), jnp.zeros((1, TK), F32))
        carry = step(first, carry, True)
        carry = step(first + 1, carry, True)
        dk, dv, dc = lax.fori_loop(first + 2, nq, lambda i, c: step(i, c, False), carry)
        dk = dk * SCALE
        if has_prev:
            dk = dk + refs[7][...]
            dv = dv + refs[8][...]
            dc = dc + refs[9][pl.ds(jb, 1), :]
        dk_ref[...] = dk
        dv_ref[...] = dv
        dc_ref[pl.ds(jb, 1), :] = dc

    head = lambda h, j: (h, 0, 0)
    full = pl.BlockSpec((None, T, HD), head)
    blk = pl.BlockSpec((None, TK, HD), lambda h, j: (h, j, 0))
    cspec = pl.BlockSpec((None, T // TK, TK), head)
    ins = [q, k, v, ct, o, do, lse] + (list(prev) if has_prev else [])
    in_specs = [full, blk, blk, cspec, full, full, pl.BlockSpec((None, T, LANES), head)] + ([blk, blk, cspec] if has_prev else [])
    return pl.pallas_call(
        body, name=name, grid=(B_HEADS, T // TK), in_specs=in_specs, out_specs=[blk, blk, cspec],
        out_shape=[jax.ShapeDtypeStruct((B_HEADS, T, HD), F32)] * 2 + [jax.ShapeDtypeStruct((B_HEADS, T // TK, TK), F32)],
        compiler_params=_params("parallel", "arbitrary"),
    )(*ins)


def _tri(upper):
    i = lax.broadcasted_iota(jnp.int32, (BLK, BLK), 0)
    j = lax.broadcasted_iota(jnp.int32, (BLK, BLK), 1)
    return ((i <= j) if upper else (i >= j)).astype(F32)


def gates_fwd(name, fgt, b_f):
    def body(f_ref, b_ref, c_ref):
        tri = _tri(True)
        carry = jnp.zeros((B_HEADS, 1), F32)
        for blk in range(T // BLK):
            sl = slice(blk * BLK, (blk + 1) * BLK)
            z = f_ref[:, sl] + b_ref[...]
            logf = jnp.minimum(z, 0.0) - jnp.log(1.0 + jnp.exp(-jnp.abs(z)))
            cs = jnp.dot(logf, tri, precision=HIGHEST, preferred_element_type=F32) + carry
            c_ref[:, sl] = cs
            carry = cs[:, BLK - 1:BLK]

    return pl.pallas_call(
        body, name=name, out_shape=jax.ShapeDtypeStruct((B_HEADS, T), F32), compiler_params=_params(),
    )(fgt, b_f)


def gates_bwd(name, fgt, b_f, dcs):
    n_dc = len(dcs)

    def body(*refs):
        f_ref, b_ref = refs[:2]
        dc_refs = refs[2:2 + n_dc]
        dz_ref, db_ref = refs[2 + n_dc:]
        tri = _tri(False)
        carry = jnp.zeros((B_HEADS, 1), F32)
        db = jnp.zeros((B_HEADS, 1), F32)
        for blk in reversed(range(T // BLK)):
            sl = slice(blk * BLK, (blk + 1) * BLK)
            dc = dc_refs[0][:, sl]
            for r in dc_refs[1:]:
                dc = dc + r[:, sl]
            rc = jnp.dot(dc, tri, precision=HIGHEST, preferred_element_type=F32) + carry
            carry = rc[:, 0:1]
            z = f_ref[:, sl] + b_ref[...]
            e = jnp.exp(-jnp.abs(z))
            dz = rc * jnp.where(z >= 0.0, e, 1.0) / (1.0 + e)
            dz_ref[:, sl] = dz
            db = db + jnp.sum(dz, axis=1, keepdims=True)
        db_ref[...] = db

    return pl.pallas_call(
        body, name=name,
        out_shape=[jax.ShapeDtypeStruct((B_HEADS, T), F32), jax.ShapeDtypeStruct((B_HEADS, 1), F32)],
        compiler_params=_params(),
    )(fgt, b_f, *dcs)


CONV_TB = 256
GELU_K = math.sqrt(2.0 / math.pi)
GELU_C = 0.044715


def _shift_down(x, halo_ref, n):
    rows = lax.broadcasted_iota(jnp.int32, x.shape, 0)
    y = pltpu.roll(x, n, 0)
    for k in range(n):
        y = jnp.where(rows == k, halo_ref[pl.ds(8 - n + k, 1), :], y)
    return y


def _conv(x, halo_ref, cw_ref, cb_ref, first):
    x1 = _shift_down(x, halo_ref, 1)
    x2 = _shift_down(x, halo_ref, 2)
    rows = lax.broadcasted_iota(jnp.int32, x.shape, 0)
    x1 = jnp.where(first & (rows < 1), 0.0, x1)
    x2 = jnp.where(first & (rows < 2), 0.0, x2)
    y = x2 * cw_ref[0:1, :] + x1 * cw_ref[1:2, :] + x * cw_ref[2:3, :] + cb_ref[...]
    return y, x1, x2


def _gelu_parts(x):
    th = jnp.tanh(GELU_K * (x + GELU_C * x * x * x))
    val = 0.5 * x * (1.0 + th)
    grad = 0.5 * (1.0 + th) + 0.5 * x * (1.0 - th * th) * GELU_K * (1.0 + 3.0 * GELU_C * x * x)
    return val, grad


def _conv_specs(tb):
    def slab(off):
        return pl.BlockSpec((None, tb, SLAB), lambda d, i: (d + off, i, 0))

    def halo(off):
        return pl.BlockSpec((None, 8, SLAB), lambda d, i: (d + off, jnp.maximum(i * (tb // 8) - 1, 0), 0))

    def par(rows, off):
        return pl.BlockSpec((None, rows, SLAB), lambda d, i: (d + off, 0, 0))

    return slab, halo, par


def convglu_fwd(name, a, cw, cb, tb=CONV_TB):
    slab, halo, par = _conv_specs(tb)

    def body(ag, hg, av, hv, cwg, cbg, cwv, cbv, u_ref):
        first = pl.program_id(1) == 0
        gate, _, _ = _conv(ag[...], hg, cwg, cbg, first)
        val, _, _ = _conv(av[...], hv, cwv, cbv, first)
        u_ref[...] = (_gelu_parts(gate)[0] * val).astype(BF16)

    return pl.pallas_call(
        body, name=name, grid=(4, T // tb),
        in_specs=[slab(0), halo(0), slab(4), halo(4), par(3, 0), par(1, 0), par(3, 4), par(1, 4)],
        out_specs=slab(0), out_shape=jax.ShapeDtypeStruct((4, T, SLAB), BF16),
        compiler_params=_params("parallel", "parallel"),
    )(a, a, a, a, cw, cb, cw, cb)


def convglu_bwd(name, du, a, cw, cb, tb=CONV_TB):
    slab, halo, par = _conv_specs(tb)

    def body(du_ref, ag, hg, av, hv, cwg, cbg, cwv, cbv, dg_ref, dv_ref, dcwg, dcbg, dcwv, dcbv):
        first = pl.program_id(1) == 0
        gate, g1, g2 = _conv(ag[...], hg, cwg, cbg, first)
        val, v1, v2 = _conv(av[...], hv, cwv, cbv, first)
        act, dact = _gelu_parts(gate)
        duv = du_ref[...].astype(F32)
        dgate = duv * val * dact
        dval = duv * act
        dg_ref[...] = dgate
        dv_ref[...] = dval
        for dy, xs, dcw_ref, dcb_ref in ((dgate, (g2, g1, ag[...]), dcwg, dcbg), (dval, (v2, v1, av[...]), dcwv, dcbv)):
            parts = [jnp.sum(dy * x, axis=0, keepdims=True) for x in xs]
            bias = jnp.sum(dy, axis=0, keepdims=True)

            @pl.when(first)
            def _():
                for k in range(3):
                    dcw_ref[k:k + 1, :] = parts[k]
                dcb_ref[...] = bias

            @pl.when(jnp.logical_not(first))
            def _():
                for k in range(3):
                    dcw_ref[k:k + 1, :] += parts[k]
                dcb_ref[...] += bias

    res = pl.pallas_call(
        body, name=name, grid=(4, T // tb),
        in_specs=[slab(0), slab(0), halo(0), slab(4), halo(4), par(3, 0), par(1, 0), par(3, 4), par(1, 4)],
        out_specs=[slab(0), slab(0), par(3, 0), par(1, 0), par(3, 0), par(1, 0)],
        out_shape=[jax.ShapeDtypeStruct((4, T, SLAB), F32)] * 2
        + [jax.ShapeDtypeStruct((4, 3, SLAB), F32), jax.ShapeDtypeStruct((4, 1, SLAB), F32)] * 2,
        compiler_params=_params("parallel", "arbitrary"),
    )(du, a, a, a, a, cw, cb, cw, cb)
    dgate, dval, dcwg, dcbg, dcwv, dcbv = res
    return (dgate, dval), jnp.concatenate([dcwg, dcwv], axis=0), jnp.concatenate([dcbg, dcbv], axis=0)


def conv_bwd_input(name, dac_pair, cw, tb=CONV_TB):
    nblk = T // tb

    def run(x, off):
        def body(x_ref, nx_ref, cw_ref, da_ref):
            last = pl.program_id(1) == nblk - 1
            xv = x_ref[...]
            rows = lax.broadcasted_iota(jnp.int32, xv.shape, 0)

            def up(n):
                y = pltpu.roll(xv, tb - n, 0)
                for k in range(n):
                    y = jnp.where(rows == tb - n + k, nx_ref[pl.ds(k, 1), :], y)
                return jnp.where(last & (rows >= tb - n), 0.0, y)

            da_ref[...] = (xv * cw_ref[2:3, :] + up(1) * cw_ref[1:2, :] + up(2) * cw_ref[0:1, :]).astype(BF16)

        return pl.pallas_call(
            body, name=f"{name}_{off}", grid=(4, nblk),
            in_specs=[pl.BlockSpec((None, tb, SLAB), lambda d, i: (d, i, 0)),
                      pl.BlockSpec((None, 8, SLAB), lambda d, i: (d, jnp.minimum((i + 1) * (tb // 8), T // 8 - 1), 0)),
                      pl.BlockSpec((None, 3, SLAB), lambda d, i: (d + off, 0, 0))],
            out_specs=pl.BlockSpec((None, tb, SLAB), lambda d, i: (d, i, 0)),
            out_shape=jax.ShapeDtypeStruct((4, T, SLAB), BF16), compiler_params=_params("parallel", "parallel"),
        )(x, x, cw)

    return jnp.concatenate([run(dac_pair[0], 0), run(dac_pair[1], 4)], axis=0)


def loss_head(name, y, target, tb=256):
    row = pl.BlockSpec((tb, D), lambda i: (i, 0))

    def body(y_ref, t_ref, dy_ref, loss_ref):
        diff = y_ref[...] - t_ref[...]
        dy_ref[...] = diff * (1.0 / D)
        part = jnp.sum(jnp.sum(diff * diff, axis=1, keepdims=True), axis=0, keepdims=True) * (0.5 / D)

        @pl.when(pl.program_id(0) == 0)
        def _():
            loss_ref[...] = part

        @pl.when(pl.program_id(0) > 0)
        def _():
            loss_ref[...] += part

    return pl.pallas_call(
        body, name=name, grid=(T // tb,), in_specs=[row, row],
        out_specs=[row, pl.BlockSpec((1, 1), lambda i: (0, 0))],
        out_shape=[jax.ShapeDtypeStruct((T, D), F32), jax.ShapeDtypeStruct((1, 1), F32)],
        compiler_params=_params("arbitrary"),
    )(y, target)


def _row_tile(rows, cols, bytes_per_elem, budget=6 * 1024 * 1024):
    for tr in (1024, 512, 256, 128, 64, 32, 16, 8):
        if rows % tr == 0 and tr * cols * bytes_per_elem <= budget:
            return tr
    return rows


def adamw(name, parts, w, m, v):
    n_parts, rows, cols = parts.shape
    tr = _row_tile(rows, cols, n_parts * parts.dtype.itemsize + 28)
    blk = pl.BlockSpec((tr, cols), lambda i: (i, 0))
    b1c = 1.0 - ADAM_B1 ** ADAM_STEP
    b2c = 1.0 - ADAM_B2 ** ADAM_STEP

    def body(p_ref, w_ref, m_ref, v_ref, g_ref, d_ref, nm_ref, nv_ref):
        g = p_ref[0].astype(F32)
        for k in range(1, n_parts):
            g = g + p_ref[k].astype(F32)
        nm = ADAM_B1 * m_ref[...] + (1.0 - ADAM_B1) * g
        nv = ADAM_B2 * v_ref[...] + (1.0 - ADAM_B2) * (g * g)
        g_ref[...] = g
        nm_ref[...] = nm
        nv_ref[...] = nv
        d_ref[...] = -ADAM_LR * ((nm / b1c) / (jnp.sqrt(nv / b2c) + ADAM_EPS) + ADAM_WD * w_ref[...])

    return pl.pallas_call(
        body, name=name, grid=(rows // tr,),
        in_specs=[pl.BlockSpec((n_parts, tr, cols), lambda i: (0, i, 0)), blk, blk, blk], out_specs=[blk] * 4,
        out_shape=[jax.ShapeDtypeStruct((rows, cols), F32)] * 4, compiler_params=_params("parallel"),
    )(parts, w, m, v)


def exchange(name, items, scatter):
    n = len(items)
    hbm = pl.BlockSpec(memory_space=pltpu.HBM)

    def body(*refs):
        ins, outs = refs[:n], refs[n:2 * n]
        send_sems, recv_sems, local_sems = refs[2 * n:]
        x, y, c = lax.axis_index("x"), lax.axis_index("y"), lax.axis_index("c")
        me = 4 * x + 2 * y + c
        copies = []
        for t in range(n):
            own = pltpu.make_async_copy(ins[t].at[me] if scatter else ins[t], outs[t].at[me], local_sems.at[t])
            own.start()
            copies.append(own)
            for rel in range(1, NDEV):
                px = 1 - x if rel & 4 else x
                py = 1 - y if rel & 2 else y
                pc = 1 - c if rel & 1 else c
                src = ins[t].at[4 * px + 2 * py + pc] if scatter else ins[t]
                cp = pltpu.make_async_remote_copy(
                    src_ref=src, dst_ref=outs[t].at[me], send_sem=send_sems.at[t, rel - 1],
                    recv_sem=recv_sems.at[t, rel - 1], device_id=(px, py, pc), device_id_type=pl.DeviceIdType.MESH)
                cp.start()
                copies.append(cp)
        for cp in copies:
            cp.wait()

    out_shape = [jax.ShapeDtypeStruct(it.shape if scatter else (NDEV,) + it.shape, it.dtype) for it in items]
    return pl.pallas_call(
        body, name=name, in_specs=[hbm] * n, out_specs=[hbm] * n, out_shape=out_shape,
        scratch_shapes=[pltpu.SemaphoreType.DMA((n, NDEV - 1)), pltpu.SemaphoreType.DMA((n, NDEV - 1)),
                        pltpu.SemaphoreType.DMA((n,))],
    )(*items)


def sum_slots(name, parts):
    _, rows, cols = parts.shape

    def body(p_ref, o_ref):
        s = p_ref[0]
        for k in range(1, NDEV):
            s = s + p_ref[k]
        o_ref[...] = s

    return pl.pallas_call(body, name=name, out_shape=jax.ShapeDtypeStruct((rows, cols), F32), compiler_params=_params())(parts)


def _heads(t):
    return t.reshape(T, B_HEADS, HD).transpose(1, 0, 2)


def _unheads(t):
    return t.transpose(1, 0, 2).reshape(T, B_HEADS * HD)


def _cols_from_slots(g):
    return g.transpose(1, 0, 2).reshape(g.shape[1], NDEV * g.shape[2])


def _slots_from_cols(w):
    return w.reshape(w.shape[0], NDEV, w.shape[1] // NDEV).transpose(1, 0, 2)


def _pack(arrays, rows):
    flat = jnp.concatenate([a.reshape(-1).astype(F32) for a in arrays])
    return jnp.pad(flat, (0, rows * LANES - flat.shape[0])).reshape(rows, LANES)


def _unpack(buf, shapes):
    flat = buf.reshape(-1)
    out, pos = [], 0
    for sh in shapes:
        size = math.prod(sh)
        out.append(flat[pos:pos + size].reshape(sh))
        pos += size
    return out


def kernel(x, norm_gains, w_qkv_a, w_o_a, w_q_b, w_o_b, kv_norm, w_kvf, b_f, w_up, conv_w, conv_b, w_down, loss_target, m_norm_gains, m_w_qkv_a, m_w_o_a, m_w_q_b, m_w_o_b, m_kv_norm, m_w_kvf, m_b_f, m_w_up, m_conv_w, m_conv_b, m_w_down, v_norm_gains, v_w_qkv_a, v_w_o_a, v_w_q_b, v_w_o_b, v_kv_norm, v_w_kvf, v_b_f, v_w_up, v_conv_w, v_conv_b, v_w_down):
    me = 4 * lax.axis_index("x") + 2 * lax.axis_index("y") + lax.axis_index("c")
    n_b = DEPTH - N_A

    big = [w_qkv_a, w_o_a, w_q_b, w_o_b, w_kvf, w_up, w_down]
    gathered = exchange("gather_weights", [w.astype(BF16) for w in big] + [norm_gains, conv_w], scatter=False)
    g_qkv, g_oa, g_qb, g_ob, g_kvf, g_up, g_down, g_gains, g_cw = gathered
    wqkv = [_cols_from_slots(g_qkv[:, l]) for l in range(N_A)]
    woa = [_cols_from_slots(g_oa[:, l]) for l in range(N_A)]
    wqb = [g_qb[:, l].reshape(D, D) for l in range(n_b)]
    wob = [g_ob[:, l].reshape(D, D) for l in range(n_b)]
    wkvf = jnp.pad(_cols_from_slots(g_kvf), ((0, 0), (0, KVF_PAD - KVF)))
    wup = [g_up[:, l] for l in range(DEPTH)]
    wdown = [g_down[:, l].reshape(4, SLAB, D) for l in range(DEPTH)]
    gains = g_gains.transpose(1, 2, 0, 3).reshape(DEPTH, 4, D)
    cws = [g_cw[:, l] for l in range(DEPTH)]
    cbs = [conv_b[l].reshape(NDEV, 1, SLAB) for l in range(DEPTH)]
    tables = rope_tables()
    b_col = b_f.reshape(B_HEADS, 1)

    h = x.reshape(T, D)
    _, (xn,) = resid_norm("norm_in", h, None, None, [gains[0, 0]])
    saved = []
    shared = None
    for l in range(DEPTH):
        s = {"h": h, "xn": xn}
        if l < N_A:
            qkv = matmul(f"qkv{l}", xn, wqkv[l], tm=1024, tn=768)
            s["qkvr"] = rope(f"rope{l}", qkv, tables, inverse=False)
            os_, lses = [], []
            for g, (_, r) in enumerate(A_GROUPS):
                o, lse = attn_a_fwd(f"attn_a{l}_{g}", s["qkvr"], g, r)
                os_.append(o)
                lses.append(lse)
            s["o"], s["lse"] = os_, lses
            s["oc"] = combine_fwd(f"combine{l}", os_, lses)
            mix = matmul(f"wo_a{l}", s["oc"], woa[l], tm=1024, tn=512)
        else:
            if l == N_A:
                kvf = matmul("kvf", xkv, wkvf, tm=1024, tn=768)
                k_h = _heads(kvf[:, :D].astype(BF16))
                v_h = _heads(kvf[:, D:2 * D].astype(BF16))
                fgt = kvf[:, 2 * D:KVF].T
                ct = gates_fwd("gates", fgt, b_col).reshape(B_HEADS, T // TK, TK)
                shared = {"xkv": xkv, "k": k_h, "v": v_h, "fgt": fgt, "ct": ct, "h": h}
            j = l - N_A
            s["q"] = _heads(matmul(f"wq_b{j}", xn, wqb[j], out_dtype=BF16, tm=1024, tn=512))
            s["o"], s["lse"] = fox_fwd(f"fox{j}", s["q"], shared["k"], shared["v"], shared["ct"])
            s["oc"] = _unheads(s["o"])
            mix = matmul(f"wo_b{j}", s["oc"], wob[j], tm=1024, tn=512)
        s["mix"] = mix
        s["h1"], (s["xn2"],) = resid_norm(f"norm_mid{l}", h, mix, gains[l, 1], [gains[l, 2]])
        s["a"] = matmul(f"up{l}", s["xn2"], wup[l], tm=1024, tn=SLAB, batch="b_out")
        s["u"] = convglu_fwd(f"convglu{l}", s["a"], cws[l], cbs[l])
        s["f"] = matmul(f"down{l}", s["u"], wdown[l], tm=1024, tn=512, batch="reduce")
        nxt = [gains[l + 1, 0]] if l + 1 < DEPTH else []
        if l == N_A - 1:
            nxt.append(kv_norm)
        h, normed = resid_norm(f"norm_out{l}", s["h1"], s["f"], gains[l, 3], nxt)
        if l + 1 < DEPTH:
            xn = normed[0]
        if l == N_A - 1:
            xkv = normed[1]
        saved.append(s)

    dh, loss_part = loss_head("loss", h, loss_target.reshape(T, D))

    d_gains = [[None] * 4 for _ in range(DEPTH)]
    d_cw, d_cb = [None] * DEPTH, [None] * DEPTH
    gw = {"qkv": [None] * N_A, "oa": [None] * N_A, "qb": [None] * n_b, "ob": [None] * n_b, "up": [None] * DEPTH,
          "down": [None] * DEPTH}
    kv_acc = None
    d_rows = []
    for l in reversed(range(DEPTH)):
        s = saved[l]
        df, d_gains[l][3] = rms_bwd(f"bwd_norm_out{l}", s["f"], gains[l, 3], dh, out_dtype=BF16)
        du = matmul(f"bwd_down_x{l}", df, wdown[l], tb=True, out_dtype=BF16, tm=1024, tn=SLAB, batch="b_out")
        gw["down"][l] = matmul(f"bwd_down_w{l}", s["u"], df, ta=True, out_dtype=BF16, tm=SLAB, tn=512, batch="a_out")
        dac, d_cw[l], d_cb[l] = convglu_bwd(f"bwd_convglu{l}", du, s["a"], cws[l], cbs[l])
        da = conv_bwd_input(f"bwd_conv{l}", dac, cws[l])
        dxn2 = matmul(f"bwd_up_x{l}", da, wup[l], tb=True, tm=1024, tn=512, batch="reduce")
        gw["up"][l] = matmul(f"bwd_up_w{l}", s["xn2"], da, ta=True, out_dtype=BF16, tm=512, tn=SLAB, batch="b_out")
        dh1, d_gains[l][2] = rms_bwd(f"bwd_norm_mid{l}", s["h1"], gains[l, 2], dxn2, add=dh)
        dmix, d_gains[l][1] = rms_bwd(f"bwd_norm_mix{l}", s["mix"], gains[l, 1], dh1, out_dtype=BF16)
        if l < N_A:
            doc = matmul(f"bwd_wo_a_x{l}", dmix, woa[l], tb=True, tm=1024, tn=A_W)
            gw["oa"][l] = matmul(f"bwd_wo_a_w{l}", s["oc"], dmix, ta=True, out_dtype=BF16, tm=A_W, tn=512)
            dos, dds = combine_bwd(f"bwd_combine{l}", doc, s["o"], s["lse"])
            cols = [None] * 9
            for g, (_, r) in enumerate(A_GROUPS):
                dq, dk, dv = attn_a_bwd(f"bwd_attn_a{l}_{g}", s["qkvr"], dos[g], s["lse"][g], dds[g], g, r)
                cols[g], cols[3 + g], cols[6 + g] = dq, dk, dv
            dqkv = rope(f"bwd_rope{l}", jnp.concatenate(cols, axis=1), tables, inverse=True)
            dxn = matmul(f"bwd_qkv_x{l}", dqkv, wqkv[l], tb=True, tm=1024, tn=512)
            gw["qkv"][l] = matmul(f"bwd_qkv_w{l}", s["xn"], dqkv, ta=True, out_dtype=BF16, tm=512, tn=768)
        else:
            j = l - N_A
            do = _heads(matmul(f"bwd_wo_b_x{j}", dmix, wob[j], tb=True, out_dtype=BF16, tm=1024, tn=512))
            gw["ob"][j] = matmul(f"bwd_wo_b_w{j}", s["oc"], dmix, ta=True, out_dtype=BF16, tm=512, tn=512)
            args = (s["q"], shared["k"], shared["v"], shared["ct"], s["o"], do, s["lse"])
            dq_h, drow = fox_bwd_q(f"bwd_fox_q{j}", *args)
            dq = _unheads(dq_h)
            d_rows.append(drow[:, :, 0])
            kv_acc = fox_bwd_kv(f"bwd_fox_kv{j}", *args, kv_acc)
            dxn = matmul(f"bwd_wq_b_x{j}", dq, wqb[j], tb=True, tm=1024, tn=512)
            gw["qb"][j] = matmul(f"bwd_wq_b_w{j}", s["xn"], dq, ta=True, out_dtype=BF16, tm=512, tn=512)
        dh, d_gains[l][0] = rms_bwd(f"bwd_norm_in{l}", s["h"], gains[l, 0], dxn, add=dh1)
        if l == N_A:
            dk_h, dv_h, dct = kv_acc
            dfgt, d_bf = gates_bwd("bwd_gates", shared["fgt"], b_col, [dct.reshape(B_HEADS, T)] + d_rows)
            dkvf = jnp.concatenate(
                [_unheads(dk_h).astype(BF16), _unheads(dv_h).astype(BF16), dfgt.T.astype(BF16),
                 jnp.zeros((T, KVF_PAD - KVF), BF16)], axis=1)
            dxkv = matmul("bwd_kvf_x", dkvf, wkvf, tb=True, tm=1024, tn=512)
            g_kvf_full = matmul("bwd_kvf_w", shared["xkv"], dkvf, ta=True, out_dtype=BF16, tm=512, tn=768)
            dh, d_kvn = rms_bwd("bwd_norm_kv", shared["h"], kv_norm, dxkv, add=dh)

    small_shapes = [(DEPTH, 4, D), (D,), (B_HEADS,), (DEPTH, 3, NDEV * SLAB), (DEPTH, NDEV * SLAB), (1,)]
    small = [
        jnp.stack([jnp.concatenate(row, axis=0) for row in d_gains]),
        d_kvn, d_bf,
        jnp.stack([d.transpose(1, 0, 2).reshape(3, NDEV * SLAB) for d in d_cw]),
        jnp.stack([d.reshape(NDEV * SLAB) for d in d_cb]),
        loss_part,
    ]
    small_rows = 848
    (small_all,) = exchange("gather_small_grads", [_pack(small, small_rows)], scatter=False)
    g_gains_full, g_kvn, g_bf, g_cw_full, g_cb, loss = _unpack(sum_slots("sum_small", small_all), small_shapes)
    g_gains_mine = lax.dynamic_slice_in_dim(g_gains_full, me * (D // NDEV), D // NDEV, axis=2)
    g_cw_mine = lax.dynamic_slice_in_dim(g_cw_full, me * SLAB, SLAB, axis=2)

    small_w = [norm_gains, kv_norm, b_f, conv_w, conv_b]
    small_m = [m_norm_gains, m_kv_norm, m_b_f, m_conv_w, m_conv_b]
    small_v = [v_norm_gains, v_kv_norm, v_b_f, v_conv_w, v_conv_b]
    small_g = [g_gains_mine, g_kvn, g_bf, g_cw_mine, g_cb]
    shapes = [w.shape for w in small_w]
    rows = 320
    res = adamw("adamw_small", _pack(small_g, rows)[None], _pack(small_w, rows), _pack(small_m, rows), _pack(small_v, rows))
    _, s_delta, s_m, s_v = [_unpack(r, shapes) for r in res]

    parts = [
        jnp.stack([_slots_from_cols(g) for g in gw["qkv"]], axis=1),
        jnp.stack([_slots_from_cols(g) for g in gw["oa"]], axis=1),
        jnp.stack([g.reshape(NDEV, D // NDEV, D) for g in gw["qb"]], axis=1),
        jnp.stack([g.reshape(NDEV, D // NDEV, D) for g in gw["ob"]], axis=1),
        _slots_from_cols(g_kvf_full[:, :KVF]),
        jnp.stack(gw["up"], axis=1),
        jnp.stack([g.reshape(NDEV, DFF // NDEV, D) for g in gw["down"]], axis=1),
    ]
    received = exchange("scatter_grads", parts, scatter=True)
    big_m = [m_w_qkv_a, m_w_o_a, m_w_q_b, m_w_o_b, m_w_kvf, m_w_up, m_w_down]
    big_v = [v_w_qkv_a, v_w_o_a, v_w_q_b, v_w_o_b, v_w_kvf, v_w_up, v_w_down]
    big_out = []
    for t, (w, m, v, rec) in enumerate(zip(big, big_m, big_v, received)):
        cols = w.shape[-1]
        flat = lambda a: a.reshape(-1, cols)
        res = adamw(f"adamw{t}", rec.reshape(NDEV, -1, cols), flat(w), flat(m), flat(v))
        big_out.append([r.reshape(w.shape) for r in res])

    def pick(k):
        b = [o[k] for o in big_out]
        sm = {0: small_g, 1: s_delta, 2: s_m, 3: s_v}[k]
        return [sm[0], b[0], b[1], b[2], b[3], sm[1], b[4], sm[2], b[5], sm[3], sm[4], b[6]]

    return (loss.reshape(()), dh.reshape(1, T, D), *pick(0), *pick(1), *pick(2), *pick(3))
```

```python
import functools
import math

import jax
import jax.numpy as jnp
from jax import lax
from jax.experimental import pallas as pl
from jax.experimental.pallas import tpu as pltpu

F32 = jnp.float32
BF16 = jnp.bfloat16

T = 2048
D = 1024
DEPTH = 4
N_A = 2
HD = 64
A_GROUPS = ((128, 1), (512, 4), (2048, 16))
A_W = 768
B_HEADS = 16
DFF = 2816
NDEV = 8
SLAB = 2 * DFF // NDEV
KVF = 2 * D + B_HEADS
KVF_PAD = 2304
ROPE_DIM = 16
ROPE_THETA = 500000.0
EPS = 1e-6
NEG = -1e30
BLK = 128
LANES = 128
SCALE = HD ** -0.5
VMEM_LIMIT = 56 * 1024 * 1024

ADAM_LR = 0.001
ADAM_B1 = 0.9
ADAM_B2 = 0.999
ADAM_EPS = 1e-08
ADAM_WD = 0.01
ADAM_STEP = 10
HIGHEST = lax.Precision.HIGHEST


def _params(*sem):
    return pltpu.CompilerParams(dimension_semantics=sem or None, vmem_limit_bytes=VMEM_LIMIT)


def _bf(x):
    return x if x.dtype == BF16 else x.astype(BF16)


def matmul(name, a, b, *, ta=False, tb=False, out_dtype=F32, tm=512, tn=512, batch=None):
    a_b = batch in ("a_out", "reduce")
    b_b = batch in ("b_out", "reduce")
    o_b = batch in ("a_out", "b_out")
    nb = a.shape[0] if a_b else (b.shape[0] if b_b else 1)
    ash = a.shape[1:] if a_b else a.shape
    bsh = b.shape[1:] if b_b else b.shape
    m, k = (ash[1], ash[0]) if ta else ash
    k2, n = (bsh[1], bsh[0]) if tb else bsh
    assert k == k2, (name, a.shape, b.shape)
    tm, tn = min(tm, m), min(tn, n)
    assert m % tm == 0 and n % tn == 0, (name, m, n, tm, tn)
    nbr = nb if batch == "reduce" else 1
    grid = (nb if o_b else 1, n // tn, m // tm, nbr)

    def bidx(bo, br):
        return bo if o_b else br

    def spec(batched, block, idx):
        if batched:
            return pl.BlockSpec((None,) + block, lambda bo, j, i, br: (bidx(bo, br),) + idx(i, j))
        return pl.BlockSpec(block, lambda bo, j, i, br: idx(i, j))

    a_spec = spec(a_b, (k, tm) if ta else (tm, k), (lambda i, j: (0, i)) if ta else (lambda i, j: (i, 0)))
    b_spec = spec(b_b, (tn, k) if tb else (k, tn), (lambda i, j: (j, 0)) if tb else (lambda i, j: (0, j)))
    o_spec = spec(o_b, (tm, tn), lambda i, j: (i, j))
    dims = (((0 if ta else 1,), (1 if tb else 0,)), ((), ()))

    def body(a_ref, b_ref, o_ref, *acc):
        p = lax.dot_general(_bf(a_ref[...]), _bf(b_ref[...]), dims, preferred_element_type=F32)
        if nbr == 1:
            o_ref[...] = p.astype(out_dtype)
        else:
            r = pl.program_id(3)

            @pl.when(r == 0)
            def _():
                acc[0][...] = p

            @pl.when(r > 0)
            def _():
                acc[0][...] += p

            @pl.when(r == nbr - 1)
            def _():
                o_ref[...] = acc[0][...].astype(out_dtype)

    out_shape = ((nb,) if o_b else ()) + (m, n)
    return pl.pallas_call(
        body, name=name, grid=grid, in_specs=[a_spec, b_spec], out_specs=o_spec,
        out_shape=jax.ShapeDtypeStruct(out_shape, out_dtype),
        scratch_shapes=[pltpu.VMEM((tm, tn), F32)] if nbr > 1 else [],
        compiler_params=_params("parallel", "parallel", "parallel", "arbitrary"),
    )(a, b)


def _rms(x, g):
    return x * lax.rsqrt(jnp.mean(x * x, axis=-1, keepdims=True) + EPS) * g


def resid_norm(name, h, y, gy, gains, tb=256):
    n_g = len(gains)
    has_y = y is not None
    row = pl.BlockSpec((tb, D), lambda i: (i, 0))
    vec = pl.BlockSpec((1, D), lambda i: (0, 0))

    def body(*refs):
        h_ref = refs[0]
        pos = 1
        hn = h_ref[...]
        if has_y:
            hn = hn + _rms(refs[1][...], refs[2][...])
            pos = 3
        g_refs = refs[pos:pos + n_g]
        outs = refs[pos + n_g:]
        if has_y:
            outs[0][...] = hn
            outs = outs[1:]
        for g_ref, o_ref in zip(g_refs, outs):
            o_ref[...] = _rms(hn, g_ref[...]).astype(BF16)

    ins = [h] + ([y, gy.reshape(1, D)] if has_y else []) + [g.reshape(1, D) for g in gains]
    in_specs = [row] + ([row, vec] if has_y else []) + [vec] * n_g
    out_shape = ([jax.ShapeDtypeStruct((T, D), F32)] if has_y else []) + [jax.ShapeDtypeStruct((T, D), BF16)] * n_g
    res = pl.pallas_call(
        body, name=name, grid=(T // tb,), in_specs=in_specs, out_specs=[row] * len(out_shape),
        out_shape=out_shape, compiler_params=_params("parallel"),
    )(*ins)
    return (res[0], list(res[1:])) if has_y else (h, list(res))


def rms_bwd(name, x, g, dy, add=None, out_dtype=F32, tb=256):
    has_add = add is not None
    row = pl.BlockSpec((tb, D), lambda i: (i, 0))
    vec = pl.BlockSpec((1, D), lambda i: (0, 0))

    def body(*refs):
        x_ref, g_ref, dy_ref = refs[:3]
        dx_ref, dg_ref = refs[-2:]
        xv = x_ref[...]
        dyv = dy_ref[...].astype(F32)
        r = lax.rsqrt(jnp.mean(xv * xv, axis=-1, keepdims=True) + EPS)
        gdy = dyv * g_ref[...]
        dx = r * gdy - xv * (r * r * r * jnp.mean(xv * gdy, axis=-1, keepdims=True))
        if has_add:
            dx = dx + refs[3][...]
        dx_ref[...] = dx.astype(out_dtype)
        part = jnp.sum(dyv * xv * r, axis=0, keepdims=True)

        @pl.when(pl.program_id(0) == 0)
        def _():
            dg_ref[...] = part

        @pl.when(pl.program_id(0) > 0)
        def _():
            dg_ref[...] += part

    ins = [x, g.reshape(1, D), dy] + ([add] if has_add else [])
    return pl.pallas_call(
        body, name=name, grid=(T // tb,), in_specs=[row, vec, row] + ([row] if has_add else []),
        out_specs=[row, vec],
        out_shape=[jax.ShapeDtypeStruct((T, D), out_dtype), jax.ShapeDtypeStruct((1, D), F32)],
        compiler_params=_params("arbitrary"),
    )(*ins)


def rope_tables():
    pos = jnp.arange(T, dtype=F32)
    inv = ROPE_THETA ** (-jnp.arange(0, ROPE_DIM, 2, dtype=F32) / ROPE_DIM)
    ang = pos[:, None] * inv[None, :]
    cos, sin = jnp.cos(ang), jnp.sin(ang)
    half = ROPE_DIM // 2
    one = jnp.ones((T, HD - ROPE_DIM), F32)
    zero = jnp.zeros((T, HD - ROPE_DIM), F32)
    zh = jnp.zeros((T, half), F32)
    c = jnp.concatenate([cos, cos, one], axis=1)
    s_up = jnp.concatenate([zh, sin, zero], axis=1)
    s_dn = jnp.concatenate([-sin, zh, zero], axis=1)
    rep = LANES // HD
    return tuple(jnp.tile(t, (1, rep)) for t in (c, s_up, s_dn))


def rope(name, t, tables, inverse, out_dtype=BF16, tb=512):
    c, s_up, s_dn = tables
    n_rot = 2 * A_W // LANES
    half = ROPE_DIM // 2
    blk = pl.BlockSpec((tb, LANES), lambda i, j: (i, j))
    tab = pl.BlockSpec((tb, LANES), lambda i, j: (i, 0))

    def body(t_ref, c_ref, su_ref, sd_ref, o_ref):
        x = t_ref[...].astype(F32)
        sgn = -1.0 if inverse else 1.0
        rot = (x * c_ref[...] + pltpu.roll(x, half, 1) * (sgn * su_ref[...])
               + pltpu.roll(x, LANES - half, 1) * (sgn * sd_ref[...]))
        o_ref[...] = jnp.where(pl.program_id(1) < n_rot, rot, x).astype(out_dtype)

    return pl.pallas_call(
        body, name=name, grid=(T // tb, 3 * A_W // LANES), in_specs=[blk, tab, tab, tab], out_specs=blk,
        out_shape=jax.ShapeDtypeStruct((T, 3 * A_W), out_dtype), compiler_params=_params("parallel", "parallel"),
    )(t, c, s_up, s_dn)


GW = 4 * HD


def _band_mask(b):
    qi = lax.broadcasted_iota(jnp.int32, (BLK, 2 * BLK), 0)
    kj = lax.broadcasted_iota(jnp.int32, (BLK, 2 * BLK), 1)
    return (kj <= qi + BLK) & (kj >= qi) & ((kj >= BLK) | (b > 0))


def attn_a_fwd(name, qkvr, g, r):
    length = T // r
    nblk = length // BLK
    view = qkvr.reshape(length, r * 3 * A_W)
    ncol = 3 * A_W // GW

    def col(section, prev):
        def idx(j, b):
            return (jnp.maximum(b - 1, 0) if prev else b, j * ncol + 3 * section + g)
        return pl.BlockSpec((BLK, GW), idx)

    out = pl.BlockSpec((BLK, GW), lambda j, b: (b, j))

    def body(q_ref, kp_ref, kc_ref, vp_ref, vc_ref, o_ref, lse_ref):
        mask = _band_mask(pl.program_id(1))
        k2 = jnp.concatenate([kp_ref[...], kc_ref[...]], axis=0)
        v2 = jnp.concatenate([vp_ref[...], vc_ref[...]], axis=0)
        q = q_ref[...]
        for h in range(4):
            sl = slice(h * HD, (h + 1) * HD)
            s = lax.dot_general(q[:, sl], k2[:, sl], (((1,), (1,)), ((), ())), preferred_element_type=F32) * SCALE
            s = jnp.where(mask, s, NEG)
            m = jnp.max(s, axis=1, keepdims=True)
            p = jnp.exp(s - m)
            l = jnp.sum(p, axis=1, keepdims=True)
            o_ref[:, sl] = jnp.dot((p / l).astype(BF16), v2[:, sl], preferred_element_type=F32)
            lse_ref[:, sl] = jnp.broadcast_to(m + jnp.log(l), (BLK, HD))

    shape = jax.ShapeDtypeStruct((length, r * GW), F32)
    o, lse = pl.pallas_call(
        body, name=name, grid=(r, nblk),
        in_specs=[col(0, False), col(1, True), col(1, False), col(2, True), col(2, False)],
        out_specs=[out, out], out_shape=[shape, shape], compiler_params=_params("parallel", "parallel"),
    )(view, view, view, view, view)
    return o.reshape(T, GW), lse.reshape(T, GW)


def attn_a_bwd(name, qkvr, do, lse, dd, g, r):
    length = T // r
    nblk = length // BLK
    view = qkvr.reshape(length, r * 3 * A_W)
    ncol = 3 * A_W // GW

    def col(section, shift):
        def idx(j, b):
            return (jnp.clip(b + shift, 0, nblk - 1), j * ncol + 3 * section + g)
        return pl.BlockSpec((BLK, GW), idx)

    def tok(shift):
        return pl.BlockSpec((BLK, GW), lambda j, b: (jnp.clip(b + shift, 0, nblk - 1), j))

    def body(q_ref, qn_ref, kp_ref, kc_ref, vp_ref, vc_ref, do_ref, don_ref, lse_ref, lsen_ref, dd_ref, ddn_ref,
             dq_ref, dk_ref, dv_ref):
        b = pl.program_id(1)
        mask = _band_mask(b)
        qi = lax.broadcasted_iota(jnp.int32, (2 * BLK, BLK), 0)
        kj = lax.broadcasted_iota(jnp.int32, (2 * BLK, BLK), 1)
        kmask = ((qi < BLK) & (kj <= qi)) | ((qi >= BLK) & (kj >= qi - BLK) & (b + 1 < nblk))
        k2 = jnp.concatenate([kp_ref[...], kc_ref[...]], axis=0)
        v2 = jnp.concatenate([vp_ref[...], vc_ref[...]], axis=0)
        q2 = jnp.concatenate([q_ref[...], qn_ref[...]], axis=0)
        do2 = jnp.concatenate([do_ref[...], don_ref[...]], axis=0)
        lse2 = jnp.concatenate([lse_ref[...], lsen_ref[...]], axis=0)
        dd2 = jnp.concatenate([dd_ref[...], ddn_ref[...]], axis=0)
        nt = (((1,), (1,)), ((), ()))
        tn = (((0,), (0,)), ((), ()))
        for h in range(4):
            sl = slice(h * HD, (h + 1) * HD)
            one = slice(h * HD, h * HD + 1)
            qh, kh, vh, doh = q2[:, sl], k2[:, sl], v2[:, sl], do2[:, sl]
            s = lax.dot_general(qh[:BLK], kh, nt, preferred_element_type=F32) * SCALE
            p = jnp.where(mask, jnp.exp(s - lse2[:BLK, one]), 0.0)
            dp = lax.dot_general(doh[:BLK], vh, nt, preferred_element_type=F32)
            ds = p * (dp + dd2[:BLK, one])
            dq_ref[:, sl] = jnp.dot(ds.astype(BF16), kh, preferred_element_type=F32) * SCALE
            kc, vc = kh[BLK:], vh[BLK:]
            s = lax.dot_general(qh, kc, nt, preferred_element_type=F32) * SCALE
            p = jnp.where(kmask, jnp.exp(s - lse2[:, one]), 0.0)
            dp = lax.dot_general(doh, vc, nt, preferred_element_type=F32)
            ds = p * (dp + dd2[:, one])
            dk_ref[:, sl] = lax.dot_general(ds.astype(BF16), qh, tn, preferred_element_type=F32) * SCALE
            dv_ref[:, sl] = lax.dot_general(p.astype(BF16), doh, tn, preferred_element_type=F32)

    dov = do.reshape(length, r * GW)
    lsev = lse.reshape(length, r * GW)
    ddv = dd.reshape(length, r * GW)
    shape = jax.ShapeDtypeStruct((length, r * GW), F32)
    dq, dk, dv = pl.pallas_call(
        body, name=name, grid=(r, nblk),
        in_specs=[col(0, 0), col(0, 1), col(1, -1), col(1, 0), col(2, -1), col(2, 0),
                  tok(0), tok(1), tok(0), tok(1), tok(0), tok(1)],
        out_specs=[tok(0)] * 3, out_shape=[shape] * 3, compiler_params=_params("parallel", "parallel"),
    )(view, view, view, view, view, view, dov, dov, lsev, lsev, ddv, ddv)
    return dq.reshape(T, GW), dk.reshape(T, GW), dv.reshape(T, GW)


def _head_sum(x):
    i = lax.div(lax.broadcasted_iota(jnp.int32, (GW, GW), 0), jnp.int32(HD))
    j = lax.div(lax.broadcasted_iota(jnp.int32, (GW, GW), 1), jnp.int32(HD))
    return jnp.dot(x, (i == j).astype(F32), precision=HIGHEST, preferred_element_type=F32)


def _alphas(lses):
    m = jnp.maximum(jnp.maximum(lses[0], lses[1]), lses[2])
    e = [jnp.exp(l - m) for l in lses]
    z = e[0] + e[1] + e[2]
    return [x / z for x in e]


def combine_fwd(name, os_, lses, tb=256):
    blk = pl.BlockSpec((tb, GW), lambda i: (i, 0))

    def body(o0, o1, o2, l0, l1, l2, oc_ref):
        al = _alphas([l0[...], l1[...], l2[...]])
        for g, o_ref in enumerate((o0, o1, o2)):
            oc_ref[:, g * GW:(g + 1) * GW] = (o_ref[...] * al[g]).astype(BF16)

    return pl.pallas_call(
        body, name=name, grid=(T // tb,), in_specs=[blk] * 6, out_specs=pl.BlockSpec((tb, A_W), lambda i: (i, 0)),
        out_shape=jax.ShapeDtypeStruct((T, A_W), BF16), compiler_params=_params("parallel"),
    )(*os_, *lses)


def combine_bwd(name, doc, os_, lses, tb=256):
    blk = pl.BlockSpec((tb, GW), lambda i: (i, 0))

    def body(doc_ref, o0, o1, o2, l0, l1, l2, d0, d1, d2, e0, e1, e2):
        al = _alphas([l0[...], l1[...], l2[...]])
        dal = [_head_sum(doc_ref[:, g * GW:(g + 1) * GW] * o_ref[...]) for g, o_ref in enumerate((o0, o1, o2))]
        mean = al[0] * dal[0] + al[1] * dal[1] + al[2] * dal[2]
        for g, (do_ref, dd_ref) in enumerate(((d0, e0), (d1, e1), (d2, e2))):
            do_ref[...] = (doc_ref[:, g * GW:(g + 1) * GW] * al[g]).astype(BF16)
            dd_ref[...] = al[g] * (dal[g] - mean) - al[g] * dal[g]

    res = pl.pallas_call(
        body, name=name, grid=(T // tb,), in_specs=[pl.BlockSpec((tb, A_W), lambda i: (i, 0))] + [blk] * 6,
        out_specs=[blk] * 6,
        out_shape=[jax.ShapeDtypeStruct((T, GW), BF16)] * 3 + [jax.ShapeDtypeStruct((T, GW), F32)] * 3,
        compiler_params=_params("parallel"),
    )(doc, *os_, *lses)
    return res[:3], res[3:]


TQ = 128
TK = 256
FOX_HEADS = 4
NT = (((1,), (1,)), ((), ()))
TN = (((0,), (0,)), ((), ()))


def _fox_scores(q, kb, cj, row0, col0, masked):
    s = lax.dot_general(q, kb, NT, preferred_element_type=F32) * SCALE - cj
    if masked:
        qi = row0 + lax.broadcasted_iota(jnp.int32, s.shape, 0)
        kj = col0 + lax.broadcasted_iota(jnp.int32, s.shape, 1)
        s = jnp.where(kj <= qi, s, NEG)
    return s


def fox_fwd(name, q, k, v, ct):
    def body(q_ref, k_ref, v_ref, c_ref, o_ref, lse_ref):
        i = pl.program_id(1)
        n_full = lax.div(i, jnp.int32(2))

        def step(jb, carry, masked):
            start = pl.multiple_of(jb * TK, TK)
            out = []
            for h in range(FOX_HEADS):
                m, l, acc = carry[h]
                kb = k_ref[h, pl.ds(start, TK), :]
                vb = v_ref[h, pl.ds(start, TK), :]
                s = _fox_scores(q_ref[h], kb, c_ref[h, pl.ds(jb, 1), :], i * TQ, jb * TK, masked)
                m_new = jnp.maximum(m, jnp.max(s, axis=1, keepdims=True))
                a = jnp.exp(m - m_new)
                p = jnp.exp(s - m_new)
                l = a * l + jnp.sum(p, axis=1, keepdims=True)
                acc = a * acc + jnp.dot(p.astype(BF16), vb, preferred_element_type=F32)
                out.append((m_new, l, acc))
            return tuple(out)

        init = tuple((jnp.full((TQ, 1), NEG, F32), jnp.zeros((TQ, 1), F32), jnp.zeros((TQ, HD), F32))
                     for _ in range(FOX_HEADS))
        carry = lax.fori_loop(0, n_full, lambda jb, c: step(jb, c, False), init)
        carry = step(n_full, carry, True)
        for h in range(FOX_HEADS):
            m, l, acc = carry[h]
            o_ref[h] = (acc / l).astype(BF16)
            lse_ref[h] = jnp.broadcast_to(m + jnp.log(l), (TQ, LANES))

    head = lambda h, i: (h, 0, 0)
    return pl.pallas_call(
        body, name=name, grid=(B_HEADS // FOX_HEADS, T // TQ),
        in_specs=[pl.BlockSpec((FOX_HEADS, TQ, HD), lambda h, i: (h, i, 0)), pl.BlockSpec((FOX_HEADS, T, HD), head),
                  pl.BlockSpec((FOX_HEADS, T, HD), head), pl.BlockSpec((FOX_HEADS, T // TK, TK), head)],
        out_specs=[pl.BlockSpec((FOX_HEADS, TQ, HD), lambda h, i: (h, i, 0)),
                   pl.BlockSpec((FOX_HEADS, TQ, LANES), lambda h, i: (h, i, 0))],
        out_shape=[jax.ShapeDtypeStruct((B_HEADS, T, HD), BF16), jax.ShapeDtypeStruct((B_HEADS, T, LANES), F32)],
        compiler_params=_params("parallel", "parallel"),
    )(q, k, v, ct)


def fox_bwd_q(name, q, k, v, ct, o, do, lse):
    def body(q_ref, k_ref, v_ref, c_ref, o_ref, do_ref, lse_ref, dq_ref, drow_ref):
        i = pl.program_id(1)
        n_full = lax.div(i, jnp.int32(2))
        delta = [jnp.sum(do_ref[h].astype(F32) * o_ref[h].astype(F32), axis=1, keepdims=True)
                 for h in range(FOX_HEADS)]

        def step(jb, carry, masked):
            start = pl.multiple_of(jb * TK, TK)
            out = []
            for h in range(FOX_HEADS):
                dq, drow = carry[h]
                kb = k_ref[h, pl.ds(start, TK), :]
                vb = v_ref[h, pl.ds(start, TK), :]
                s = _fox_scores(q_ref[h], kb, c_ref[h, pl.ds(jb, 1), :], i * TQ, jb * TK, masked)
                p = jnp.exp(s - lse_ref[h, :, 0:1])
                dp = lax.dot_general(do_ref[h], vb, NT, preferred_element_type=F32)
                ds = p * (dp - delta[h])
                out.append((dq + jnp.dot(ds.astype(BF16), kb, preferred_element_type=F32),
                            drow + jnp.sum(ds, axis=1, keepdims=True)))
            return tuple(out)

        init = tuple((jnp.zeros((TQ, HD), F32), jnp.zeros((TQ, 1), F32)) for _ in range(FOX_HEADS))
        carry = lax.fori_loop(0, n_full, lambda jb, c: step(jb, c, False), init)
        carry = step(n_full, carry, True)
        for h in range(FOX_HEADS):
            dq, drow = carry[h]
            dq_ref[h] = (dq * SCALE).astype(BF16)
            drow_ref[h] = jnp.broadcast_to(drow, (TQ, LANES))

    head = lambda h, i: (h, 0, 0)
    blk = pl.BlockSpec((FOX_HEADS, TQ, HD), lambda h, i: (h, i, 0))
    stat = pl.BlockSpec((FOX_HEADS, TQ, LANES), lambda h, i: (h, i, 0))
    return pl.pallas_call(
        body, name=name, grid=(B_HEADS // FOX_HEADS, T // TQ),
        in_specs=[blk, pl.BlockSpec((FOX_HEADS, T, HD), head), pl.BlockSpec((FOX_HEADS, T, HD), head),
                  pl.BlockSpec((FOX_HEADS, T // TK, TK), head), blk, blk, stat],
        out_specs=[blk, stat],
        out_shape=[jax.ShapeDtypeStruct((B_HEADS, T, HD), BF16), jax.ShapeDtypeStruct((B_HEADS, T, LANES), F32)],
        compiler_params=_params("parallel", "parallel"),
    )(q, k, v, ct, o, do, lse)


def fox_bwd_kv(name, q, k, v, ct, o, do, lse, prev):
    has_prev = prev is not None
    nq = T // TQ

    def body(*refs):
        q_ref, k_ref, v_ref, c_ref, o_ref, do_ref, lse_ref = refs[:7]
        dk_ref, dv_ref, dc_ref = refs[-3:]
        jb = pl.program_id(1)
        dk_ref[...] = jnp.zeros_like(dk_ref)
        dv_ref[...] = jnp.zeros_like(dv_ref)

        def step(i, dcs, masked):
            start = pl.multiple_of(i * TQ, TQ)
            out = []
            for h in range(FOX_HEADS):
                qv = q_ref[h, pl.ds(start, TQ), :]
                dov = do_ref[h, pl.ds(start, TQ), :]
                ov = o_ref[h, pl.ds(start, TQ), :]
                lsev = lse_ref[h, pl.ds(start, TQ), 0:1]
                delta = jnp.sum(dov.astype(F32) * ov.astype(F32), axis=1, keepdims=True)
                s = _fox_scores(qv, k_ref[h], c_ref[h, pl.ds(jb, 1), :], i * TQ, jb * TK, masked)
                p = jnp.exp(s - lsev)
                dp = lax.dot_general(dov, v_ref[h], NT, preferred_element_type=F32)
                ds = p * (dp - delta)
                dv_ref[h] += lax.dot_general(p.astype(BF16), dov, TN, preferred_element_type=F32)
                dk_ref[h] += lax.dot_general(ds.astype(BF16), qv, TN, preferred_element_type=F32)
                out.append(dcs[h] - jnp.sum(ds, axis=0, keepdims=True))
            return tuple(out)

        first = 2 * jb
        dcs = tuple(jnp.zeros((1, TK), F32) for _ in range(FOX_HEADS))
        dcs = step(first, dcs, True)
        dcs = step(first + 1, dcs, True)
        dcs = lax.fori_loop(first + 2, nq, lambda i, c: step(i, c, False), dcs)
        for h in range(FOX_HEADS):
            dk = dk_ref[h] * SCALE
            dc = dcs[h]
            if has_prev:
                dk = dk + refs[7][h]
                dv_ref[h] += refs[8][h]
                dc = dc + refs[9][h, pl.ds(jb, 1), :]
            dk_ref[h] = dk
            dc_ref[h, pl.ds(jb, 1), :] = dc

    head = lambda h, j: (h, 0, 0)
    full = pl.BlockSpec((FOX_HEADS, T, HD), head)
    blk = pl.BlockSpec((FOX_HEADS, TK, HD), lambda h, j: (h, j, 0))
    cspec = pl.BlockSpec((FOX_HEADS, T // TK, TK), head)
    ins = [q, k, v, ct, o, do, lse] + (list(prev) if has_prev else [])
    in_specs = [full, blk, blk, cspec, full, full, pl.BlockSpec((FOX_HEADS, T, LANES), head)] + ([blk, blk, cspec] if has_prev else [])
    return pl.pallas_call(
        body, name=name, grid=(B_HEADS // FOX_HEADS, T // TK), in_specs=in_specs, out_specs=[blk, blk, cspec],
        out_shape=[jax.ShapeDtypeStruct((B_HEADS, T, HD), F32)] * 2 + [jax.ShapeDtypeStruct((B_HEADS, T // TK, TK), F32)],
        compiler_params=_params("parallel", "arbitrary"),
    )(*ins)


def _tri(upper):
    i = lax.broadcasted_iota(jnp.int32, (BLK, BLK), 0)
    j = lax.broadcasted_iota(jnp.int32, (BLK, BLK), 1)
    return ((i <= j) if upper else (i >= j)).astype(F32)


def gates_fwd(name, fgt, b_f):
    def body(f_ref, b_ref, c_ref):
        tri = _tri(True)
        carry = jnp.zeros((B_HEADS, 1), F32)
        for blk in range(T // BLK):
            sl = slice(blk * BLK, (blk + 1) * BLK)
            z = f_ref[:, sl] + b_ref[...]
            logf = jnp.minimum(z, 0.0) - jnp.log(1.0 + jnp.exp(-jnp.abs(z)))
            cs = jnp.dot(logf, tri, precision=HIGHEST, preferred_element_type=F32) + carry
            c_ref[:, sl] = cs
            carry = cs[:, BLK - 1:BLK]

    return pl.pallas_call(
        body, name=name, out_shape=jax.ShapeDtypeStruct((B_HEADS, T), F32), compiler_params=_params(),
    )(fgt, b_f)


def gates_bwd(name, fgt, b_f, dcs):
    n_dc = len(dcs)

    def body(*refs):
        f_ref, b_ref = refs[:2]
        dc_refs = refs[2:2 + n_dc]
        dz_ref, db_ref = refs[2 + n_dc:]
        tri = _tri(False)
        carry = jnp.zeros((B_HEADS, 1), F32)
        db = jnp.zeros((B_HEADS, 1), F32)
        for blk in reversed(range(T // BLK)):
            sl = slice(blk * BLK, (blk + 1) * BLK)
            dc = dc_refs[0][:, sl]
            for r in dc_refs[1:]:
                dc = dc + r[:, sl]
            rc = jnp.dot(dc, tri, precision=HIGHEST, preferred_element_type=F32) + carry
            carry = rc[:, 0:1]
            z = f_ref[:, sl] + b_ref[...]
            e = jnp.exp(-jnp.abs(z))
            dz = rc * jnp.where(z >= 0.0, e, 1.0) / (1.0 + e)
            dz_ref[:, sl] = dz
            db = db + jnp.sum(dz, axis=1, keepdims=True)
        db_ref[...] = db

    return pl.pallas_call(
        body, name=name,
        out_shape=[jax.ShapeDtypeStruct((B_HEADS, T), F32), jax.ShapeDtypeStruct((B_HEADS, 1), F32)],
        compiler_params=_params(),
    )(fgt, b_f, *dcs)


CONV_TB = 256
GELU_K = math.sqrt(2.0 / math.pi)
GELU_C = 0.044715


def _shift_down(x, halo_ref, n):
    rows = lax.broadcasted_iota(jnp.int32, x.shape, 0)
    y = pltpu.roll(x, n, 0)
    for k in range(n):
        y = jnp.where(rows == k, halo_ref[pl.ds(8 - n + k, 1), :], y)
    return y


def _conv(x, halo_ref, cw_ref, cb_ref, first):
    x1 = _shift_down(x, halo_ref, 1)
    x2 = _shift_down(x, halo_ref, 2)
    rows = lax.broadcasted_iota(jnp.int32, x.shape, 0)
    x1 = jnp.where(first & (rows < 1), 0.0, x1)
    x2 = jnp.where(first & (rows < 2), 0.0, x2)
    y = x2 * cw_ref[0:1, :] + x1 * cw_ref[1:2, :] + x * cw_ref[2:3, :] + cb_ref[...]
    return y, x1, x2


def _gelu_parts(x):
    th = jnp.tanh(GELU_K * (x + GELU_C * x * x * x))
    val = 0.5 * x * (1.0 + th)
    grad = 0.5 * (1.0 + th) + 0.5 * x * (1.0 - th * th) * GELU_K * (1.0 + 3.0 * GELU_C * x * x)
    return val, grad


def _conv_specs(tb):
    def slab(off):
        return pl.BlockSpec((None, tb, SLAB), lambda d, i: (d + off, i, 0))

    def halo(off):
        return pl.BlockSpec((None, 8, SLAB), lambda d, i: (d + off, jnp.maximum(i * (tb // 8) - 1, 0), 0))

    def par(rows, off):
        return pl.BlockSpec((None, rows, SLAB), lambda d, i: (d + off, 0, 0))

    return slab, halo, par


def convglu_fwd(name, a, cw, cb, tb=CONV_TB):
    slab, halo, par = _conv_specs(tb)

    def body(ag, hg, av, hv, cwg, cbg, cwv, cbv, u_ref):
        first = pl.program_id(1) == 0
        gate, _, _ = _conv(ag[...], hg, cwg, cbg, first)
        val, _, _ = _conv(av[...], hv, cwv, cbv, first)
        u_ref[...] = (_gelu_parts(gate)[0] * val).astype(BF16)

    return pl.pallas_call(
        body, name=name, grid=(4, T // tb),
        in_specs=[slab(0), halo(0), slab(4), halo(4), par(3, 0), par(1, 0), par(3, 4), par(1, 4)],
        out_specs=slab(0), out_shape=jax.ShapeDtypeStruct((4, T, SLAB), BF16),
        compiler_params=_params("parallel", "parallel"),
    )(a, a, a, a, cw, cb, cw, cb)


def convglu_bwd(name, du, a, cw, cb, tb=CONV_TB):
    slab, halo, par = _conv_specs(tb)

    def body(du_ref, ag, hg, av, hv, cwg, cbg, cwv, cbv, dg_ref, dv_ref, dcwg, dcbg, dcwv, dcbv):
        first = pl.program_id(1) == 0
        gate, g1, g2 = _conv(ag[...], hg, cwg, cbg, first)
        val, v1, v2 = _conv(av[...], hv, cwv, cbv, first)
        act, dact = _gelu_parts(gate)
        duv = du_ref[...].astype(F32)
        dgate = duv * val * dact
        dval = duv * act
        dg_ref[...] = dgate
        dv_ref[...] = dval
        for dy, xs, dcw_ref, dcb_ref in ((dgate, (g2, g1, ag[...]), dcwg, dcbg), (dval, (v2, v1, av[...]), dcwv, dcbv)):
            parts = [jnp.sum(dy * x, axis=0, keepdims=True) for x in xs]
            bias = jnp.sum(dy, axis=0, keepdims=True)

            @pl.when(first)
            def _():
                for k in range(3):
                    dcw_ref[k:k + 1, :] = parts[k]
                dcb_ref[...] = bias

            @pl.when(jnp.logical_not(first))
            def _():
                for k in range(3):
                    dcw_ref[k:k + 1, :] += parts[k]
                dcb_ref[...] += bias

    res = pl.pallas_call(
        body, name=name, grid=(4, T // tb),
        in_specs=[slab(0), slab(0), halo(0), slab(4), halo(4), par(3, 0), par(1, 0), par(3, 4), par(1, 4)],
        out_specs=[slab(0), slab(0), par(3, 0), par(1, 0), par(3, 0), par(1, 0)],
        out_shape=[jax.ShapeDtypeStruct((4, T, SLAB), F32)] * 2
        + [jax.ShapeDtypeStruct((4, 3, SLAB), F32), jax.ShapeDtypeStruct((4, 1, SLAB), F32)] * 2,
        compiler_params=_params("parallel", "arbitrary"),
    )(du, a, a, a, a, cw, cb, cw, cb)
    dgate, dval, dcwg, dcbg, dcwv, dcbv = res
    return (dgate, dval), jnp.concatenate([dcwg, dcwv], axis=0), jnp.concatenate([dcbg, dcbv], axis=0)


def conv_bwd_input(name, dac_pair, cw, tb=CONV_TB):
    nblk = T // tb

    def run(x, off):
        def body(x_ref, nx_ref, cw_ref, da_ref):
            last = pl.program_id(1) == nblk - 1
            xv = x_ref[...]
            rows = lax.broadcasted_iota(jnp.int32, xv.shape, 0)

            def up(n):
                y = pltpu.roll(xv, tb - n, 0)
                for k in range(n):
                    y = jnp.where(rows == tb - n + k, nx_ref[pl.ds(k, 1), :], y)
                return jnp.where(last & (rows >= tb - n), 0.0, y)

            da_ref[...] = (xv * cw_ref[2:3, :] + up(1) * cw_ref[1:2, :] + up(2) * cw_ref[0:1, :]).astype(BF16)

        return pl.pallas_call(
            body, name=f"{name}_{off}", grid=(4, nblk),
            in_specs=[pl.BlockSpec((None, tb, SLAB), lambda d, i: (d, i, 0)),
                      pl.BlockSpec((None, 8, SLAB), lambda d, i: (d, jnp.minimum((i + 1) * (tb // 8), T // 8 - 1), 0)),
                      pl.BlockSpec((None, 3, SLAB), lambda d, i: (d + off, 0, 0))],
            out_specs=pl.BlockSpec((None, tb, SLAB), lambda d, i: (d, i, 0)),
            out_shape=jax.ShapeDtypeStruct((4, T, SLAB), BF16), compiler_params=_params("parallel", "parallel"),
        )(x, x, cw)

    return jnp.concatenate([run(dac_pair[0], 0), run(dac_pair[1], 4)], axis=0)


def loss_head(name, y, target, tb=256):
    row = pl.BlockSpec((tb, D), lambda i: (i, 0))

    def body(y_ref, t_ref, dy_ref, loss_ref):
        diff = y_ref[...] - t_ref[...]
        dy_ref[...] = diff * (1.0 / D)
        part = jnp.sum(jnp.sum(diff * diff, axis=1, keepdims=True), axis=0, keepdims=True) * (0.5 / D)

        @pl.when(pl.program_id(0) == 0)
        def _():
            loss_ref[...] = part

        @pl.when(pl.program_id(0) > 0)
        def _():
            loss_ref[...] += part

    return pl.pallas_call(
        body, name=name, grid=(T // tb,), in_specs=[row, row],
        out_specs=[row, pl.BlockSpec((1, 1), lambda i: (0, 0))],
        out_shape=[jax.ShapeDtypeStruct((T, D), F32), jax.ShapeDtypeStruct((1, 1), F32)],
        compiler_params=_params("arbitrary"),
    )(y, target)


def _row_tile(rows, cols, bytes_per_elem, budget=6 * 1024 * 1024):
    for tr in (1024, 512, 256, 128, 64, 32, 16, 8):
        if rows % tr == 0 and tr * cols * bytes_per_elem <= budget:
            return tr
    return rows


def adamw(name, parts, w, m, v):
    n_parts, rows, cols = parts.shape
    tr = _row_tile(rows, cols, n_parts * parts.dtype.itemsize + 28)
    blk = pl.BlockSpec((tr, cols), lambda i: (i, 0))
    b1c = 1.0 - ADAM_B1 ** ADAM_STEP
    b2c = 1.0 - ADAM_B2 ** ADAM_STEP

    def body(p_ref, w_ref, m_ref, v_ref, g_ref, d_ref, nm_ref, nv_ref):
        g = p_ref[0].astype(F32)
        for k in range(1, n_parts):
            g = g + p_ref[k].astype(F32)
        nm = ADAM_B1 * m_ref[...] + (1.0 - ADAM_B1) * g
        nv = ADAM_B2 * v_ref[...] + (1.0 - ADAM_B2) * (g * g)
        g_ref[...] = g
        nm_ref[...] = nm
        nv_ref[...] = nv
        d_ref[...] = -ADAM_LR * ((nm / b1c) / (jnp.sqrt(nv / b2c) + ADAM_EPS) + ADAM_WD * w_ref[...])

    return pl.pallas_call(
        body, name=name, grid=(rows // tr,),
        in_specs=[pl.BlockSpec((n_parts, tr, cols), lambda i: (0, i, 0)), blk, blk, blk], out_specs=[blk] * 4,
        out_shape=[jax.ShapeDtypeStruct((rows, cols), F32)] * 4, compiler_params=_params("parallel"),
    )(parts, w, m, v)


def exchange(name, items, scatter):
    n = len(items)
    hbm = pl.BlockSpec(memory_space=pltpu.HBM)

    def body(*refs):
        ins, outs = refs[:n], refs[n:2 * n]
        send_sems, recv_sems, local_sems = refs[2 * n:]
        x, y, c = lax.axis_index("x"), lax.axis_index("y"), lax.axis_index("c")
        me = 4 * x + 2 * y + c
        copies = []
        for t in range(n):
            own = pltpu.make_async_copy(ins[t].at[me] if scatter else ins[t], outs[t].at[me], local_sems.at[t])
            own.start()
            copies.append(own)
            for rel in range(1, NDEV):
                px = 1 - x if rel & 4 else x
                py = 1 - y if rel & 2 else y
                pc = 1 - c if rel & 1 else c
                src = ins[t].at[4 * px + 2 * py + pc] if scatter else ins[t]
                cp = pltpu.make_async_remote_copy(
                    src_ref=src, dst_ref=outs[t].at[me], send_sem=send_sems.at[t, rel - 1],
                    recv_sem=recv_sems.at[t, rel - 1], device_id=(px, py, pc), device_id_type=pl.DeviceIdType.MESH)
                cp.start()
                copies.append(cp)
        for cp in copies:
            cp.wait()

    out_shape = [jax.ShapeDtypeStruct(it.shape if scatter else (NDEV,) + it.shape, it.dtype) for it in items]
    return pl.pallas_call(
        body, name=name, in_specs=[hbm] * n, out_specs=[hbm] * n, out_shape=out_shape,
        scratch_shapes=[pltpu.SemaphoreType.DMA((n, NDEV - 1)), pltpu.SemaphoreType.DMA((n, NDEV - 1)),
                        pltpu.SemaphoreType.DMA((n,))],
    )(*items)


def sum_slots(name, parts):
    _, rows, cols = parts.shape

    def body(p_ref, o_ref):
        s = p_ref[0]
        for k in range(1, NDEV):
            s = s + p_ref[k]
        o_ref[...] = s

    return pl.pallas_call(body, name=name, out_shape=jax.ShapeDtypeStruct((rows, cols), F32), compiler_params=_params())(parts)


def _heads(t):
    return t.reshape(T, B_HEADS, HD).transpose(1, 0, 2)


def _unheads(t):
    return t.transpose(1, 0, 2).reshape(T, B_HEADS * HD)


def _cols_from_slots(g):
    return g.transpose(1, 0, 2).reshape(g.shape[1], NDEV * g.shape[2])


def _slots_from_cols(w):
    return w.reshape(w.shape[0], NDEV, w.shape[1] // NDEV).transpose(1, 0, 2)


def _pack(arrays, rows):
    flat = jnp.concatenate([a.reshape(-1).astype(F32) for a in arrays])
    return jnp.pad(flat, (0, rows * LANES - flat.shape[0])).reshape(rows, LANES)


def _unpack(buf, shapes):
    flat = buf.reshape(-1)
    out, pos = [], 0
    for sh in shapes:
        size = math.prod(sh)
        out.append(flat[pos:pos + size].reshape(sh))
        pos += size
    return out


def kernel(x, norm_gains, w_qkv_a, w_o_a, w_q_b, w_o_b, kv_norm, w_kvf, b_f, w_up, conv_w, conv_b, w_down, loss_target, m_norm_gains, m_w_qkv_a, m_w_o_a, m_w_q_b, m_w_o_b, m_kv_norm, m_w_kvf, m_b_f, m_w_up, m_conv_w, m_conv_b, m_w_down, v_norm_gains, v_w_qkv_a, v_w_o_a, v_w_q_b, v_w_o_b, v_kv_norm, v_w_kvf, v_b_f, v_w_up, v_conv_w, v_conv_b, v_w_down):
    me = 4 * lax.axis_index("x") + 2 * lax.axis_index("y") + lax.axis_index("c")
    n_b = DEPTH - N_A

    big = [w_qkv_a, w_o_a, w_q_b, w_o_b, w_kvf, w_up, w_down]
    gathered = exchange("gather_weights", [w.astype(BF16) for w in big] + [norm_gains, conv_w], scatter=False)
    g_qkv, g_oa, g_qb, g_ob, g_kvf, g_up, g_down, g_gains, g_cw = gathered
    wqkv = [_cols_from_slots(g_qkv[:, l]) for l in range(N_A)]
    woa = [_cols_from_slots(g_oa[:, l]) for l in range(N_A)]
    wqb = [g_qb[:, l].reshape(D, D) for l in range(n_b)]
    wob = [g_ob[:, l].reshape(D, D) for l in range(n_b)]
    wkvf = jnp.pad(_cols_from_slots(g_kvf), ((0, 0), (0, KVF_PAD - KVF)))
    wup = [g_up[:, l] for l in range(DEPTH)]
    wdown = [g_down[:, l].reshape(4, SLAB, D) for l in range(DEPTH)]
    gains = g_gains.transpose(1, 2, 0, 3).reshape(DEPTH, 4, D)
    cws = [g_cw[:, l] for l in range(DEPTH)]
    cbs = [conv_b[l].reshape(NDEV, 1, SLAB) for l in range(DEPTH)]
    tables = rope_tables()
    b_col = b_f.reshape(B_HEADS, 1)

    h = x.reshape(T, D)
    _, (xn,) = resid_norm("norm_in", h, None, None, [gains[0, 0]])
    saved = []
    shared = None
    for l in range(DEPTH):
        s = {"h": h, "xn": xn}
        if l < N_A:
            qkv = matmul(f"qkv{l}", xn, wqkv[l], tm=1024, tn=768)
            s["qkvr"] = rope(f"rope{l}", qkv, tables, inverse=False)
            os_, lses = [], []
            for g, (_, r) in enumerate(A_GROUPS):
                o, lse = attn_a_fwd(f"attn_a{l}_{g}", s["qkvr"], g, r)
                os_.append(o)
                lses.append(lse)
            s["o"], s["lse"] = os_, lses
            s["oc"] = combine_fwd(f"combine{l}", os_, lses)
            mix = matmul(f"wo_a{l}", s["oc"], woa[l], tm=1024, tn=512)
        else:
            if l == N_A:
                kvf = matmul("kvf", xkv, wkvf, tm=1024, tn=768)
                k_h = _heads(kvf[:, :D].astype(BF16))
                v_h = _heads(kvf[:, D:2 * D].astype(BF16))
                fgt = kvf[:, 2 * D:KVF].T
                ct = gates_fwd("gates", fgt, b_col).reshape(B_HEADS, T // TK, TK)
                shared = {"xkv": xkv, "k": k_h, "v": v_h, "fgt": fgt, "ct": ct, "h": h}
            j = l - N_A
            s["q"] = _heads(matmul(f"wq_b{j}", xn, wqb[j], out_dtype=BF16, tm=1024, tn=512))
            s["o"], s["lse"] = fox_fwd(f"fox{j}", s["q"], shared["k"], shared["v"], shared["ct"])
            s["oc"] = _unheads(s["o"])
            mix = matmul(f"wo_b{j}", s["oc"], wob[j], tm=1024, tn=512)
        s["mix"] = mix
        s["h1"], (s["xn2"],) = resid_norm(f"norm_mid{l}", h, mix, gains[l, 1], [gains[l, 2]])
        s["a"] = matmul(f"up{l}", s["xn2"], wup[l], tm=1024, tn=SLAB, batch="b_out")
        s["u"] = convglu_fwd(f"convglu{l}", s["a"], cws[l], cbs[l])
        s["f"] = matmul(f"down{l}", s["u"], wdown[l], tm=1024, tn=512, batch="reduce")
        nxt = [gains[l + 1, 0]] if l + 1 < DEPTH else []
        if l == N_A - 1:
            nxt.append(kv_norm)
        h, normed = resid_norm(f"norm_out{l}", s["h1"], s["f"], gains[l, 3], nxt)
        if l + 1 < DEPTH:
            xn = normed[0]
        if l == N_A - 1:
            xkv = normed[1]
        saved.append(s)

    dh, loss_part = loss_head("loss", h, loss_target.reshape(T, D))

    d_gains = [[None] * 4 for _ in range(DEPTH)]
    d_cw, d_cb = [None] * DEPTH, [None] * DEPTH
    gw = {"qkv": [None] * N_A, "oa": [None] * N_A, "qb": [None] * n_b, "ob": [None] * n_b, "up": [None] * DEPTH,
          "down": [None] * DEPTH}
    kv_acc = None
    d_rows = []
    for l in reversed(range(DEPTH)):
        s = saved[l]
        df, d_gains[l][3] = rms_bwd(f"bwd_norm_out{l}", s["f"], gains[l, 3], dh, out_dtype=BF16)
        du = matmul(f"bwd_down_x{l}", df, wdown[l], tb=True, out_dtype=BF16, tm=1024, tn=SLAB, batch="b_out")
        gw["down"][l] = matmul(f"bwd_down_w{l}", s["u"], df, ta=True, out_dtype=BF16, tm=SLAB, tn=512, batch="a_out")
        dac, d_cw[l], d_cb[l] = convglu_bwd(f"bwd_convglu{l}", du, s["a"], cws[l], cbs[l])
        da = conv_bwd_input(f"bwd_conv{l}", dac, cws[l])
        dxn2 = matmul(f"bwd_up_x{l}", da, wup[l], tb=True, tm=1024, tn=512, batch="reduce")
        gw["up"][l] = matmul(f"bwd_up_w{l}", s["xn2"], da, ta=True, out_dtype=BF16, tm=512, tn=SLAB, batch="b_out")
        dh1, d_gains[l][2] = rms_bwd(f"bwd_norm_mid{l}", s["h1"], gains[l, 2], dxn2, add=dh)
        dmix, d_gains[l][1] = rms_bwd(f"bwd_norm_mix{l}", s["mix"], gains[l, 1], dh1, out_dtype=BF16)
        if l < N_A:
            doc = matmul(f"bwd_wo_a_x{l}", dmix, woa[l], tb=True, tm=1024, tn=A_W)
            gw["oa"][l] = matmul(f"bwd_wo_a_w{l}", s["oc"], dmix, ta=True, out_dtype=BF16, tm=A_W, tn=512)
            dos, dds = combine_bwd(f"bwd_combine{l}", doc, s["o"], s["lse"])
            cols = [None] * 9
            for g, (_, r) in enumerate(A_GROUPS):
                dq, dk, dv = attn_a_bwd(f"bwd_attn_a{l}_{g}", s["qkvr"], dos[g], s["lse"][g], dds[g], g, r)
                cols[g], cols[3 + g], cols[6 + g] = dq, dk, dv
            dqkv = rope(f"bwd_rope{l}", jnp.concatenate(cols, axis=1), tables, inverse=True)
            dxn = matmul(f"bwd_qkv_x{l}", dqkv, wqkv[l], tb=True, tm=1024, tn=512)
            gw["qkv"][l] = matmul(f"bwd_qkv_w{l}", s["xn"], dqkv, ta=True, out_dtype=BF16, tm=512, tn=768)
        else:
            j = l - N_A
            do = _heads(matmul(f"bwd_wo_b_x{j}", dmix, wob[j], tb=True, out_dtype=BF16, tm=1024, tn=512))
            gw["ob"][j] = matmul(f"bwd_wo_b_w{j}", s["oc"], dmix, ta=True, out_dtype=BF16, tm=512, tn=512)
            args = (s["q"], shared["k"], shared["v"], shared["ct"], s["o"], do, s["lse"])
            dq_h, drow = fox_bwd_q(f"bwd_fox_q{j}", *args)
            dq = _unheads(dq_h)
            d_rows.append(drow[:, :, 0])
            kv_acc = fox_bwd_kv(f"bwd_fox_kv{j}", *args, kv_acc)
            dxn = matmul(f"bwd_wq_b_x{j}", dq, wqb[j], tb=True, tm=1024, tn=512)
            gw["qb"][j] = matmul(f"bwd_wq_b_w{j}", s["xn"], dq, ta=True, out_dtype=BF16, tm=512, tn=512)
        dh, d_gains[l][0] = rms_bwd(f"bwd_norm_in{l}", s["h"], gains[l, 0], dxn, add=dh1)
        if l == N_A:
            dk_h, dv_h, dct = kv_acc
            dfgt, d_bf = gates_bwd("bwd_gates", shared["fgt"], b_col, [dct.reshape(B_HEADS, T)] + d_rows)
            dkvf = jnp.concatenate(
                [_unheads(dk_h).astype(BF16), _unheads(dv_h).astype(BF16), dfgt.T.astype(BF16),
                 jnp.zeros((T, KVF_PAD - KVF), BF16)], axis=1)
            dxkv = matmul("bwd_kvf_x", dkvf, wkvf, tb=True, tm=1024, tn=512)
            g_kvf_full = matmul("bwd_kvf_w", shared["xkv"], dkvf, ta=True, out_dtype=BF16, tm=512, tn=768)
            dh, d_kvn = rms_bwd("bwd_norm_kv", shared["h"], kv_norm, dxkv, add=dh)

    small_shapes = [(DEPTH, 4, D), (D,), (B_HEADS,), (DEPTH, 3, NDEV * SLAB), (DEPTH, NDEV * SLAB), (1,)]
    small = [
        jnp.stack([jnp.concatenate(row, axis=0) for row in d_gains]),
        d_kvn, d_bf,
        jnp.stack([d.transpose(1, 0, 2).reshape(3, NDEV * SLAB) for d in d_cw]),
        jnp.stack([d.reshape(NDEV * SLAB) for d in d_cb]),
        loss_part,
    ]
    small_rows = 848
    (small_all,) = exchange("gather_small_grads", [_pack(small, small_rows)], scatter=False)
    g_gains_full, g_kvn, g_bf, g_cw_full, g_cb, loss = _unpack(sum_slots("sum_small", small_all), small_shapes)
    g_gains_mine = lax.dynamic_slice_in_dim(g_gains_full, me * (D // NDEV), D // NDEV, axis=2)
    g_cw_mine = lax.dynamic_slice_in_dim(g_cw_full, me * SLAB, SLAB, axis=2)

    small_w = [norm_gains, kv_norm, b_f, conv_w, conv_b]
    small_m = [m_norm_gains, m_kv_norm, m_b_f, m_conv_w, m_conv_b]
    small_v = [v_norm_gains, v_kv_norm, v_b_f, v_conv_w, v_conv_b]
    small_g = [g_gains_mine, g_kvn, g_bf, g_cw_mine, g_cb]
    shapes = [w.shape for w in small_w]
    rows = 320
    res = adamw("adamw_small", _pack(small_g, rows)[None], _pack(small_w, rows), _pack(small_m, rows), _pack(small_v, rows))
    _, s_delta, s_m, s_v = [_unpack(r, shapes) for r in res]

    parts = [
        jnp.stack([_slots_from_cols(g) for g in gw["qkv"]], axis=1),
        jnp.stack([_slots_from_cols(g) for g in gw["oa"]], axis=1),
        jnp.stack([g.reshape(NDEV, D // NDEV, D) for g in gw["qb"]], axis=1),
        jnp.stack([g.reshape(NDEV, D // NDEV, D) for g in gw["ob"]], axis=1),
        _slots_from_cols(g_kvf_full[:, :KVF]),
        jnp.stack(gw["up"], axis=1),
        jnp.stack([g.reshape(NDEV, DFF // NDEV, D) for g in gw["down"]], axis=1),
    ]
    received = exchange("scatter_grads", parts, scatter=True)
    big_m = [m_w_qkv_a, m_w_o_a, m_w_q_b, m_w_o_b, m_w_kvf, m_w_up, m_w_down]
    big_v = [v_w_qkv_a, v_w_o_a, v_w_q_b, v_w_o_b, v_w_kvf, v_w_up, v_w_down]
    big_out = []
    for t, (w, m, v, rec) in enumerate(zip(big, big_m, big_v, received)):
        cols = w.shape[-1]
        flat = lambda a: a.reshape(-1, cols)
        res = adamw(f"adamw{t}", rec.reshape(NDEV, -1, cols), flat(w), flat(m), flat(v))
        big_out.append([r.reshape(w.shape) for r in res])

    def pick(k):
        b = [o[k] for o in big_out]
        sm = {0: small_g, 1: s_delta, 2: s_m, 3: s_v}[k]
        return [sm[0], b[0], b[1], b[2], b[3], sm[1], b[4], sm[2], b[5], sm[3], sm[4], b[6]]

    return (loss.reshape(()), dh.reshape(1, T, D), *pick(0), *pick(1), *pick(2), *pick(3))
```

```python
import functools
import math

import jax
import jax.numpy as jnp
from jax import lax
from jax.experimental import pallas as pl
from jax.experimental.pallas import tpu as pltpu

F32 = jnp.float32
BF16 = jnp.bfloat16

T = 2048
D = 1024
DEPTH = 4
N_A = 2
HD = 64
A_GROUPS = ((128, 1), (512, 4), (2048, 16))
A_W = 768
B_HEADS = 16
DFF = 2816
NDEV = 8
SLAB = 2 * DFF // NDEV
KVF = 2 * D + B_HEADS
KVF_PAD = 2304
ROPE_DIM = 16
ROPE_THETA = 500000.0
EPS = 1e-6
NEG = -1e30
BLK = 128
LANES = 128
SCALE = HD ** -0.5
VMEM_LIMIT = 56 * 1024 * 1024

ADAM_LR = 0.001
ADAM_B1 = 0.9
ADAM_B2 = 0.999
ADAM_EPS = 1e-08
ADAM_WD = 0.01
ADAM_STEP = 10
HIGHEST = lax.Precision.HIGHEST


def _params(*sem):
    return pltpu.CompilerParams(dimension_semantics=sem or None, vmem_limit_bytes=VMEM_LIMIT)


def _bf(x):
    return x if x.dtype == BF16 else x.astype(BF16)


def matmul(name, a, b, *, ta=False, tb=False, out_dtype=F32, tm=512, tn=512, batch=None):
    a_b = batch in ("a_out", "reduce")
    b_b = batch in ("b_out", "reduce")
    o_b = batch in ("a_out", "b_out")
    nb = a.shape[0] if a_b else (b.shape[0] if b_b else 1)
    ash = a.shape[1:] if a_b else a.shape
    bsh = b.shape[1:] if b_b else b.shape
    m, k = (ash[1], ash[0]) if ta else ash
    k2, n = (bsh[1], bsh[0]) if tb else bsh
    assert k == k2, (name, a.shape, b.shape)
    tm, tn = min(tm, m), min(tn, n)
    assert m % tm == 0 and n % tn == 0, (name, m, n, tm, tn)
    nbr = nb if batch == "reduce" else 1
    grid = (nb if o_b else 1, n // tn, m // tm, nbr)

    def bidx(bo, br):
        return bo if o_b else br

    def spec(batched, block, idx):
        if batched:
            return pl.BlockSpec((None,) + block, lambda bo, j, i, br: (bidx(bo, br),) + idx(i, j))
        return pl.BlockSpec(block, lambda bo, j, i, br: idx(i, j))

    a_spec = spec(a_b, (k, tm) if ta else (tm, k), (lambda i, j: (0, i)) if ta else (lambda i, j: (i, 0)))
    b_spec = spec(b_b, (tn, k) if tb else (k, tn), (lambda i, j: (j, 0)) if tb else (lambda i, j: (0, j)))
    o_spec = spec(o_b, (tm, tn), lambda i, j: (i, j))
    dims = (((0 if ta else 1,), (1 if tb else 0,)), ((), ()))

    def body(a_ref, b_ref, o_ref, *acc):
        p = lax.dot_general(_bf(a_ref[...]), _bf(b_ref[...]), dims, preferred_element_type=F32)
        if nbr == 1:
            o_ref[...] = p.astype(out_dtype)
        else:
            r = pl.program_id(3)

            @pl.when(r == 0)
            def _():
                acc[0][...] = p

            @pl.when(r > 0)
            def _():
                acc[0][...] += p

            @pl.when(r == nbr - 1)
            def _():
                o_ref[...] = acc[0][...].astype(out_dtype)

    out_shape = ((nb,) if o_b else ()) + (m, n)
    return pl.pallas_call(
        body, name=name, grid=grid, in_specs=[a_spec, b_spec], out_specs=o_spec,
        out_shape=jax.ShapeDtypeStruct(out_shape, out_dtype),
        scratch_shapes=[pltpu.VMEM((tm, tn), F32)] if nbr > 1 else [],
        compiler_params=_params("parallel", "parallel", "parallel", "arbitrary"),
    )(a, b)


def _rms(x, g):
    return x * lax.rsqrt(jnp.mean(x * x, axis=-1, keepdims=True) + EPS) * g


def resid_norm(name, h, y, gy, gains, tb=256, dep=None):
    n_g = len(gains)
    has_y = y is not None
    has_dep = dep is not None
    row = pl.BlockSpec((tb, D), lambda i: (i, 0))
    vec = pl.BlockSpec((1, D), lambda i: (0, 0))

    def body(*refs):
        h_ref = refs[0]
        pos = 1
        hn = h_ref[...]
        if has_y:
            hn = hn + _rms(refs[1][...], refs[2][...])
            pos = 3
        g_refs = refs[pos:pos + n_g]
        outs = refs[pos + n_g + has_dep:]
        if has_y:
            outs[0][...] = hn
            outs = outs[1:]
        for g_ref, o_ref in zip(g_refs, outs):
            o_ref[...] = _rms(hn, g_ref[...]).astype(BF16)

    ins = [h] + ([y, gy.reshape(1, D)] if has_y else []) + [g.reshape(1, D) for g in gains] + ([dep] if has_dep else [])
    in_specs = [row] + ([row, vec] if has_y else []) + [vec] * n_g + [pl.BlockSpec(memory_space=pl.ANY)] * has_dep
    out_shape = ([jax.ShapeDtypeStruct((T, D), F32)] if has_y else []) + [jax.ShapeDtypeStruct((T, D), BF16)] * n_g
    res = pl.pallas_call(
        body, name=name, grid=(T // tb,), in_specs=in_specs, out_specs=[row] * len(out_shape),
        out_shape=out_shape, compiler_params=_params("parallel"),
    )(*ins)
    return (res[0], list(res[1:])) if has_y else (h, list(res))


def rms_bwd(name, x, g, dy, add=None, out_dtype=F32, tb=256, dep=None):
    has_add = add is not None
    has_dep = dep is not None
    row = pl.BlockSpec((tb, D), lambda i: (i, 0))
    vec = pl.BlockSpec((1, D), lambda i: (0, 0))

    def body(*refs):
        x_ref, g_ref, dy_ref = refs[:3]
        dx_ref, dg_ref = refs[-2:]
        xv = x_ref[...]
        dyv = dy_ref[...].astype(F32)
        r = lax.rsqrt(jnp.mean(xv * xv, axis=-1, keepdims=True) + EPS)
        gdy = dyv * g_ref[...]
        dx = r * gdy - xv * (r * r * r * jnp.mean(xv * gdy, axis=-1, keepdims=True))
        if has_add:
            dx = dx + refs[3][...]
        dx_ref[...] = dx.astype(out_dtype)
        part = jnp.sum(dyv * xv * r, axis=0, keepdims=True)

        @pl.when(pl.program_id(0) == 0)
        def _():
            dg_ref[...] = part

        @pl.when(pl.program_id(0) > 0)
        def _():
            dg_ref[...] += part

    ins = [x, g.reshape(1, D), dy] + ([add] if has_add else []) + ([dep] if has_dep else [])
    return pl.pallas_call(
        body, name=name, grid=(T // tb,),
        in_specs=[row, vec, row] + ([row] if has_add else []) + [pl.BlockSpec(memory_space=pl.ANY)] * has_dep,
        out_specs=[row, vec],
        out_shape=[jax.ShapeDtypeStruct((T, D), out_dtype), jax.ShapeDtypeStruct((1, D), F32)],
        compiler_params=_params("arbitrary"),
    )(*ins)


def rope_tables():
    pos = jnp.arange(T, dtype=F32)
    inv = ROPE_THETA ** (-jnp.arange(0, ROPE_DIM, 2, dtype=F32) / ROPE_DIM)
    ang = pos[:, None] * inv[None, :]
    cos, sin = jnp.cos(ang), jnp.sin(ang)
    half = ROPE_DIM // 2
    one = jnp.ones((T, HD - ROPE_DIM), F32)
    zero = jnp.zeros((T, HD - ROPE_DIM), F32)
    zh = jnp.zeros((T, half), F32)
    c = jnp.concatenate([cos, cos, one], axis=1)
    s_up = jnp.concatenate([zh, sin, zero], axis=1)
    s_dn = jnp.concatenate([-sin, zh, zero], axis=1)
    rep = LANES // HD
    return tuple(jnp.tile(t, (1, rep)) for t in (c, s_up, s_dn))


def rope(name, t, tables, inverse, out_dtype=BF16, tb=512):
    c, s_up, s_dn = tables
    n_rot = 2 * A_W // LANES
    half = ROPE_DIM // 2
    blk = pl.BlockSpec((tb, LANES), lambda i, j: (i, j))
    tab = pl.BlockSpec((tb, LANES), lambda i, j: (i, 0))

    def body(t_ref, c_ref, su_ref, sd_ref, o_ref):
        x = t_ref[...].astype(F32)
        sgn = -1.0 if inverse else 1.0
        rot = (x * c_ref[...] + pltpu.roll(x, half, 1) * (sgn * su_ref[...])
               + pltpu.roll(x, LANES - half, 1) * (sgn * sd_ref[...]))
        o_ref[...] = jnp.where(pl.program_id(1) < n_rot, rot, x).astype(out_dtype)

    return pl.pallas_call(
        body, name=name, grid=(T // tb, 3 * A_W // LANES), in_specs=[blk, tab, tab, tab], out_specs=blk,
        out_shape=jax.ShapeDtypeStruct((T, 3 * A_W), out_dtype), compiler_params=_params("parallel", "parallel"),
    )(t, c, s_up, s_dn)


GW = 4 * HD


def _band_mask(b):
    qi = lax.broadcasted_iota(jnp.int32, (BLK, 2 * BLK), 0)
    kj = lax.broadcasted_iota(jnp.int32, (BLK, 2 * BLK), 1)
    return (kj <= qi + BLK) & (kj >= qi) & ((kj >= BLK) | (b > 0))


def attn_a_fwd(name, qkvr, g, r):
    length = T // r
    nblk = length // BLK
    view = qkvr.reshape(length, r * 3 * A_W)
    ncol = 3 * A_W // GW

    def col(section, prev):
        def idx(j, b):
            return (jnp.maximum(b - 1, 0) if prev else b, j * ncol + 3 * section + g)
        return pl.BlockSpec((BLK, GW), idx)

    out = pl.BlockSpec((BLK, GW), lambda j, b: (b, j))

    def body(q_ref, kp_ref, kc_ref, vp_ref, vc_ref, o_ref, lse_ref):
        mask = _band_mask(pl.program_id(1))
        k2 = jnp.concatenate([kp_ref[...], kc_ref[...]], axis=0)
        v2 = jnp.concatenate([vp_ref[...], vc_ref[...]], axis=0)
        q = q_ref[...]
        for h in range(4):
            sl = slice(h * HD, (h + 1) * HD)
            s = lax.dot_general(q[:, sl], k2[:, sl], (((1,), (1,)), ((), ())), preferred_element_type=F32) * SCALE
            s = jnp.where(mask, s, NEG)
            m = jnp.max(s, axis=1, keepdims=True)
            p = jnp.exp(s - m)
            l = jnp.sum(p, axis=1, keepdims=True)
            o_ref[:, sl] = jnp.dot((p / l).astype(BF16), v2[:, sl], preferred_element_type=F32)
            lse_ref[:, sl] = jnp.broadcast_to(m + jnp.log(l), (BLK, HD))

    shape = jax.ShapeDtypeStruct((length, r * GW), F32)
    o, lse = pl.pallas_call(
        body, name=name, grid=(r, nblk),
        in_specs=[col(0, False), col(1, True), col(1, False), col(2, True), col(2, False)],
        out_specs=[out, out], out_shape=[shape, shape], compiler_params=_params("parallel", "parallel"),
    )(view, view, view, view, view)
    return o.reshape(T, GW), lse.reshape(T, GW)


def attn_a_bwd(name, qkvr, do, lse, dd, g, r):
    length = T // r
    nblk = length // BLK
    view = qkvr.reshape(length, r * 3 * A_W)
    ncol = 3 * A_W // GW

    def col(section, shift):
        def idx(j, b):
            return (jnp.clip(b + shift, 0, nblk - 1), j * ncol + 3 * section + g)
        return pl.BlockSpec((BLK, GW), idx)

    def tok(shift):
        return pl.BlockSpec((BLK, GW), lambda j, b: (jnp.clip(b + shift, 0, nblk - 1), j))

    def body(q_ref, qn_ref, kp_ref, kc_ref, vp_ref, vc_ref, do_ref, don_ref, lse_ref, lsen_ref, dd_ref, ddn_ref,
             dq_ref, dk_ref, dv_ref):
        b = pl.program_id(1)
        mask = _band_mask(b)
        qi = lax.broadcasted_iota(jnp.int32, (2 * BLK, BLK), 0)
        kj = lax.broadcasted_iota(jnp.int32, (2 * BLK, BLK), 1)
        kmask = ((qi < BLK) & (kj <= qi)) | ((qi >= BLK) & (kj >= qi - BLK) & (b + 1 < nblk))
        k2 = jnp.concatenate([kp_ref[...], kc_ref[...]], axis=0)
        v2 = jnp.concatenate([vp_ref[...], vc_ref[...]], axis=0)
        q2 = jnp.concatenate([q_ref[...], qn_ref[...]], axis=0)
        do2 = jnp.concatenate([do_ref[...], don_ref[...]], axis=0)
        lse2 = jnp.concatenate([lse_ref[...], lsen_ref[...]], axis=0)
        dd2 = jnp.concatenate([dd_ref[...], ddn_ref[...]], axis=0)
        nt = (((1,), (1,)), ((), ()))
        tn = (((0,), (0,)), ((), ()))
        for h in range(4):
            sl = slice(h * HD, (h + 1) * HD)
            one = slice(h * HD, h * HD + 1)
            qh, kh, vh, doh = q2[:, sl], k2[:, sl], v2[:, sl], do2[:, sl]
            s = lax.dot_general(qh[:BLK], kh, nt, preferred_element_type=F32) * SCALE
            p = jnp.where(mask, jnp.exp(s - lse2[:BLK, one]), 0.0)
            dp = lax.dot_general(doh[:BLK], vh, nt, preferred_element_type=F32)
            ds = p * (dp + dd2[:BLK, one])
            dq_ref[:, sl] = jnp.dot(ds.astype(BF16), kh, preferred_element_type=F32) * SCALE
            kc, vc = kh[BLK:], vh[BLK:]
            s = lax.dot_general(qh, kc, nt, preferred_element_type=F32) * SCALE
            p = jnp.where(kmask, jnp.exp(s - lse2[:, one]), 0.0)
            dp = lax.dot_general(doh, vc, nt, preferred_element_type=F32)
            ds = p * (dp + dd2[:, one])
            dk_ref[:, sl] = lax.dot_general(ds.astype(BF16), qh, tn, preferred_element_type=F32) * SCALE
            dv_ref[:, sl] = lax.dot_general(p.astype(BF16), doh, tn, preferred_element_type=F32)

    dov = do.reshape(length, r * GW)
    lsev = lse.reshape(length, r * GW)
    ddv = dd.reshape(length, r * GW)
    shape = jax.ShapeDtypeStruct((length, r * GW), F32)
    dq, dk, dv = pl.pallas_call(
        body, name=name, grid=(r, nblk),
        in_specs=[col(0, 0), col(0, 1), col(1, -1), col(1, 0), col(2, -1), col(2, 0),
                  tok(0), tok(1), tok(0), tok(1), tok(0), tok(1)],
        out_specs=[tok(0)] * 3, out_shape=[shape] * 3, compiler_params=_params("parallel", "parallel"),
    )(view, view, view, view, view, view, dov, dov, lsev, lsev, ddv, ddv)
    return dq.reshape(T, GW), dk.reshape(T, GW), dv.reshape(T, GW)


def _head_sum(x):
    i = lax.div(lax.broadcasted_iota(jnp.int32, (GW, GW), 0), jnp.int32(HD))
    j = lax.div(lax.broadcasted_iota(jnp.int32, (GW, GW), 1), jnp.int32(HD))
    return jnp.dot(x, (i == j).astype(F32), precision=HIGHEST, preferred_element_type=F32)


def _alphas(lses):
    m = jnp.maximum(jnp.maximum(lses[0], lses[1]), lses[2])
    e = [jnp.exp(l - m) for l in lses]
    z = e[0] + e[1] + e[2]
    return [x / z for x in e]


def combine_fwd(name, os_, lses, tb=256):
    blk = pl.BlockSpec((tb, GW), lambda i: (i, 0))

    def body(o0, o1, o2, l0, l1, l2, oc_ref):
        al = _alphas([l0[...], l1[...], l2[...]])
        for g, o_ref in enumerate((o0, o1, o2)):
            oc_ref[:, g * GW:(g + 1) * GW] = (o_ref[...] * al[g]).astype(BF16)

    return pl.pallas_call(
        body, name=name, grid=(T // tb,), in_specs=[blk] * 6, out_specs=pl.BlockSpec((tb, A_W), lambda i: (i, 0)),
        out_shape=jax.ShapeDtypeStruct((T, A_W), BF16), compiler_params=_params("parallel"),
    )(*os_, *lses)


def combine_bwd(name, doc, os_, lses, tb=256):
    blk = pl.BlockSpec((tb, GW), lambda i: (i, 0))

    def body(doc_ref, o0, o1, o2, l0, l1, l2, d0, d1, d2, e0, e1, e2):
        al = _alphas([l0[...], l1[...], l2[...]])
        dal = [_head_sum(doc_ref[:, g * GW:(g + 1) * GW] * o_ref[...]) for g, o_ref in enumerate((o0, o1, o2))]
        mean = al[0] * dal[0] + al[1] * dal[1] + al[2] * dal[2]
        for g, (do_ref, dd_ref) in enumerate(((d0, e0), (d1, e1), (d2, e2))):
            do_ref[...] = (doc_ref[:, g * GW:(g + 1) * GW] * al[g]).astype(BF16)
            dd_ref[...] = al[g] * (dal[g] - mean) - al[g] * dal[g]

    res = pl.pallas_call(
        body, name=name, grid=(T // tb,), in_specs=[pl.BlockSpec((tb, A_W), lambda i: (i, 0))] + [blk] * 6,
        out_specs=[blk] * 6,
        out_shape=[jax.ShapeDtypeStruct((T, GW), BF16)] * 3 + [jax.ShapeDtypeStruct((T, GW), F32)] * 3,
        compiler_params=_params("parallel"),
    )(doc, *os_, *lses)
    return res[:3], res[3:]


TQ = 128
TK = 256
FOX_HEADS = 4
NT = (((1,), (1,)), ((), ()))
TN = (((0,), (0,)), ((), ()))


def _fox_scores(q, kb, cj, row0, col0, masked):
    s = lax.dot_general(q, kb, NT, preferred_element_type=F32) * SCALE - cj
    if masked:
        qi = row0 + lax.broadcasted_iota(jnp.int32, s.shape, 0)
        kj = col0 + lax.broadcasted_iota(jnp.int32, s.shape, 1)
        s = jnp.where(kj <= qi, s, NEG)
    return s


def fox_fwd(name, q, k, v, ct):
    def body(q_ref, k_ref, v_ref, c_ref, o_ref, lse_ref):
        i = pl.program_id(1)
        n_full = lax.div(i, jnp.int32(2))

        def step(jb, carry, masked):
            start = pl.multiple_of(jb * TK, TK)
            out = []
            for h in range(FOX_HEADS):
                m, l, acc = carry[h]
                kb = k_ref[h, pl.ds(start, TK), :]
                vb = v_ref[h, pl.ds(start, TK), :]
                s = _fox_scores(q_ref[h], kb, c_ref[h, pl.ds(jb, 1), :], i * TQ, jb * TK, masked)
                m_new = jnp.maximum(m, jnp.max(s, axis=1, keepdims=True))
                a = jnp.exp(m - m_new)
                p = jnp.exp(s - m_new)
                l = a * l + jnp.sum(p, axis=1, keepdims=True)
                acc = a * acc + jnp.dot(p.astype(BF16), vb, preferred_element_type=F32)
                out.append((m_new, l, acc))
            return tuple(out)

        init = tuple((jnp.full((TQ, 1), NEG, F32), jnp.zeros((TQ, 1), F32), jnp.zeros((TQ, HD), F32))
                     for _ in range(FOX_HEADS))
        carry = lax.fori_loop(0, n_full, lambda jb, c: step(jb, c, False), init)
        carry = step(n_full, carry, True)
        for h in range(FOX_HEADS):
            m, l, acc = carry[h]
            o_ref[h] = (acc / l).astype(BF16)
            lse_ref[h] = jnp.broadcast_to(m + jnp.log(l), (TQ, LANES))

    head = lambda h, i: (h, 0, 0)
    return pl.pallas_call(
        body, name=name, grid=(B_HEADS // FOX_HEADS, T // TQ),
        in_specs=[pl.BlockSpec((FOX_HEADS, TQ, HD), lambda h, i: (h, i, 0)), pl.BlockSpec((FOX_HEADS, T, HD), head),
                  pl.BlockSpec((FOX_HEADS, T, HD), head), pl.BlockSpec((FOX_HEADS, T // TK, TK), head)],
        out_specs=[pl.BlockSpec((FOX_HEADS, TQ, HD), lambda h, i: (h, i, 0)),
                   pl.BlockSpec((FOX_HEADS, TQ, LANES), lambda h, i: (h, i, 0))],
        out_shape=[jax.ShapeDtypeStruct((B_HEADS, T, HD), BF16), jax.ShapeDtypeStruct((B_HEADS, T, LANES), F32)],
        compiler_params=_params("parallel", "parallel"),
    )(q, k, v, ct)


def fox_bwd_q(name, q, k, v, ct, o, do, lse):
    def body(q_ref, k_ref, v_ref, c_ref, o_ref, do_ref, lse_ref, dq_ref, drow_ref):
        i = pl.program_id(1)
        n_full = lax.div(i, jnp.int32(2))
        delta = [jnp.sum(do_ref[h].astype(F32) * o_ref[h].astype(F32), axis=1, keepdims=True)
                 for h in range(FOX_HEADS)]

        def step(jb, carry, masked):
            start = pl.multiple_of(jb * TK, TK)
            out = []
            for h in range(FOX_HEADS):
                dq, drow = carry[h]
                kb = k_ref[h, pl.ds(start, TK), :]
                vb = v_ref[h, pl.ds(start, TK), :]
                s = _fox_scores(q_ref[h], kb, c_ref[h, pl.ds(jb, 1), :], i * TQ, jb * TK, masked)
                p = jnp.exp(s - lse_ref[h, :, 0:1])
                dp = lax.dot_general(do_ref[h], vb, NT, preferred_element_type=F32)
                ds = p * (dp - delta[h])
                out.append((dq + jnp.dot(ds.astype(BF16), kb, preferred_element_type=F32),
                            drow + jnp.sum(ds, axis=1, keepdims=True)))
            return tuple(out)

        init = tuple((jnp.zeros((TQ, HD), F32), jnp.zeros((TQ, 1), F32)) for _ in range(FOX_HEADS))
        carry = lax.fori_loop(0, n_full, lambda jb, c: step(jb, c, False), init)
        carry = step(n_full, carry, True)
        for h in range(FOX_HEADS):
            dq, drow = carry[h]
            dq_ref[h] = (dq * SCALE).astype(BF16)
            drow_ref[h] = jnp.broadcast_to(drow, (TQ, LANES))

    head = lambda h, i: (h, 0, 0)
    blk = pl.BlockSpec((FOX_HEADS, TQ, HD), lambda h, i: (h, i, 0))
    stat = pl.BlockSpec((FOX_HEADS, TQ, LANES), lambda h, i: (h, i, 0))
    return pl.pallas_call(
        body, name=name, grid=(B_HEADS // FOX_HEADS, T // TQ),
        in_specs=[blk, pl.BlockSpec((FOX_HEADS, T, HD), head), pl.BlockSpec((FOX_HEADS, T, HD), head),
                  pl.BlockSpec((FOX_HEADS, T // TK, TK), head), blk, blk, stat],
        out_specs=[blk, stat],
        out_shape=[jax.ShapeDtypeStruct((B_HEADS, T, HD), BF16), jax.ShapeDtypeStruct((B_HEADS, T, LANES), F32)],
        compiler_params=_params("parallel", "parallel"),
    )(q, k, v, ct, o, do, lse)


def fox_bwd_kv(name, q, k, v, ct, o, do, lse, prev):
    has_prev = prev is not None
    nq = T // TQ

    def body(*refs):
        q_ref, k_ref, v_ref, c_ref, o_ref, do_ref, lse_ref = refs[:7]
        dk_ref, dv_ref, dc_ref = refs[-3:]
        jb = pl.program_id(1)
        dk_ref[...] = jnp.zeros_like(dk_ref)
        dv_ref[...] = jnp.zeros_like(dv_ref)

        def step(i, dcs, masked):
            start = pl.multiple_of(i * TQ, TQ)
            out = []
            for h in range(FOX_HEADS):
                qv = q_ref[h, pl.ds(start, TQ), :]
                dov = do_ref[h, pl.ds(start, TQ), :]
                ov = o_ref[h, pl.ds(start, TQ), :]
                lsev = lse_ref[h, pl.ds(start, TQ), 0:1]
                delta = jnp.sum(dov.astype(F32) * ov.astype(F32), axis=1, keepdims=True)
                s = _fox_scores(qv, k_ref[h], c_ref[h, pl.ds(jb, 1), :], i * TQ, jb * TK, masked)
                p = jnp.exp(s - lsev)
                dp = lax.dot_general(dov, v_ref[h], NT, preferred_element_type=F32)
                ds = p * (dp - delta)
                dv_ref[h] += lax.dot_general(p.astype(BF16), dov, TN, preferred_element_type=F32)
                dk_ref[h] += lax.dot_general(ds.astype(BF16), qv, TN, preferred_element_type=F32)
                out.append(dcs[h] - jnp.sum(ds, axis=0, keepdims=True))
            return tuple(out)

        first = 2 * jb
        dcs = tuple(jnp.zeros((1, TK), F32) for _ in range(FOX_HEADS))
        dcs = step(first, dcs, True)
        dcs = step(first + 1, dcs, True)
        dcs = lax.fori_loop(first + 2, nq, lambda i, c: step(i, c, False), dcs)
        for h in range(FOX_HEADS):
            dk = dk_ref[h] * SCALE
            dc = dcs[h]
            if has_prev:
                dk = dk + refs[7][h]
                dv_ref[h] += refs[8][h]
                dc = dc + refs[9][h, pl.ds(jb, 1), :]
            dk_ref[h] = dk
            dc_ref[h, pl.ds(jb, 1), :] = dc

    head = lambda h, j: (h, 0, 0)
    full = pl.BlockSpec((FOX_HEADS, T, HD), head)
    blk = pl.BlockSpec((FOX_HEADS, TK, HD), lambda h, j: (h, j, 0))
    cspec = pl.BlockSpec((FOX_HEADS, T // TK, TK), head)
    ins = [q, k, v, ct, o, do, lse] + (list(prev) if has_prev else [])
    in_specs = [full, blk, blk, cspec, full, full, pl.BlockSpec((FOX_HEADS, T, LANES), head)] + ([blk, blk, cspec] if has_prev else [])
    return pl.pallas_call(
        body, name=name, grid=(B_HEADS // FOX_HEADS, T // TK), in_specs=in_specs, out_specs=[blk, blk, cspec],
        out_shape=[jax.ShapeDtypeStruct((B_HEADS, T, HD), F32)] * 2 + [jax.ShapeDtypeStruct((B_HEADS, T // TK, TK), F32)],
        compiler_params=_params("parallel", "arbitrary"),
    )(*ins)


def _tri(upper):
    i = lax.broadcasted_iota(jnp.int32, (BLK, BLK), 0)
    j = lax.broadcasted_iota(jnp.int32, (BLK, BLK), 1)
    return ((i <= j) if upper else (i >= j)).astype(F32)


def gates_fwd(name, fgt, b_f):
    def body(f_ref, b_ref, c_ref):
        tri = _tri(True)
        carry = jnp.zeros((B_HEADS, 1), F32)
        for blk in range(T // BLK):
            sl = slice(blk * BLK, (blk + 1) * BLK)
            z = f_ref[:, sl] + b_ref[...]
            logf = jnp.minimum(z, 0.0) - jnp.log(1.0 + jnp.exp(-jnp.abs(z)))
            cs = jnp.dot(logf, tri, precision=HIGHEST, preferred_element_type=F32) + carry
            c_ref[:, sl] = cs
            carry = cs[:, BLK - 1:BLK]

    return pl.pallas_call(
        body, name=name, out_shape=jax.ShapeDtypeStruct((B_HEADS, T), F32), compiler_params=_params(),
    )(fgt, b_f)


def gates_bwd(name, fgt, b_f, dcs):
    n_dc = len(dcs)

    def body(*refs):
        f_ref, b_ref = refs[:2]
        dc_refs = refs[2:2 + n_dc]
        dz_ref, db_ref = refs[2 + n_dc:]
        tri = _tri(False)
        carry = jnp.zeros((B_HEADS, 1), F32)
        db = jnp.zeros((B_HEADS, 1), F32)
        for blk in reversed(range(T // BLK)):
            sl = slice(blk * BLK, (blk + 1) * BLK)
            dc = dc_refs[0][:, sl]
            for r in dc_refs[1:]:
                dc = dc + r[:, sl]
            rc = jnp.dot(dc, tri, precision=HIGHEST, preferred_element_type=F32) + carry
            carry = rc[:, 0:1]
            z = f_ref[:, sl] + b_ref[...]
            e = jnp.exp(-jnp.abs(z))
            dz = rc * jnp.where(z >= 0.0, e, 1.0) / (1.0 + e)
            dz_ref[:, sl] = dz
            db = db + jnp.sum(dz, axis=1, keepdims=True)
        db_ref[...] = db

    return pl.pallas_call(
        body, name=name,
        out_shape=[jax.ShapeDtypeStruct((B_HEADS, T), F32), jax.ShapeDtypeStruct((B_HEADS, 1), F32)],
        compiler_params=_params(),
    )(fgt, b_f, *dcs)


CONV_TB = 256
GELU_K = math.sqrt(2.0 / math.pi)
GELU_C = 0.044715


def _shift_down(x, halo_ref, n):
    rows = lax.broadcasted_iota(jnp.int32, x.shape, 0)
    y = pltpu.roll(x, n, 0)
    for k in range(n):
        y = jnp.where(rows == k, halo_ref[pl.ds(8 - n + k, 1), :], y)
    return y


def _conv(x, halo_ref, cw_ref, cb_ref, first):
    x1 = _shift_down(x, halo_ref, 1)
    x2 = _shift_down(x, halo_ref, 2)
    rows = lax.broadcasted_iota(jnp.int32, x.shape, 0)
    x1 = jnp.where(first & (rows < 1), 0.0, x1)
    x2 = jnp.where(first & (rows < 2), 0.0, x2)
    y = x2 * cw_ref[0:1, :] + x1 * cw_ref[1:2, :] + x * cw_ref[2:3, :] + cb_ref[...]
    return y, x1, x2


def _gelu_parts(x):
    th = jnp.tanh(GELU_K * (x + GELU_C * x * x * x))
    val = 0.5 * x * (1.0 + th)
    grad = 0.5 * (1.0 + th) + 0.5 * x * (1.0 - th * th) * GELU_K * (1.0 + 3.0 * GELU_C * x * x)
    return val, grad


def _conv_specs(tb):
    def slab(off):
        return pl.BlockSpec((None, tb, SLAB), lambda d, i: (d + off, i, 0))

    def halo(off):
        return pl.BlockSpec((None, 8, SLAB), lambda d, i: (d + off, jnp.maximum(i * (tb // 8) - 1, 0), 0))

    def par(rows, off):
        return pl.BlockSpec((None, rows, SLAB), lambda d, i: (d + off, 0, 0))

    return slab, halo, par


def convglu_fwd(name, a, cw, cb, tb=CONV_TB):
    slab, halo, par = _conv_specs(tb)

    def body(ag, hg, av, hv, cwg, cbg, cwv, cbv, u_ref):
        first = pl.program_id(1) == 0
        gate, _, _ = _conv(ag[...], hg, cwg, cbg, first)
        val, _, _ = _conv(av[...], hv, cwv, cbv, first)
        u_ref[...] = (_gelu_parts(gate)[0] * val).astype(BF16)

    return pl.pallas_call(
        body, name=name, grid=(4, T // tb),
        in_specs=[slab(0), halo(0), slab(4), halo(4), par(3, 0), par(1, 0), par(3, 4), par(1, 4)],
        out_specs=slab(0), out_shape=jax.ShapeDtypeStruct((4, T, SLAB), BF16),
        compiler_params=_params("parallel", "parallel"),
    )(a, a, a, a, cw, cb, cw, cb)


def convglu_bwd(name, du, a, cw, cb, tb=CONV_TB):
    slab, halo, par = _conv_specs(tb)

    def body(du_ref, ag, hg, av, hv, cwg, cbg, cwv, cbv, dg_ref, dv_ref, dcwg, dcbg, dcwv, dcbv):
        first = pl.program_id(1) == 0
        gate, g1, g2 = _conv(ag[...], hg, cwg, cbg, first)
        val, v1, v2 = _conv(av[...], hv, cwv, cbv, first)
        act, dact = _gelu_parts(gate)
        duv = du_ref[...].astype(F32)
        dgate = duv * val * dact
        dval = duv * act
        dg_ref[...] = dgate
        dv_ref[...] = dval
        for dy, xs, dcw_ref, dcb_ref in ((dgate, (g2, g1, ag[...]), dcwg, dcbg), (dval, (v2, v1, av[...]), dcwv, dcbv)):
            parts = [jnp.sum(dy * x, axis=0, keepdims=True) for x in xs]
            bias = jnp.sum(dy, axis=0, keepdims=True)

            @pl.when(first)
            def _():
                for k in range(3):
                    dcw_ref[k:k + 1, :] = parts[k]
                dcb_ref[...] = bias

            @pl.when(jnp.logical_not(first))
            def _():
                for k in range(3):
                    dcw_ref[k:k + 1, :] += parts[k]
                dcb_ref[...] += bias

    res = pl.pallas_call(
        body, name=name, grid=(4, T // tb),
        in_specs=[slab(0), slab(0), halo(0), slab(4), halo(4), par(3, 0), par(1, 0), par(3, 4), par(1, 4)],
        out_specs=[slab(0), slab(0), par(3, 0), par(1, 0), par(3, 0), par(1, 0)],
        out_shape=[jax.ShapeDtypeStruct((4, T, SLAB), F32)] * 2
        + [jax.ShapeDtypeStruct((4, 3, SLAB), F32), jax.ShapeDtypeStruct((4, 1, SLAB), F32)] * 2,
        compiler_params=_params("parallel", "arbitrary"),
    )(du, a, a, a, a, cw, cb, cw, cb)
    dgate, dval, dcwg, dcbg, dcwv, dcbv = res
    return (dgate, dval), jnp.concatenate([dcwg, dcwv], axis=0), jnp.concatenate([dcbg, dcbv], axis=0)


def conv_bwd_input(name, dac_pair, cw, tb=CONV_TB):
    nblk = T // tb

    def run(x, off):
        def body(x_ref, nx_ref, cw_ref, da_ref):
            last = pl.program_id(1) == nblk - 1
            xv = x_ref[...]
            rows = lax.broadcasted_iota(jnp.int32, xv.shape, 0)

            def up(n):
                y = pltpu.roll(xv, tb - n, 0)
                for k in range(n):
                    y = jnp.where(rows == tb - n + k, nx_ref[pl.ds(k, 1), :], y)
                return jnp.where(last & (rows >= tb - n), 0.0, y)

            da_ref[...] = (xv * cw_ref[2:3, :] + up(1) * cw_ref[1:2, :] + up(2) * cw_ref[0:1, :]).astype(BF16)

        return pl.pallas_call(
            body, name=f"{name}_{off}", grid=(4, nblk),
            in_specs=[pl.BlockSpec((None, tb, SLAB), lambda d, i: (d, i, 0)),
                      pl.BlockSpec((None, 8, SLAB), lambda d, i: (d, jnp.minimum((i + 1) * (tb // 8), T // 8 - 1), 0)),
                      pl.BlockSpec((None, 3, SLAB), lambda d, i: (d + off, 0, 0))],
            out_specs=pl.BlockSpec((None, tb, SLAB), lambda d, i: (d, i, 0)),
            out_shape=jax.ShapeDtypeStruct((4, T, SLAB), BF16), compiler_params=_params("parallel", "parallel"),
        )(x, x, cw)

    return jnp.concatenate([run(dac_pair[0], 0), run(dac_pair[1], 4)], axis=0)


def loss_head(name, y, target, tb=256):
    row = pl.BlockSpec((tb, D), lambda i: (i, 0))

    def body(y_ref, t_ref, dy_ref, loss_ref):
        diff = y_ref[...] - t_ref[...]
        dy_ref[...] = diff * (1.0 / D)
        part = jnp.sum(jnp.sum(diff * diff, axis=1, keepdims=True), axis=0, keepdims=True) * (0.5 / D)

        @pl.when(pl.program_id(0) == 0)
        def _():
            loss_ref[...] = part

        @pl.when(pl.program_id(0) > 0)
        def _():
            loss_ref[...] += part

    return pl.pallas_call(
        body, name=name, grid=(T // tb,), in_specs=[row, row],
        out_specs=[row, pl.BlockSpec((1, 1), lambda i: (0, 0))],
        out_shape=[jax.ShapeDtypeStruct((T, D), F32), jax.ShapeDtypeStruct((1, 1), F32)],
        compiler_params=_params("arbitrary"),
    )(y, target)


def _row_tile(rows, cols, bytes_per_elem, budget=6 * 1024 * 1024):
    for tr in (1024, 512, 256, 128, 64, 32, 16, 8):
        if rows % tr == 0 and tr * cols * bytes_per_elem <= budget:
            return tr
    return rows


def adamw(name, parts, w, m, v, row0=0, prev=None):
    n_parts, rows, cols = parts.shape
    rows_all = w.shape[0]
    tr = _row_tile(math.gcd(rows, row0) if row0 else rows, cols, n_parts * parts.dtype.itemsize + 28)
    blk = pl.BlockSpec((tr, cols), lambda i: (row0 // tr + i, 0))
    b1c = 1.0 - ADAM_B1 ** ADAM_STEP
    b2c = 1.0 - ADAM_B2 ** ADAM_STEP
    n_prev = 0 if prev is None else 4

    def body(p_ref, w_ref, m_ref, v_ref, *rest):
        g_ref, d_ref, nm_ref, nv_ref = rest[n_prev:]
        g = p_ref[0].astype(F32)
        for k in range(1, n_parts):
            g = g + p_ref[k].astype(F32)
        nm = ADAM_B1 * m_ref[...] + (1.0 - ADAM_B1) * g
        nv = ADAM_B2 * v_ref[...] + (1.0 - ADAM_B2) * (g * g)
        g_ref[...] = g
        nm_ref[...] = nm
        nv_ref[...] = nv
        d_ref[...] = -ADAM_LR * ((nm / b1c) / (jnp.sqrt(nv / b2c) + ADAM_EPS) + ADAM_WD * w_ref[...])

    return pl.pallas_call(
        body, name=name, grid=(rows // tr,),
        in_specs=[pl.BlockSpec((n_parts, tr, cols), lambda i: (0, i, 0)), blk, blk, blk]
        + [pl.BlockSpec(memory_space=pl.ANY)] * n_prev,
        out_specs=[blk] * 4, out_shape=[jax.ShapeDtypeStruct((rows_all, cols), F32)] * 4,
        input_output_aliases={4 + k: k for k in range(n_prev)}, compiler_params=_params("parallel"),
    )(parts, w, m, v, *(prev or []))


def exchange(name, items, scatter):
    n = len(items)
    hbm = pl.BlockSpec(memory_space=pltpu.HBM)

    def body(*refs):
        ins, outs = refs[:n], refs[n:2 * n]
        send_sems, recv_sems, local_sems = refs[2 * n:]
        x, y, c = lax.axis_index("x"), lax.axis_index("y"), lax.axis_index("c")
        me = 4 * x + 2 * y + c
        copies = []
        for t in range(n):
            own = pltpu.make_async_copy(ins[t].at[me] if scatter else ins[t], outs[t].at[me], local_sems.at[t])
            own.start()
            copies.append(own)
            for rel in range(1, NDEV):
                px = 1 - x if rel & 4 else x
                py = 1 - y if rel & 2 else y
                pc = 1 - c if rel & 1 else c
                src = ins[t].at[4 * px + 2 * py + pc] if scatter else ins[t]
                cp = pltpu.make_async_remote_copy(
                    src_ref=src, dst_ref=outs[t].at[me], send_sem=send_sems.at[t, rel - 1],
                    recv_sem=recv_sems.at[t, rel - 1], device_id=(px, py, pc), device_id_type=pl.DeviceIdType.MESH)
                cp.start()
                copies.append(cp)
        for cp in copies:
            cp.wait()

    out_shape = [jax.ShapeDtypeStruct(it.shape if scatter else (NDEV,) + it.shape, it.dtype) for it in items]
    return pl.pallas_call(
        body, name=name, in_specs=[hbm] * n, out_specs=[hbm] * n, out_shape=out_shape,
        scratch_shapes=[pltpu.SemaphoreType.DMA((n, NDEV - 1)), pltpu.SemaphoreType.DMA((n, NDEV - 1)),
                        pltpu.SemaphoreType.DMA((n,))],
    )(*items)


def _peer(rel, x, y, c):
    return (1 - x if rel & 4 else x, 1 - y if rel & 2 else y, 1 - c if rel & 1 else c)


def _split_copies(ins, lands, send_sems, recv_sems, scatter):
    x, y, c = lax.axis_index("x"), lax.axis_index("y"), lax.axis_index("c")
    me = 4 * x + 2 * y + c
    copies = []
    for t in range(len(ins)):
        for rel in range(1, NDEV):
            px, py, pc = _peer(rel, x, y, c)
            src = ins[t].at[4 * px + 2 * py + pc] if scatter else ins[t]
            copies.append(pltpu.make_async_remote_copy(
                src_ref=src, dst_ref=lands[t].at[me], send_sem=send_sems.at[t * (NDEV - 1) + rel - 1],
                recv_sem=recv_sems.at[t * (NDEV - 1) + rel - 1], device_id=(px, py, pc),
                device_id_type=pl.DeviceIdType.MESH))
    return me, copies


def exchange_start(name, items, scatter, dep=None):
    n = len(items)
    hbm = pl.BlockSpec(memory_space=pltpu.HBM)
    sem = pl.BlockSpec(memory_space=pltpu.SEMAPHORE)
    has_dep = dep is not None
    land_shapes = [it.shape if scatter else (NDEV,) + it.shape for it in items]

    def body(*refs):
        ins, lands = refs[:n], refs[n:2 * n]
        outs = refs[2 * n + has_dep:]
        send_sems, recv_sems, token, local_sem = outs[0], outs[1], outs[2 + 2 * n], outs[3 + 2 * n]
        me, copies = _split_copies(ins, lands, send_sems, recv_sems, scatter)
        for cp in copies:
            cp.start()
        for t in range(n):
            own = pltpu.make_async_copy(ins[t].at[me] if scatter else ins[t], lands[t].at[me], local_sem)
            own.start()
            own.wait()
        token[...] = jnp.zeros_like(token)

    sems = pltpu.SemaphoreType.DMA((n * (NDEV - 1),))
    out_shape = ([sems, sems] + [pltpu.HBM(it.shape, it.dtype) for it in items]
                 + [pltpu.HBM(sh, it.dtype) for sh, it in zip(land_shapes, items)] + [jax.ShapeDtypeStruct((8, LANES), F32)])
    operands = ([pltpu.with_memory_space_constraint(it, pltpu.HBM) for it in items]
                + [pltpu.with_memory_space_constraint(lax.empty(sh, it.dtype), pltpu.HBM) for sh, it in zip(land_shapes, items)]
                + ([dep] if has_dep else []))
    res = pl.pallas_call(
        body, name=name, in_specs=[hbm] * (2 * n) + [pl.BlockSpec(memory_space=pl.ANY)] * has_dep,
        out_specs=[sem, sem] + [hbm] * (2 * n) + [pl.BlockSpec(memory_space=pltpu.VMEM)], out_shape=out_shape,
        input_output_aliases={t: 2 + t for t in range(2 * n)},
        scratch_shapes=[pltpu.SemaphoreType.DMA(())],
        compiler_params=pltpu.CompilerParams(has_side_effects=pltpu.SideEffectType.DATAFLOW_SIDE_EFFECTING),
    )(*operands)
    return (res[0], res[1], list(res[2:2 + n]), list(res[2 + n:2 + 2 * n]), scatter), res[2 + 2 * n]


def exchange_wait(name, handle, after):
    send_sems, recv_sems, ins, lands, scatter = handle
    n = len(ins)
    hbm = pl.BlockSpec(memory_space=pltpu.HBM)
    sem = pl.BlockSpec(memory_space=pltpu.SEMAPHORE)

    def body(*refs):
        _, copies = _split_copies(refs[:n], refs[n:2 * n], refs[2 * n], refs[2 * n + 1], scatter)
        for cp in copies:
            cp.wait_send()
            cp.wait_recv()

    res = pl.pallas_call(
        body, name=name, in_specs=[hbm] * (2 * n) + [sem, sem, pl.BlockSpec(memory_space=pl.ANY)],
        out_specs=[hbm] * (2 * n), out_shape=[pltpu.HBM(a.shape, a.dtype) for a in ins + lands],
        input_output_aliases={t: t for t in range(2 * n)},
        compiler_params=pltpu.CompilerParams(has_side_effects=pltpu.SideEffectType.DATAFLOW_SIDE_EFFECTING),
    )(*ins, *lands, send_sems, recv_sems, after)
    return list(res[n:])


def sum_slots(name, parts):
    _, rows, cols = parts.shape

    def body(p_ref, o_ref):
        s = p_ref[0]
        for k in range(1, NDEV):
            s = s + p_ref[k]
        o_ref[...] = s

    return pl.pallas_call(body, name=name, out_shape=jax.ShapeDtypeStruct((rows, cols), F32), compiler_params=_params())(parts)


def _heads(t):
    return t.reshape(T, B_HEADS, HD).transpose(1, 0, 2)


def _unheads(t):
    return t.transpose(1, 0, 2).reshape(T, B_HEADS * HD)


def _cols_from_slots(g):
    return g.transpose(1, 0, 2).reshape(g.shape[1], NDEV * g.shape[2])


def _slots_from_cols(w):
    return w.reshape(w.shape[0], NDEV, w.shape[1] // NDEV).transpose(1, 0, 2)


def _pack(arrays, rows):
    flat = jnp.concatenate([a.reshape(-1).astype(F32) for a in arrays])
    return jnp.pad(flat, (0, rows * LANES - flat.shape[0])).reshape(rows, LANES)


def _unpack(buf, shapes):
    flat = buf.reshape(-1)
    out, pos = [], 0
    for sh in shapes:
        size = math.prod(sh)
        out.append(flat[pos:pos + size].reshape(sh))
        pos += size
    return out


def kernel(x, norm_gains, w_qkv_a, w_o_a, w_q_b, w_o_b, kv_norm, w_kvf, b_f, w_up, conv_w, conv_b, w_down, loss_target, m_norm_gains, m_w_qkv_a, m_w_o_a, m_w_q_b, m_w_o_b, m_kv_norm, m_w_kvf, m_b_f, m_w_up, m_conv_w, m_conv_b, m_w_down, v_norm_gains, v_w_qkv_a, v_w_o_a, v_w_q_b, v_w_o_b, v_kv_norm, v_w_kvf, v_b_f, v_w_up, v_conv_w, v_conv_b, v_w_down):
    me = 4 * lax.axis_index("x") + 2 * lax.axis_index("y") + lax.axis_index("c")
    n_b = DEPTH - N_A

    def bf(a):
        return a.astype(BF16)

    mixer_w = [[bf(w_qkv_a[l]), bf(w_o_a[l])] if l < N_A else [bf(w_q_b[l - N_A]), bf(w_o_b[l - N_A])] for l in range(DEPTH)]
    ffn_w = [[bf(w_up[l]), bf(w_down[l])] for l in range(DEPTH)]
    groups = [mixer_w[0] + [norm_gains, conv_w], ffn_w[0], mixer_w[1] + ffn_w[1] + [bf(w_kvf)],
              mixer_w[2] + ffn_w[2], mixer_w[3] + ffn_w[3]]
    handles, tok = [], None
    for i, items in enumerate(groups):
        hd, tok = exchange_start(f"gather_start{i}", items, False, dep=tok)
        handles.append(hd)

    wqkv, woa, wqb, wob, wup, wdown = {}, {}, {}, {}, {}, {}

    def take_mixer(l, arrived):
        if l < N_A:
            wqkv[l] = _cols_from_slots(arrived[0])
            woa[l] = _cols_from_slots(arrived[1])
        else:
            wqb[l - N_A] = arrived[0].reshape(D, D)
            wob[l - N_A] = arrived[1].reshape(D, D)

    def take_ffn(l, arrived):
        wup[l] = arrived[0]
        wdown[l] = arrived[1].reshape(4, SLAB, D)

    arrived = exchange_wait("gather_wait0", handles[0], tok)
    take_mixer(0, arrived)
    gains = arrived[2].transpose(1, 2, 0, 3).reshape(DEPTH, 4, D)
    cws = [arrived[3][:, l] for l in range(DEPTH)]
    cbs = [conv_b[l].reshape(NDEV, 1, SLAB) for l in range(DEPTH)]
    tables = rope_tables()
    b_col = b_f.reshape(B_HEADS, 1)

    h = x.reshape(T, D)
    _, (xn,) = resid_norm("norm_in", h, None, None, [gains[0, 0]])
    saved = []
    shared = None
    for l in range(DEPTH):
        s = {"h": h, "xn": xn}
        if l >= 1:
            arrived = exchange_wait(f"gather_wait{l + 1}", handles[l + 1], h)
            take_mixer(l, arrived[:2])
            take_ffn(l, arrived[2:4])
            if l == N_A - 1:
                wkvf = jnp.pad(_cols_from_slots(arrived[4]), ((0, 0), (0, KVF_PAD - KVF)))
        if l < N_A:
            qkv = matmul(f"qkv{l}", xn, wqkv[l], tm=1024, tn=768)
            s["qkvr"] = rope(f"rope{l}", qkv, tables, inverse=False)
            os_, lses = [], []
            for g, (_, r) in enumerate(A_GROUPS):
                o, lse = attn_a_fwd(f"attn_a{l}_{g}", s["qkvr"], g, r)
                os_.append(o)
                lses.append(lse)
            s["o"], s["lse"] = os_, lses
            s["oc"] = combine_fwd(f"combine{l}", os_, lses)
            mix = matmul(f"wo_a{l}", s["oc"], woa[l], tm=1024, tn=512)
        else:
            if l == N_A:
                kvf = matmul("kvf", xkv, wkvf, tm=1024, tn=768)
                k_h = _heads(kvf[:, :D].astype(BF16))
                v_h = _heads(kvf[:, D:2 * D].astype(BF16))
                fgt = kvf[:, 2 * D:KVF].T
                ct = gates_fwd("gates", fgt, b_col).reshape(B_HEADS, T // TK, TK)
                shared = {"xkv": xkv, "k": k_h, "v": v_h, "fgt": fgt, "ct": ct, "h": h}
            j = l - N_A
            s["q"] = _heads(matmul(f"wq_b{j}", xn, wqb[j], out_dtype=BF16, tm=1024, tn=512))
            s["o"], s["lse"] = fox_fwd(f"fox{j}", s["q"], shared["k"], shared["v"], shared["ct"])
            s["oc"] = _unheads(s["o"])
            mix = matmul(f"wo_b{j}", s["oc"], wob[j], tm=1024, tn=512)
        s["mix"] = mix
        if l == 0:
            take_ffn(0, exchange_wait("gather_wait1", handles[1], mix))
        s["h1"], (s["xn2"],) = resid_norm(f"norm_mid{l}", h, mix, gains[l, 1], [gains[l, 2]])
        s["a"] = matmul(f"up{l}", s["xn2"], wup[l], tm=1024, tn=SLAB, batch="b_out")
        s["u"] = convglu_fwd(f"convglu{l}", s["a"], cws[l], cbs[l])
        s["f"] = matmul(f"down{l}", s["u"], wdown[l], tm=1024, tn=512, batch="reduce")
        nxt = [gains[l + 1, 0]] if l + 1 < DEPTH else []
        if l == N_A - 1:
            nxt.append(kv_norm)
        h, normed = resid_norm(f"norm_out{l}", s["h1"], s["f"], gains[l, 3], nxt)
        if l + 1 < DEPTH:
            xn = normed[0]
        if l == N_A - 1:
            xkv = normed[1]
        saved.append(s)

    dh, loss_part = loss_head("loss", h, loss_target.reshape(T, D))

    d_gains = [[None] * 4 for _ in range(DEPTH)]
    d_cw, d_cb = [None] * DEPTH, [None] * DEPTH
    gw = {"qkv": [None] * N_A, "oa": [None] * N_A, "qb": [None] * n_b, "ob": [None] * n_b, "up": [None] * DEPTH,
          "down": [None] * DEPTH}
    kv_acc = None
    d_rows = []
    sent = []
    tok = None

    def slots_rows(g):
        return g.reshape(NDEV, g.shape[0] // NDEV, g.shape[1])

    for l in reversed(range(DEPTH)):
        s = saved[l]
        df, d_gains[l][3] = rms_bwd(f"bwd_norm_out{l}", s["f"], gains[l, 3], dh, out_dtype=BF16, dep=tok)
        du = matmul(f"bwd_down_x{l}", df, wdown[l], tb=True, out_dtype=BF16, tm=1024, tn=SLAB, batch="b_out")
        gw["down"][l] = matmul(f"bwd_down_w{l}", s["u"], df, ta=True, out_dtype=BF16, tm=SLAB, tn=512, batch="a_out")
        dac, d_cw[l], d_cb[l] = convglu_bwd(f"bwd_convglu{l}", du, s["a"], cws[l], cbs[l])
        da = conv_bwd_input(f"bwd_conv{l}", dac, cws[l])
        dxn2 = matmul(f"bwd_up_x{l}", da, wup[l], tb=True, tm=1024, tn=512, batch="reduce")
        gw["up"][l] = matmul(f"bwd_up_w{l}", s["xn2"], da, ta=True, out_dtype=BF16, tm=512, tn=SLAB, batch="b_out")
        dh1, d_gains[l][2] = rms_bwd(f"bwd_norm_mid{l}", s["h1"], gains[l, 2], dxn2, add=dh)
        ffn_items = [gw["up"][l], slots_rows(gw["down"][l].reshape(DFF, D))]
        ffn_what = [("up", l), ("down", l)]
        tok = None
        if l == 0:
            hd, tok = exchange_start("scatter_start_ffn0", ffn_items, True)
            sent.append((hd, ffn_what))
            ffn_items, ffn_what = [], []
        dmix, d_gains[l][1] = rms_bwd(f"bwd_norm_mix{l}", s["mix"], gains[l, 1], dh1, out_dtype=BF16, dep=tok)
        if l < N_A:
            doc = matmul(f"bwd_wo_a_x{l}", dmix, woa[l], tb=True, tm=1024, tn=A_W)
            gw["oa"][l] = matmul(f"bwd_wo_a_w{l}", s["oc"], dmix, ta=True, out_dtype=BF16, tm=A_W, tn=512)
            dos, dds = combine_bwd(f"bwd_combine{l}", doc, s["o"], s["lse"])
            cols = [None] * 9
            for g, (_, r) in enumerate(A_GROUPS):
                dq, dk, dv = attn_a_bwd(f"bwd_attn_a{l}_{g}", s["qkvr"], dos[g], s["lse"][g], dds[g], g, r)
                cols[g], cols[3 + g], cols[6 + g] = dq, dk, dv
            dqkv = rope(f"bwd_rope{l}", jnp.concatenate(cols, axis=1), tables, inverse=True)
            dxn = matmul(f"bwd_qkv_x{l}", dqkv, wqkv[l], tb=True, tm=1024, tn=512)
            gw["qkv"][l] = matmul(f"bwd_qkv_w{l}", s["xn"], dqkv, ta=True, out_dtype=BF16, tm=512, tn=768)
        else:
            j = l - N_A
            do = _heads(matmul(f"bwd_wo_b_x{j}", dmix, wob[j], tb=True, out_dtype=BF16, tm=1024, tn=512))
            gw["ob"][j] = matmul(f"bwd_wo_b_w{j}", s["oc"], dmix, ta=True, out_dtype=BF16, tm=512, tn=512)
            args = (s["q"], shared["k"], shared["v"], shared["ct"], s["o"], do, s["lse"])
            dq_h, drow = fox_bwd_q(f"bwd_fox_q{j}", *args)
            dq = _unheads(dq_h)
            d_rows.append(drow[:, :, 0])
            kv_acc = fox_bwd_kv(f"bwd_fox_kv{j}", *args, kv_acc)
            dxn = matmul(f"bwd_wq_b_x{j}", dq, wqb[j], tb=True, tm=1024, tn=512)
            gw["qb"][j] = matmul(f"bwd_wq_b_w{j}", s["xn"], dq, ta=True, out_dtype=BF16, tm=512, tn=512)
        dh, d_gains[l][0] = rms_bwd(f"bwd_norm_in{l}", s["h"], gains[l, 0], dxn, add=dh1)
        if l == N_A:
            dk_h, dv_h, dct = kv_acc
            dfgt, d_bf = gates_bwd("bwd_gates", shared["fgt"], b_col, [dct.reshape(B_HEADS, T)] + d_rows)
            dkvf = jnp.concatenate(
                [_unheads(dk_h).astype(BF16), _unheads(dv_h).astype(BF16), dfgt.T.astype(BF16),
                 jnp.zeros((T, KVF_PAD - KVF), BF16)], axis=1)
            dxkv = matmul("bwd_kvf_x", dkvf, wkvf, tb=True, tm=1024, tn=512)
            g_kvf_full = matmul("bwd_kvf_w", shared["xkv"], dkvf, ta=True, out_dtype=BF16, tm=512, tn=768)
            dh, d_kvn = rms_bwd("bwd_norm_kv", shared["h"], kv_norm, dxkv, add=dh)
        if l < N_A:
            items = [_slots_from_cols(gw["qkv"][l]), _slots_from_cols(gw["oa"][l])]
            what = [("qkv", l), ("oa", l)]
        else:
            items = [slots_rows(gw["qb"][l - N_A]), slots_rows(gw["ob"][l - N_A])]
            what = [("qb", l - N_A), ("ob", l - N_A)]
        if l == N_A:
            items.append(_slots_from_cols(g_kvf_full[:, :KVF]))
            what.append(("kvf", 0))
        hd, tok = exchange_start(f"scatter_start{l}", items + ffn_items, True)
        sent.append((hd, what + ffn_what))

    small_shapes = [(DEPTH, 4, D), (D,), (B_HEADS,), (DEPTH, 3, NDEV * SLAB), (DEPTH, NDEV * SLAB), (1,)]
    small = [
        jnp.stack([jnp.concatenate(row, axis=0) for row in d_gains]),
        d_kvn, d_bf,
        jnp.stack([d.transpose(1, 0, 2).reshape(3, NDEV * SLAB) for d in d_cw]),
        jnp.stack([d.reshape(NDEV * SLAB) for d in d_cb]),
        loss_part,
    ]
    small_rows = 848
    (small_all,) = exchange("gather_small_grads", [_pack(small, small_rows)], scatter=False)
    g_gains_full, g_kvn, g_bf, g_cw_full, g_cb, loss = _unpack(sum_slots("sum_small", small_all), small_shapes)
    g_gains_mine = lax.dynamic_slice_in_dim(g_gains_full, me * (D // NDEV), D // NDEV, axis=2)
    g_cw_mine = lax.dynamic_slice_in_dim(g_cw_full, me * SLAB, SLAB, axis=2)

    small_w = [norm_gains, kv_norm, b_f, conv_w, conv_b]
    small_m = [m_norm_gains, m_kv_norm, m_b_f, m_conv_w, m_conv_b]
    small_v = [v_norm_gains, v_kv_norm, v_b_f, v_conv_w, v_conv_b]
    small_g = [g_gains_mine, g_kvn, g_bf, g_cw_mine, g_cb]
    shapes = [w.shape for w in small_w]
    rows = 320
    res = adamw("adamw_small", _pack(small_g, rows)[None], _pack(small_w, rows), _pack(small_m, rows), _pack(small_v, rows))
    _, s_delta, s_m, s_v = [_unpack(r, shapes) for r in res]

    big = {"qkv": (w_qkv_a, m_w_qkv_a, v_w_qkv_a), "oa": (w_o_a, m_w_o_a, v_w_o_a), "qb": (w_q_b, m_w_q_b, v_w_q_b),
           "ob": (w_o_b, m_w_o_b, v_w_o_b), "kvf": (w_kvf, m_w_kvf, v_w_kvf), "up": (w_up, m_w_up, v_w_up),
           "down": (w_down, m_w_down, v_w_down)}
    updated = {name: None for name in big}
    after = s_delta[0]
    for handle, what in sent:
        received = exchange_wait(f"scatter_wait_{what[0][0]}{what[0][1]}", handle, after)
        for (name, layer), rec in zip(what, received):
            cols = rec.shape[-1]
            rows = rec.size // (NDEV * cols)
            w, m, v = (a.reshape(-1, cols) for a in big[name])
            updated[name] = adamw(f"adamw_{name}{layer}", rec.reshape(NDEV, rows, cols), w, m, v, row0=layer * rows,
                                  prev=updated[name])
        after = updated[what[0][0]][0]
    big_out = [[r.reshape(big[name][0].shape) for r in updated[name]] for name in ("qkv", "oa", "qb", "ob", "kvf", "up", "down")]

    def pick(k):
        b = [o[k] for o in big_out]
        sm = {0: small_g, 1: s_delta, 2: s_m, 3: s_v}[k]
        return [sm[0], b[0], b[1], b[2], b[3], sm[1], b[4], sm[2], b[5], sm[3], sm[4], b[6]]

    return (loss.reshape(()), dh.reshape(1, T, D), *pick(0), *pick(1), *pick(2), *pick(3))
```

```python
import functools
import math

import jax
import jax.numpy as jnp
from jax import lax
from jax.experimental import pallas as pl
from jax.experimental.pallas import tpu as pltpu

F32 = jnp.float32
BF16 = jnp.bfloat16

T = 2048
D = 1024
DEPTH = 4
N_A = 2
HD = 64
A_GROUPS = ((128, 1), (512, 4), (2048, 16))
A_W = 768
B_HEADS = 16
DFF = 2816
NDEV = 8
SLAB = 2 * DFF // NDEV
KVF = 2 * D + B_HEADS
KVF_PAD = 2304
ROPE_DIM = 16
ROPE_THETA = 500000.0
EPS = 1e-6
NEG = -1e30
BLK = 128
LANES = 128
SCALE = HD ** -0.5
VMEM_LIMIT = 56 * 1024 * 1024

ADAM_LR = 0.001
ADAM_B1 = 0.9
ADAM_B2 = 0.999
ADAM_EPS = 1e-08
ADAM_WD = 0.01
ADAM_STEP = 10
HIGHEST = lax.Precision.HIGHEST


def _params(*sem):
    return pltpu.CompilerParams(dimension_semantics=sem or None, vmem_limit_bytes=VMEM_LIMIT)


def _bf(x):
    return x if x.dtype == BF16 else x.astype(BF16)


def matmul(name, a, b, *, ta=False, tb=False, out_dtype=F32, tm=512, tn=512, batch=None):
    a_b = batch in ("a_out", "reduce")
    b_b = batch in ("b_out", "reduce")
    o_b = batch in ("a_out", "b_out")
    nb = a.shape[0] if a_b else (b.shape[0] if b_b else 1)
    ash = a.shape[1:] if a_b else a.shape
    bsh = b.shape[1:] if b_b else b.shape
    m, k = (ash[1], ash[0]) if ta else ash
    k2, n = (bsh[1], bsh[0]) if tb else bsh
    assert k == k2, (name, a.shape, b.shape)
    tm, tn = min(tm, m), min(tn, n)
    assert m % tm == 0 and n % tn == 0, (name, m, n, tm, tn)
    nbr = nb if batch == "reduce" else 1
    grid = (nb if o_b else 1, n // tn, m // tm, nbr)

    def bidx(bo, br):
        return bo if o_b else br

    def spec(batched, block, idx):
        if batched:
            return pl.BlockSpec((None,) + block, lambda bo, j, i, br: (bidx(bo, br),) + idx(i, j))
        return pl.BlockSpec(block, lambda bo, j, i, br: idx(i, j))

    a_spec = spec(a_b, (k, tm) if ta else (tm, k), (lambda i, j: (0, i)) if ta else (lambda i, j: (i, 0)))
    b_spec = spec(b_b, (tn, k) if tb else (k, tn), (lambda i, j: (j, 0)) if tb else (lambda i, j: (0, j)))
    o_spec = spec(o_b, (tm, tn), lambda i, j: (i, j))
    dims = (((0 if ta else 1,), (1 if tb else 0,)), ((), ()))

    def body(a_ref, b_ref, o_ref, *acc):
        p = lax.dot_general(_bf(a_ref[...]), _bf(b_ref[...]), dims, preferred_element_type=F32)
        if nbr == 1:
            o_ref[...] = p.astype(out_dtype)
        else:
            r = pl.program_id(3)

            @pl.when(r == 0)
            def _():
                acc[0][...] = p

            @pl.when(r > 0)
            def _():
                acc[0][...] += p

            @pl.when(r == nbr - 1)
            def _():
                o_ref[...] = acc[0][...].astype(out_dtype)

    out_shape = ((nb,) if o_b else ()) + (m, n)
    return pl.pallas_call(
        body, name=name, grid=grid, in_specs=[a_spec, b_spec], out_specs=o_spec,
        out_shape=jax.ShapeDtypeStruct(out_shape, out_dtype),
        scratch_shapes=[pltpu.VMEM((tm, tn), F32)] if nbr > 1 else [],
        compiler_params=_params("parallel", "parallel", "parallel", "arbitrary"),
    )(a, b)


def _rms(x, g):
    return x * lax.rsqrt(jnp.mean(x * x, axis=-1, keepdims=True) + EPS) * g


def resid_norm(name, h, y, gy, gains, tb=256, dep=None):
    n_g = len(gains)
    has_y = y is not None
    has_dep = dep is not None
    row = pl.BlockSpec((tb, D), lambda i: (i, 0))
    vec = pl.BlockSpec((1, D), lambda i: (0, 0))

    def body(*refs):
        h_ref = refs[0]
        pos = 1
        hn = h_ref[...]
        if has_y:
            hn = hn + _rms(refs[1][...], refs[2][...])
            pos = 3
        g_refs = refs[pos:pos + n_g]
        outs = refs[pos + n_g + has_dep:]
        if has_y:
            outs[0][...] = hn
            outs = outs[1:]
        for g_ref, o_ref in zip(g_refs, outs):
            o_ref[...] = _rms(hn, g_ref[...]).astype(BF16)

    ins = [h] + ([y, gy.reshape(1, D)] if has_y else []) + [g.reshape(1, D) for g in gains] + ([dep] if has_dep else [])
    in_specs = [row] + ([row, vec] if has_y else []) + [vec] * n_g + [pl.BlockSpec(memory_space=pl.ANY)] * has_dep
    out_shape = ([jax.ShapeDtypeStruct((T, D), F32)] if has_y else []) + [jax.ShapeDtypeStruct((T, D), BF16)] * n_g
    res = pl.pallas_call(
        body, name=name, grid=(T // tb,), in_specs=in_specs, out_specs=[row] * len(out_shape),
        out_shape=out_shape, compiler_params=_params("parallel"),
    )(*ins)
    return (res[0], list(res[1:])) if has_y else (h, list(res))


def rms_bwd(name, x, g, dy, add=None, out_dtype=F32, tb=256, dep=None):
    has_add = add is not None
    has_dep = dep is not None
    row = pl.BlockSpec((tb, D), lambda i: (i, 0))
    vec = pl.BlockSpec((1, D), lambda i: (0, 0))

    def body(*refs):
        x_ref, g_ref, dy_ref = refs[:3]
        dx_ref, dg_ref = refs[-2:]
        xv = x_ref[...]
        dyv = dy_ref[...].astype(F32)
        r = lax.rsqrt(jnp.mean(xv * xv, axis=-1, keepdims=True) + EPS)
        gdy = dyv * g_ref[...]
        dx = r * gdy - xv * (r * r * r * jnp.mean(xv * gdy, axis=-1, keepdims=True))
        if has_add:
            dx = dx + refs[3][...]
        dx_ref[...] = dx.astype(out_dtype)
        part = jnp.sum(dyv * xv * r, axis=0, keepdims=True)

        @pl.when(pl.program_id(0) == 0)
        def _():
            dg_ref[...] = part

        @pl.when(pl.program_id(0) > 0)
        def _():
            dg_ref[...] += part

    ins = [x, g.reshape(1, D), dy] + ([add] if has_add else []) + ([dep] if has_dep else [])
    return pl.pallas_call(
        body, name=name, grid=(T // tb,),
        in_specs=[row, vec, row] + ([row] if has_add else []) + [pl.BlockSpec(memory_space=pl.ANY)] * has_dep,
        out_specs=[row, vec],
        out_shape=[jax.ShapeDtypeStruct((T, D), out_dtype), jax.ShapeDtypeStruct((1, D), F32)],
        compiler_params=_params("arbitrary"),
    )(*ins)


def rope_tables():
    pos = jnp.arange(T, dtype=F32)
    inv = ROPE_THETA ** (-jnp.arange(0, ROPE_DIM, 2, dtype=F32) / ROPE_DIM)
    ang = pos[:, None] * inv[None, :]
    cos, sin = jnp.cos(ang), jnp.sin(ang)
    half = ROPE_DIM // 2
    one = jnp.ones((T, HD - ROPE_DIM), F32)
    zero = jnp.zeros((T, HD - ROPE_DIM), F32)
    zh = jnp.zeros((T, half), F32)
    c = jnp.concatenate([cos, cos, one], axis=1)
    s_up = jnp.concatenate([zh, sin, zero], axis=1)
    s_dn = jnp.concatenate([-sin, zh, zero], axis=1)
    rep = LANES // HD
    return tuple(jnp.tile(t, (1, rep)) for t in (c, s_up, s_dn))


def rope(name, t, tables, inverse, out_dtype=BF16, tb=512):
    c, s_up, s_dn = tables
    n_rot = 2 * A_W // LANES
    half = ROPE_DIM // 2
    blk = pl.BlockSpec((tb, LANES), lambda i, j: (i, j))
    tab = pl.BlockSpec((tb, LANES), lambda i, j: (i, 0))

    def body(t_ref, c_ref, su_ref, sd_ref, o_ref):
        x = t_ref[...].astype(F32)
        sgn = -1.0 if inverse else 1.0
        rot = (x * c_ref[...] + pltpu.roll(x, half, 1) * (sgn * su_ref[...])
               + pltpu.roll(x, LANES - half, 1) * (sgn * sd_ref[...]))
        o_ref[...] = jnp.where(pl.program_id(1) < n_rot, rot, x).astype(out_dtype)

    return pl.pallas_call(
        body, name=name, grid=(T // tb, 3 * A_W // LANES), in_specs=[blk, tab, tab, tab], out_specs=blk,
        out_shape=jax.ShapeDtypeStruct((T, 3 * A_W), out_dtype), compiler_params=_params("parallel", "parallel"),
    )(t, c, s_up, s_dn)


GW = 4 * HD


def _band_mask(b):
    qi = lax.broadcasted_iota(jnp.int32, (BLK, 2 * BLK), 0)
    kj = lax.broadcasted_iota(jnp.int32, (BLK, 2 * BLK), 1)
    return (kj <= qi + BLK) & (kj >= qi) & ((kj >= BLK) | (b > 0))


def attn_a_fwd(name, qkvr, g, r):
    length = T // r
    nblk = length // BLK
    view = qkvr.reshape(length, r * 3 * A_W)
    ncol = 3 * A_W // GW

    def col(section, prev):
        def idx(j, b):
            return (jnp.maximum(b - 1, 0) if prev else b, j * ncol + 3 * section + g)
        return pl.BlockSpec((BLK, GW), idx)

    out = pl.BlockSpec((BLK, GW), lambda j, b: (b, j))

    def body(q_ref, kp_ref, kc_ref, vp_ref, vc_ref, o_ref, lse_ref):
        mask = _band_mask(pl.program_id(1))
        k2 = jnp.concatenate([kp_ref[...], kc_ref[...]], axis=0)
        v2 = jnp.concatenate([vp_ref[...], vc_ref[...]], axis=0)
        q = q_ref[...]
        for h in range(4):
            sl = slice(h * HD, (h + 1) * HD)
            s = lax.dot_general(q[:, sl], k2[:, sl], (((1,), (1,)), ((), ())), preferred_element_type=F32) * SCALE
            s = jnp.where(mask, s, NEG)
            m = jnp.max(s, axis=1, keepdims=True)
            p = jnp.exp(s - m)
            l = jnp.sum(p, axis=1, keepdims=True)
            o_ref[:, sl] = jnp.dot((p / l).astype(BF16), v2[:, sl], preferred_element_type=F32)
            lse_ref[:, sl] = jnp.broadcast_to(m + jnp.log(l), (BLK, HD))

    shape = jax.ShapeDtypeStruct((length, r * GW), F32)
    o, lse = pl.pallas_call(
        body, name=name, grid=(r, nblk),
        in_specs=[col(0, False), col(1, True), col(1, False), col(2, True), col(2, False)],
        out_specs=[out, out], out_shape=[shape, shape], compiler_params=_params("parallel", "parallel"),
    )(view, view, view, view, view)
    return o.reshape(T, GW), lse.reshape(T, GW)


def attn_a_bwd(name, qkvr, do, lse, dd, g, r):
    length = T // r
    nblk = length // BLK
    view = qkvr.reshape(length, r * 3 * A_W)
    ncol = 3 * A_W // GW

    def col(section, shift):
        def idx(j, b):
            return (jnp.clip(b + shift, 0, nblk - 1), j * ncol + 3 * section + g)
        return pl.BlockSpec((BLK, GW), idx)

    def tok(shift):
        return pl.BlockSpec((BLK, GW), lambda j, b: (jnp.clip(b + shift, 0, nblk - 1), j))

    def body(q_ref, qn_ref, kp_ref, kc_ref, vp_ref, vc_ref, do_ref, don_ref, lse_ref, lsen_ref, dd_ref, ddn_ref,
             dq_ref, dk_ref, dv_ref):
        b = pl.program_id(1)
        mask = _band_mask(b)
        qi = lax.broadcasted_iota(jnp.int32, (2 * BLK, BLK), 0)
        kj = lax.broadcasted_iota(jnp.int32, (2 * BLK, BLK), 1)
        kmask = ((qi < BLK) & (kj <= qi)) | ((qi >= BLK) & (kj >= qi - BLK) & (b + 1 < nblk))
        k2 = jnp.concatenate([kp_ref[...], kc_ref[...]], axis=0)
        v2 = jnp.concatenate([vp_ref[...], vc_ref[...]], axis=0)
        q2 = jnp.concatenate([q_ref[...], qn_ref[...]], axis=0)
        do2 = jnp.concatenate([do_ref[...], don_ref[...]], axis=0)
        lse2 = jnp.concatenate([lse_ref[...], lsen_ref[...]], axis=0)
        dd2 = jnp.concatenate([dd_ref[...], ddn_ref[...]], axis=0)
        nt = (((1,), (1,)), ((), ()))
        tn = (((0,), (0,)), ((), ()))
        for h in range(4):
            sl = slice(h * HD, (h + 1) * HD)
            one = slice(h * HD, h * HD + 1)
            qh, kh, vh, doh = q2[:, sl], k2[:, sl], v2[:, sl], do2[:, sl]
            s = lax.dot_general(qh[:BLK], kh, nt, preferred_element_type=F32) * SCALE
            p = jnp.where(mask, jnp.exp(s - lse2[:BLK, one]), 0.0)
            dp = lax.dot_general(doh[:BLK], vh, nt, preferred_element_type=F32)
            ds = p * (dp + dd2[:BLK, one])
            dq_ref[:, sl] = jnp.dot(ds.astype(BF16), kh, preferred_element_type=F32) * SCALE
            kc, vc = kh[BLK:], vh[BLK:]
            s = lax.dot_general(qh, kc, nt, preferred_element_type=F32) * SCALE
            p = jnp.where(kmask, jnp.exp(s - lse2[:, one]), 0.0)
            dp = lax.dot_general(doh, vc, nt, preferred_element_type=F32)
            ds = p * (dp + dd2[:, one])
            dk_ref[:, sl] = lax.dot_general(ds.astype(BF16), qh, tn, preferred_element_type=F32) * SCALE
            dv_ref[:, sl] = lax.dot_general(p.astype(BF16), doh, tn, preferred_element_type=F32)

    dov = do.reshape(length, r * GW)
    lsev = lse.reshape(length, r * GW)
    ddv = dd.reshape(length, r * GW)
    shape = jax.ShapeDtypeStruct((length, r * GW), F32)
    dq, dk, dv = pl.pallas_call(
        body, name=name, grid=(r, nblk),
        in_specs=[col(0, 0), col(0, 1), col(1, -1), col(1, 0), col(2, -1), col(2, 0),
                  tok(0), tok(1), tok(0), tok(1), tok(0), tok(1)],
        out_specs=[tok(0)] * 3, out_shape=[shape] * 3, compiler_params=_params("parallel", "parallel"),
    )(view, view, view, view, view, view, dov, dov, lsev, lsev, ddv, ddv)
    return dq.reshape(T, GW), dk.reshape(T, GW), dv.reshape(T, GW)


def _head_sum(x):
    i = lax.div(lax.broadcasted_iota(jnp.int32, (GW, GW), 0), jnp.int32(HD))
    j = lax.div(lax.broadcasted_iota(jnp.int32, (GW, GW), 1), jnp.int32(HD))
    return jnp.dot(x, (i == j).astype(F32), precision=HIGHEST, preferred_element_type=F32)


def _alphas(lses):
    m = jnp.maximum(jnp.maximum(lses[0], lses[1]), lses[2])
    e = [jnp.exp(l - m) for l in lses]
    z = e[0] + e[1] + e[2]
    return [x / z for x in e]


def combine_fwd(name, os_, lses, tb=256):
    blk = pl.BlockSpec((tb, GW), lambda i: (i, 0))

    def body(o0, o1, o2, l0, l1, l2, oc_ref):
        al = _alphas([l0[...], l1[...], l2[...]])
        for g, o_ref in enumerate((o0, o1, o2)):
            oc_ref[:, g * GW:(g + 1) * GW] = (o_ref[...] * al[g]).astype(BF16)

    return pl.pallas_call(
        body, name=name, grid=(T // tb,), in_specs=[blk] * 6, out_specs=pl.BlockSpec((tb, A_W), lambda i: (i, 0)),
        out_shape=jax.ShapeDtypeStruct((T, A_W), BF16), compiler_params=_params("parallel"),
    )(*os_, *lses)


def combine_bwd(name, doc, os_, lses, tb=256):
    blk = pl.BlockSpec((tb, GW), lambda i: (i, 0))

    def body(doc_ref, o0, o1, o2, l0, l1, l2, d0, d1, d2, e0, e1, e2):
        al = _alphas([l0[...], l1[...], l2[...]])
        dal = [_head_sum(doc_ref[:, g * GW:(g + 1) * GW] * o_ref[...]) for g, o_ref in enumerate((o0, o1, o2))]
        mean = al[0] * dal[0] + al[1] * dal[1] + al[2] * dal[2]
        for g, (do_ref, dd_ref) in enumerate(((d0, e0), (d1, e1), (d2, e2))):
            do_ref[...] = (doc_ref[:, g * GW:(g + 1) * GW] * al[g]).astype(BF16)
            dd_ref[...] = al[g] * (dal[g] - mean) - al[g] * dal[g]

    res = pl.pallas_call(
        body, name=name, grid=(T // tb,), in_specs=[pl.BlockSpec((tb, A_W), lambda i: (i, 0))] + [blk] * 6,
        out_specs=[blk] * 6,
        out_shape=[jax.ShapeDtypeStruct((T, GW), BF16)] * 3 + [jax.ShapeDtypeStruct((T, GW), F32)] * 3,
        compiler_params=_params("parallel"),
    )(doc, *os_, *lses)
    return res[:3], res[3:]


TQ = 128
TK = 256
FOX_HEADS = 4
NT = (((1,), (1,)), ((), ()))
TN = (((0,), (0,)), ((), ()))


def _fox_scores(q, kb, cj, row0, col0, masked):
    s = lax.dot_general(q, kb, NT, preferred_element_type=F32) * SCALE - cj
    if masked:
        qi = row0 + lax.broadcasted_iota(jnp.int32, s.shape, 0)
        kj = col0 + lax.broadcasted_iota(jnp.int32, s.shape, 1)
        s = jnp.where(kj <= qi, s, NEG)
    return s


def fox_fwd(name, q, k, v, ct):
    def body(q_ref, k_ref, v_ref, c_ref, o_ref, lse_ref):
        i = pl.program_id(1)
        n_full = lax.div(i, jnp.int32(2))

        def step(jb, carry, masked):
            start = pl.multiple_of(jb * TK, TK)
            out = []
            for h in range(FOX_HEADS):
                m, l, acc = carry[h]
                kb = k_ref[h, pl.ds(start, TK), :]
                vb = v_ref[h, pl.ds(start, TK), :]
                s = _fox_scores(q_ref[h], kb, c_ref[h, pl.ds(jb, 1), :], i * TQ, jb * TK, masked)
                m_new = jnp.maximum(m, jnp.max(s, axis=1, keepdims=True))
                a = jnp.exp(m - m_new)
                p = jnp.exp(s - m_new)
                l = a * l + jnp.sum(p, axis=1, keepdims=True)
                acc = a * acc + jnp.dot(p.astype(BF16), vb, preferred_element_type=F32)
                out.append((m_new, l, acc))
            return tuple(out)

        init = tuple((jnp.full((TQ, 1), NEG, F32), jnp.zeros((TQ, 1), F32), jnp.zeros((TQ, HD), F32))
                     for _ in range(FOX_HEADS))
        carry = lax.fori_loop(0, n_full, lambda jb, c: step(jb, c, False), init)
        carry = step(n_full, carry, True)
        for h in range(FOX_HEADS):
            m, l, acc = carry[h]
            o_ref[h] = (acc / l).astype(BF16)
            lse_ref[h] = jnp.broadcast_to(m + jnp.log(l), (TQ, LANES))

    head = lambda h, i: (h, 0, 0)
    return pl.pallas_call(
        body, name=name, grid=(B_HEADS // FOX_HEADS, T // TQ),
        in_specs=[pl.BlockSpec((FOX_HEADS, TQ, HD), lambda h, i: (h, i, 0)), pl.BlockSpec((FOX_HEADS, T, HD), head),
                  pl.BlockSpec((FOX_HEADS, T, HD), head), pl.BlockSpec((FOX_HEADS, T // TK, TK), head)],
        out_specs=[pl.BlockSpec((FOX_HEADS, TQ, HD), lambda h, i: (h, i, 0)),
                   pl.BlockSpec((FOX_HEADS, TQ, LANES), lambda h, i: (h, i, 0))],
        out_shape=[jax.ShapeDtypeStruct((B_HEADS, T, HD), BF16), jax.ShapeDtypeStruct((B_HEADS, T, LANES), F32)],
        compiler_params=_params("parallel", "parallel"),
    )(q, k, v, ct)


def fox_bwd_q(name, q, k, v, ct, o, do, lse):
    def body(q_ref, k_ref, v_ref, c_ref, o_ref, do_ref, lse_ref, dq_ref, drow_ref):
        i = pl.program_id(1)
        n_full = lax.div(i, jnp.int32(2))
        delta = [jnp.sum(do_ref[h].astype(F32) * o_ref[h].astype(F32), axis=1, keepdims=True)
                 for h in range(FOX_HEADS)]

        def step(jb, carry, masked):
            start = pl.multiple_of(jb * TK, TK)
            out = []
            for h in range(FOX_HEADS):
                dq, drow = carry[h]
                kb = k_ref[h, pl.ds(start, TK), :]
                vb = v_ref[h, pl.ds(start, TK), :]
                s = _fox_scores(q_ref[h], kb, c_ref[h, pl.ds(jb, 1), :], i * TQ, jb * TK, masked)
                p = jnp.exp(s - lse_ref[h, :, 0:1])
                dp = lax.dot_general(do_ref[h], vb, NT, preferred_element_type=F32)
                ds = p * (dp - delta[h])
                out.append((dq + jnp.dot(ds.astype(BF16), kb, preferred_element_type=F32),
                            drow + jnp.sum(ds, axis=1, keepdims=True)))
            return tuple(out)

        init = tuple((jnp.zeros((TQ, HD), F32), jnp.zeros((TQ, 1), F32)) for _ in range(FOX_HEADS))
        carry = lax.fori_loop(0, n_full, lambda jb, c: step(jb, c, False), init)
        carry = step(n_full, carry, True)
        for h in range(FOX_HEADS):
            dq, drow = carry[h]
            dq_ref[h] = (dq * SCALE).astype(BF16)
            drow_ref[h] = jnp.broadcast_to(drow, (TQ, LANES))

    head = lambda h, i: (h, 0, 0)
    blk = pl.BlockSpec((FOX_HEADS, TQ, HD), lambda h, i: (h, i, 0))
    stat = pl.BlockSpec((FOX_HEADS, TQ, LANES), lambda h, i: (h, i, 0))
    return pl.pallas_call(
        body, name=name, grid=(B_HEADS // FOX_HEADS, T // TQ),
        in_specs=[blk, pl.BlockSpec((FOX_HEADS, T, HD), head), pl.BlockSpec((FOX_HEADS, T, HD), head),
                  pl.BlockSpec((FOX_HEADS, T // TK, TK), head), blk, blk, stat],
        out_specs=[blk, stat],
        out_shape=[jax.ShapeDtypeStruct((B_HEADS, T, HD), BF16), jax.ShapeDtypeStruct((B_HEADS, T, LANES), F32)],
        compiler_params=_params("parallel", "parallel"),
    )(q, k, v, ct, o, do, lse)


def fox_bwd_kv(name, q, k, v, ct, o, do, lse, prev):
    has_prev = prev is not None
    nq = T // TQ

    def body(*refs):
        q_ref, k_ref, v_ref, c_ref, o_ref, do_ref, lse_ref = refs[:7]
        dk_ref, dv_ref, dc_ref = refs[-3:]
        jb = pl.program_id(1)
        dk_ref[...] = jnp.zeros_like(dk_ref)
        dv_ref[...] = jnp.zeros_like(dv_ref)

        def step(i, dcs, masked):
            start = pl.multiple_of(i * TQ, TQ)
            out = []
            for h in range(FOX_HEADS):
                qv = q_ref[h, pl.ds(start, TQ), :]
                dov = do_ref[h, pl.ds(start, TQ), :]
                ov = o_ref[h, pl.ds(start, TQ), :]
                lsev = lse_ref[h, pl.ds(start, TQ), 0:1]
                delta = jnp.sum(dov.astype(F32) * ov.astype(F32), axis=1, keepdims=True)
                s = _fox_scores(qv, k_ref[h], c_ref[h, pl.ds(jb, 1), :], i * TQ, jb * TK, masked)
                p = jnp.exp(s - lsev)
                dp = lax.dot_general(dov, v_ref[h], NT, preferred_element_type=F32)
                ds = p * (dp - delta)
                dv_ref[h] += lax.dot_general(p.astype(BF16), dov, TN, preferred_element_type=F32)
                dk_ref[h] += lax.dot_general(ds.astype(BF16), qv, TN, preferred_element_type=F32)
                out.append(dcs[h] - jnp.sum(ds, axis=0, keepdims=True))
            return tuple(out)

        first = 2 * jb
        dcs = tuple(jnp.zeros((1, TK), F32) for _ in range(FOX_HEADS))
        dcs = step(first, dcs, True)
        dcs = step(first + 1, dcs, True)
        dcs = lax.fori_loop(first + 2, nq, lambda i, c: step(i, c, False), dcs)
        for h in range(FOX_HEADS):
            dk = dk_ref[h] * SCALE
            dc = dcs[h]
            if has_prev:
                dk = dk + refs[7][h]
                dv_ref[h] += refs[8][h]
                dc = dc + refs[9][h, pl.ds(jb, 1), :]
            dk_ref[h] = dk
            dc_ref[h, pl.ds(jb, 1), :] = dc

    head = lambda h, j: (h, 0, 0)
    full = pl.BlockSpec((FOX_HEADS, T, HD), head)
    blk = pl.BlockSpec((FOX_HEADS, TK, HD), lambda h, j: (h, j, 0))
    cspec = pl.BlockSpec((FOX_HEADS, T // TK, TK), head)
    ins = [q, k, v, ct, o, do, lse] + (list(prev) if has_prev else [])
    in_specs = [full, blk, blk, cspec, full, full, pl.BlockSpec((FOX_HEADS, T, LANES), head)] + ([blk, blk, cspec] if has_prev else [])
    return pl.pallas_call(
        body, name=name, grid=(B_HEADS // FOX_HEADS, T // TK), in_specs=in_specs, out_specs=[blk, blk, cspec],
        out_shape=[jax.ShapeDtypeStruct((B_HEADS, T, HD), F32)] * 2 + [jax.ShapeDtypeStruct((B_HEADS, T // TK, TK), F32)],
        compiler_params=_params("parallel", "arbitrary"),
    )(*ins)


def _tri(upper):
    i = lax.broadcasted_iota(jnp.int32, (BLK, BLK), 0)
    j = lax.broadcasted_iota(jnp.int32, (BLK, BLK), 1)
    return ((i <= j) if upper else (i >= j)).astype(F32)


def gates_fwd(name, fgt, b_f):
    def body(f_ref, b_ref, c_ref):
        tri = _tri(True)
        carry = jnp.zeros((B_HEADS, 1), F32)
        for blk in range(T // BLK):
            sl = slice(blk * BLK, (blk + 1) * BLK)
            z = f_ref[:, sl] + b_ref[...]
            logf = jnp.minimum(z, 0.0) - jnp.log(1.0 + jnp.exp(-jnp.abs(z)))
            cs = jnp.dot(logf, tri, precision=HIGHEST, preferred_element_type=F32) + carry
            c_ref[:, sl] = cs
            carry = cs[:, BLK - 1:BLK]

    return pl.pallas_call(
        body, name=name, out_shape=jax.ShapeDtypeStruct((B_HEADS, T), F32), compiler_params=_params(),
    )(fgt, b_f)


def gates_bwd(name, fgt, b_f, dcs):
    n_dc = len(dcs)

    def body(*refs):
        f_ref, b_ref = refs[:2]
        dc_refs = refs[2:2 + n_dc]
        dz_ref, db_ref = refs[2 + n_dc:]
        tri = _tri(False)
        carry = jnp.zeros((B_HEADS, 1), F32)
        db = jnp.zeros((B_HEADS, 1), F32)
        for blk in reversed(range(T // BLK)):
            sl = slice(blk * BLK, (blk + 1) * BLK)
            dc = dc_refs[0][:, sl]
            for r in dc_refs[1:]:
                dc = dc + r[:, sl]
            rc = jnp.dot(dc, tri, precision=HIGHEST, preferred_element_type=F32) + carry
            carry = rc[:, 0:1]
            z = f_ref[:, sl] + b_ref[...]
            e = jnp.exp(-jnp.abs(z))
            dz = rc * jnp.where(z >= 0.0, e, 1.0) / (1.0 + e)
            dz_ref[:, sl] = dz
            db = db + jnp.sum(dz, axis=1, keepdims=True)
        db_ref[...] = db

    return pl.pallas_call(
        body, name=name,
        out_shape=[jax.ShapeDtypeStruct((B_HEADS, T), F32), jax.ShapeDtypeStruct((B_HEADS, 1), F32)],
        compiler_params=_params(),
    )(fgt, b_f, *dcs)


CONV_TB = 256
GELU_K = math.sqrt(2.0 / math.pi)
GELU_C = 0.044715


def _shift_down(x, halo_ref, n):
    rows = lax.broadcasted_iota(jnp.int32, x.shape, 0)
    y = pltpu.roll(x, n, 0)
    for k in range(n):
        y = jnp.where(rows == k, halo_ref[pl.ds(8 - n + k, 1), :], y)
    return y


def _conv(x, halo_ref, cw_ref, cb_ref, first):
    x1 = _shift_down(x, halo_ref, 1)
    x2 = _shift_down(x, halo_ref, 2)
    rows = lax.broadcasted_iota(jnp.int32, x.shape, 0)
    x1 = jnp.where(first & (rows < 1), 0.0, x1)
    x2 = jnp.where(first & (rows < 2), 0.0, x2)
    y = x2 * cw_ref[0:1, :] + x1 * cw_ref[1:2, :] + x * cw_ref[2:3, :] + cb_ref[...]
    return y, x1, x2


def _gelu_parts(x):
    th = jnp.tanh(GELU_K * (x + GELU_C * x * x * x))
    val = 0.5 * x * (1.0 + th)
    grad = 0.5 * (1.0 + th) + 0.5 * x * (1.0 - th * th) * GELU_K * (1.0 + 3.0 * GELU_C * x * x)
    return val, grad


def _conv_specs(tb):
    def slab(off):
        return pl.BlockSpec((None, tb, SLAB), lambda d, i: (d + off, i, 0))

    def halo(off):
        return pl.BlockSpec((None, 8, SLAB), lambda d, i: (d + off, jnp.maximum(i * (tb // 8) - 1, 0), 0))

    def par(rows, off):
        return pl.BlockSpec((None, rows, SLAB), lambda d, i: (d + off, 0, 0))

    return slab, halo, par


def convglu_fwd(name, a, cw, cb, tb=CONV_TB):
    slab, halo, par = _conv_specs(tb)

    def body(ag, hg, av, hv, cwg, cbg, cwv, cbv, u_ref):
        first = pl.program_id(1) == 0
        gate, _, _ = _conv(ag[...], hg, cwg, cbg, first)
        val, _, _ = _conv(av[...], hv, cwv, cbv, first)
        u_ref[...] = (_gelu_parts(gate)[0] * val).astype(BF16)

    return pl.pallas_call(
        body, name=name, grid=(4, T // tb),
        in_specs=[slab(0), halo(0), slab(4), halo(4), par(3, 0), par(1, 0), par(3, 4), par(1, 4)],
        out_specs=slab(0), out_shape=jax.ShapeDtypeStruct((4, T, SLAB), BF16),
        compiler_params=_params("parallel", "parallel"),
    )(a, a, a, a, cw, cb, cw, cb)


def convglu_bwd(name, du, a, cw, cb, tb=CONV_TB):
    slab, halo, par = _conv_specs(tb)

    def body(du_ref, ag, hg, av, hv, cwg, cbg, cwv, cbv, dg_ref, dv_ref, dcwg, dcbg, dcwv, dcbv):
        first = pl.program_id(1) == 0
        gate, g1, g2 = _conv(ag[...], hg, cwg, cbg, first)
        val, v1, v2 = _conv(av[...], hv, cwv, cbv, first)
        act, dact = _gelu_parts(gate)
        duv = du_ref[...].astype(F32)
        dgate = duv * val * dact
        dval = duv * act
        dg_ref[...] = dgate
        dv_ref[...] = dval
        for dy, xs, dcw_ref, dcb_ref in ((dgate, (g2, g1, ag[...]), dcwg, dcbg), (dval, (v2, v1, av[...]), dcwv, dcbv)):
            parts = [jnp.sum(dy * x, axis=0, keepdims=True) for x in xs]
            bias = jnp.sum(dy, axis=0, keepdims=True)

            @pl.when(first)
            def _():
                for k in range(3):
                    dcw_ref[k:k + 1, :] = parts[k]
                dcb_ref[...] = bias

            @pl.when(jnp.logical_not(first))
            def _():
                for k in range(3):
                    dcw_ref[k:k + 1, :] += parts[k]
                dcb_ref[...] += bias

    res = pl.pallas_call(
        body, name=name, grid=(4, T // tb),
        in_specs=[slab(0), slab(0), halo(0), slab(4), halo(4), par(3, 0), par(1, 0), par(3, 4), par(1, 4)],
        out_specs=[slab(0), slab(0), par(3, 0), par(1, 0), par(3, 0), par(1, 0)],
        out_shape=[jax.ShapeDtypeStruct((4, T, SLAB), F32)] * 2
        + [jax.ShapeDtypeStruct((4, 3, SLAB), F32), jax.ShapeDtypeStruct((4, 1, SLAB), F32)] * 2,
        compiler_params=_params("parallel", "arbitrary"),
    )(du, a, a, a, a, cw, cb, cw, cb)
    dgate, dval, dcwg, dcbg, dcwv, dcbv = res
    return (dgate, dval), jnp.concatenate([dcwg, dcwv], axis=0), jnp.concatenate([dcbg, dcbv], axis=0)


def conv_bwd_input(name, dac_pair, cw, tb=CONV_TB):
    nblk = T // tb

    def run(x, off):
        def body(x_ref, nx_ref, cw_ref, da_ref):
            last = pl.program_id(1) == nblk - 1
            xv = x_ref[...]
            rows = lax.broadcasted_iota(jnp.int32, xv.shape, 0)

            def up(n):
                y = pltpu.roll(xv, tb - n, 0)
                for k in range(n):
                    y = jnp.where(rows == tb - n + k, nx_ref[pl.ds(k, 1), :], y)
                return jnp.where(last & (rows >= tb - n), 0.0, y)

            da_ref[...] = (xv * cw_ref[2:3, :] + up(1) * cw_ref[1:2, :] + up(2) * cw_ref[0:1, :]).astype(BF16)

        return pl.pallas_call(
            body, name=f"{name}_{off}", grid=(4, nblk),
            in_specs=[pl.BlockSpec((None, tb, SLAB), lambda d, i: (d, i, 0)),
                      pl.BlockSpec((None, 8, SLAB), lambda d, i: (d, jnp.minimum((i + 1) * (tb // 8), T // 8 - 1), 0)),
                      pl.BlockSpec((None, 3, SLAB), lambda d, i: (d + off, 0, 0))],
            out_specs=pl.BlockSpec((None, tb, SLAB), lambda d, i: (d, i, 0)),
            out_shape=jax.ShapeDtypeStruct((4, T, SLAB), BF16), compiler_params=_params("parallel", "parallel"),
        )(x, x, cw)

    return jnp.concatenate([run(dac_pair[0], 0), run(dac_pair[1], 4)], axis=0)


def loss_head(name, y, target, tb=256):
    row = pl.BlockSpec((tb, D), lambda i: (i, 0))

    def body(y_ref, t_ref, dy_ref, loss_ref):
        diff = y_ref[...] - t_ref[...]
        dy_ref[...] = diff * (1.0 / D)
        part = jnp.sum(jnp.sum(diff * diff, axis=1, keepdims=True), axis=0, keepdims=True) * (0.5 / D)

        @pl.when(pl.program_id(0) == 0)
        def _():
            loss_ref[...] = part

        @pl.when(pl.program_id(0) > 0)
        def _():
            loss_ref[...] += part

    return pl.pallas_call(
        body, name=name, grid=(T // tb,), in_specs=[row, row],
        out_specs=[row, pl.BlockSpec((1, 1), lambda i: (0, 0))],
        out_shape=[jax.ShapeDtypeStruct((T, D), F32), jax.ShapeDtypeStruct((1, 1), F32)],
        compiler_params=_params("arbitrary"),
    )(y, target)


def _row_tile(rows, cols, bytes_per_elem, budget=6 * 1024 * 1024):
    for tr in (1024, 512, 256, 128, 64, 32, 16, 8):
        if rows % tr == 0 and tr * cols * bytes_per_elem <= budget:
            return tr
    return rows


def adamw(name, parts, w, m, v, row0=0, prev=None):
    n_parts, rows, cols = parts.shape
    rows_all = w.shape[0]
    tr = _row_tile(math.gcd(rows, row0) if row0 else rows, cols, n_parts * parts.dtype.itemsize + 28)
    blk = pl.BlockSpec((tr, cols), lambda i: (row0 // tr + i, 0))
    b1c = 1.0 - ADAM_B1 ** ADAM_STEP
    b2c = 1.0 - ADAM_B2 ** ADAM_STEP
    n_prev = 0 if prev is None else 4

    def body(p_ref, w_ref, m_ref, v_ref, *rest):
        g_ref, d_ref, nm_ref, nv_ref = rest[n_prev:]
        g = p_ref[0].astype(F32)
        for k in range(1, n_parts):
            g = g + p_ref[k].astype(F32)
        nm = ADAM_B1 * m_ref[...] + (1.0 - ADAM_B1) * g
        nv = ADAM_B2 * v_ref[...] + (1.0 - ADAM_B2) * (g * g)
        g_ref[...] = g
        nm_ref[...] = nm
        nv_ref[...] = nv
        d_ref[...] = -ADAM_LR * ((nm / b1c) / (jnp.sqrt(nv / b2c) + ADAM_EPS) + ADAM_WD * w_ref[...])

    return pl.pallas_call(
        body, name=name, grid=(rows // tr,),
        in_specs=[pl.BlockSpec((n_parts, tr, cols), lambda i: (0, i, 0)), blk, blk, blk]
        + [pl.BlockSpec(memory_space=pl.ANY)] * n_prev,
        out_specs=[blk] * 4, out_shape=[jax.ShapeDtypeStruct((rows_all, cols), F32)] * 4,
        input_output_aliases={4 + k: k for k in range(n_prev)}, compiler_params=_params("parallel"),
    )(parts, w, m, v, *(prev or []))


def exchange(name, items, scatter):
    n = len(items)
    hbm = pl.BlockSpec(memory_space=pltpu.HBM)

    def body(*refs):
        ins, outs = refs[:n], refs[n:2 * n]
        send_sems, recv_sems, local_sems = refs[2 * n:]
        x, y, c = lax.axis_index("x"), lax.axis_index("y"), lax.axis_index("c")
        me = 4 * x + 2 * y + c
        copies = []
        for t in range(n):
            own = pltpu.make_async_copy(ins[t].at[me] if scatter else ins[t], outs[t].at[me], local_sems.at[t])
            own.start()
            copies.append(own)
            for rel in range(1, NDEV):
                px = 1 - x if rel & 4 else x
                py = 1 - y if rel & 2 else y
                pc = 1 - c if rel & 1 else c
                src = ins[t].at[4 * px + 2 * py + pc] if scatter else ins[t]
                cp = pltpu.make_async_remote_copy(
                    src_ref=src, dst_ref=outs[t].at[me], send_sem=send_sems.at[t, rel - 1],
                    recv_sem=recv_sems.at[t, rel - 1], device_id=(px, py, pc), device_id_type=pl.DeviceIdType.MESH)
                cp.start()
                copies.append(cp)
        for cp in copies:
            cp.wait()

    out_shape = [jax.ShapeDtypeStruct(it.shape if scatter else (NDEV,) + it.shape, it.dtype) for it in items]
    return pl.pallas_call(
        body, name=name, in_specs=[hbm] * n, out_specs=[hbm] * n, out_shape=out_shape,
        scratch_shapes=[pltpu.SemaphoreType.DMA((n, NDEV - 1)), pltpu.SemaphoreType.DMA((n, NDEV - 1)),
                        pltpu.SemaphoreType.DMA((n,))],
    )(*items)


def _peer(rel, x, y, c):
    return (1 - x if rel & 4 else x, 1 - y if rel & 2 else y, 1 - c if rel & 1 else c)


def _split_copies(ins, lands, send_sems, recv_sems, scatter):
    x, y, c = lax.axis_index("x"), lax.axis_index("y"), lax.axis_index("c")
    me = 4 * x + 2 * y + c
    copies = []
    for t in range(len(ins)):
        for rel in range(1, NDEV):
            px, py, pc = _peer(rel, x, y, c)
            src = ins[t].at[4 * px + 2 * py + pc] if scatter else ins[t]
            copies.append(pltpu.make_async_remote_copy(
                src_ref=src, dst_ref=lands[t].at[me], send_sem=send_sems.at[t * (NDEV - 1) + rel - 1],
                recv_sem=recv_sems.at[t * (NDEV - 1) + rel - 1], device_id=(px, py, pc),
                device_id_type=pl.DeviceIdType.MESH))
    return me, copies


def exchange_start(name, items, scatter, dep=None):
    n = len(items)
    hbm = pl.BlockSpec(memory_space=pltpu.HBM)
    sem = pl.BlockSpec(memory_space=pltpu.SEMAPHORE)
    has_dep = dep is not None
    land_shapes = [it.shape if scatter else (NDEV,) + it.shape for it in items]

    def body(*refs):
        ins, lands = refs[:n], refs[n:2 * n]
        outs = refs[2 * n + has_dep:]
        send_sems, recv_sems, token = outs[0], outs[1], outs[2 + 2 * n]
        _, copies = _split_copies(ins, lands, send_sems, recv_sems, scatter)
        for cp in copies:
            cp.start()
        token[...] = jnp.zeros_like(token)

    sems = pltpu.SemaphoreType.DMA((n * (NDEV - 1),))
    out_shape = ([sems, sems] + [pltpu.HBM(it.shape, it.dtype) for it in items]
                 + [pltpu.HBM(sh, it.dtype) for sh, it in zip(land_shapes, items)] + [jax.ShapeDtypeStruct((8, LANES), F32)])
    operands = ([pltpu.with_memory_space_constraint(it, pltpu.HBM) for it in items]
                + [pltpu.with_memory_space_constraint(lax.empty(sh, it.dtype), pltpu.HBM) for sh, it in zip(land_shapes, items)]
                + ([dep] if has_dep else []))
    res = pl.pallas_call(
        body, name=name, in_specs=[hbm] * (2 * n) + [pl.BlockSpec(memory_space=pl.ANY)] * has_dep,
        out_specs=[sem, sem] + [hbm] * (2 * n) + [pl.BlockSpec(memory_space=pltpu.VMEM)], out_shape=out_shape,
        input_output_aliases={t: 2 + t for t in range(2 * n)},
        compiler_params=pltpu.CompilerParams(has_side_effects=pltpu.SideEffectType.DATAFLOW_SIDE_EFFECTING),
    )(*operands)
    return (res[0], res[1], list(res[2:2 + n]), list(res[2 + n:2 + 2 * n]), scatter), res[2 + 2 * n]


def exchange_wait(name, handle, after):
    send_sems, recv_sems, ins, lands, scatter = handle
    n = len(ins)
    after = list(after) if isinstance(after, (list, tuple)) else [after]
    hbm = pl.BlockSpec(memory_space=pltpu.HBM)
    sem = pl.BlockSpec(memory_space=pltpu.SEMAPHORE)

    def body(*refs):
        _, copies = _split_copies(refs[:n], refs[n:2 * n], refs[2 * n], refs[2 * n + 1], scatter)
        for cp in copies:
            cp.wait_send()
            cp.wait_recv()

    res = pl.pallas_call(
        body, name=name, in_specs=[hbm] * (2 * n) + [sem, sem] + [pl.BlockSpec(memory_space=pl.ANY)] * len(after),
        out_specs=[hbm] * (2 * n), out_shape=[pltpu.HBM(a.shape, a.dtype) for a in ins + lands],
        input_output_aliases={t: t for t in range(2 * n)},
        compiler_params=pltpu.CompilerParams(has_side_effects=pltpu.SideEffectType.DATAFLOW_SIDE_EFFECTING),
    )(*ins, *lands, send_sems, recv_sems, *after)
    me = 4 * lax.axis_index("x") + 2 * lax.axis_index("y") + lax.axis_index("c")
    out = []
    for src, landed in zip(res[:n], res[n:]):
        own = lax.dynamic_index_in_dim(src, me, axis=0, keepdims=True) if scatter else src[None]
        out.append(lax.dynamic_update_slice_in_dim(landed, own, me, axis=0))
    return out


def sum_slots(name, parts):
    _, rows, cols = parts.shape

    def body(p_ref, o_ref):
        s = p_ref[0]
        for k in range(1, NDEV):
            s = s + p_ref[k]
        o_ref[...] = s

    return pl.pallas_call(body, name=name, out_shape=jax.ShapeDtypeStruct((rows, cols), F32), compiler_params=_params())(parts)


def _heads(t):
    return t.reshape(T, B_HEADS, HD).transpose(1, 0, 2)


def _unheads(t):
    return t.transpose(1, 0, 2).reshape(T, B_HEADS * HD)


def _cols_from_slots(g):
    return g.transpose(1, 0, 2).reshape(g.shape[1], NDEV * g.shape[2])


def _slots_from_cols(w):
    return w.reshape(w.shape[0], NDEV, w.shape[1] // NDEV).transpose(1, 0, 2)


def _pack(arrays, rows):
    flat = jnp.concatenate([a.reshape(-1).astype(F32) for a in arrays])
    return jnp.pad(flat, (0, rows * LANES - flat.shape[0])).reshape(rows, LANES)


def _unpack(buf, shapes):
    flat = buf.reshape(-1)
    out, pos = [], 0
    for sh in shapes:
        size = math.prod(sh)
        out.append(flat[pos:pos + size].reshape(sh))
        pos += size
    return out


def kernel(x, norm_gains, w_qkv_a, w_o_a, w_q_b, w_o_b, kv_norm, w_kvf, b_f, w_up, conv_w, conv_b, w_down, loss_target, m_norm_gains, m_w_qkv_a, m_w_o_a, m_w_q_b, m_w_o_b, m_kv_norm, m_w_kvf, m_b_f, m_w_up, m_conv_w, m_conv_b, m_w_down, v_norm_gains, v_w_qkv_a, v_w_o_a, v_w_q_b, v_w_o_b, v_kv_norm, v_w_kvf, v_b_f, v_w_up, v_conv_w, v_conv_b, v_w_down):
    me = 4 * lax.axis_index("x") + 2 * lax.axis_index("y") + lax.axis_index("c")
    n_b = DEPTH - N_A

    def bf(a):
        return a.astype(BF16)

    mixer_w = [[bf(w_qkv_a[l]), bf(w_o_a[l])] if l < N_A else [bf(w_q_b[l - N_A]), bf(w_o_b[l - N_A])] for l in range(DEPTH)]
    ffn_w = [[bf(w_up[l]), bf(w_down[l])] for l in range(DEPTH)]
    mixer_w[0] += [norm_gains, conv_w]
    mixer_w[N_A] += [bf(w_kvf)]
    handles, tok = [], None
    for l in range(DEPTH):
        for part, items in (("mixer", mixer_w[l]), ("ffn", ffn_w[l])):
            hd, tok = exchange_start(f"gather_start_{part}{l}", items, False, dep=tok)
            handles.append(hd)

    wqkv, woa, wqb, wob, wup, wdown = {}, {}, {}, {}, {}, {}

    def take_mixer(l, arrived):
        if l < N_A:
            wqkv[l] = _cols_from_slots(arrived[0])
            woa[l] = _cols_from_slots(arrived[1])
        else:
            wqb[l - N_A] = arrived[0].reshape(D, D)
            wob[l - N_A] = arrived[1].reshape(D, D)

    def take_ffn(l, arrived):
        wup[l] = arrived[0]
        wdown[l] = arrived[1].reshape(4, SLAB, D)

    arrived = exchange_wait("gather_wait_mixer0", handles[0], tok)
    take_mixer(0, arrived)
    gains = arrived[2].transpose(1, 2, 0, 3).reshape(DEPTH, 4, D)
    cws = [arrived[3][:, l] for l in range(DEPTH)]
    cbs = [conv_b[l].reshape(NDEV, 1, SLAB) for l in range(DEPTH)]
    tables = rope_tables()
    b_col = b_f.reshape(B_HEADS, 1)

    h = x.reshape(T, D)
    _, (xn,) = resid_norm("norm_in", h, None, None, [gains[0, 0]])
    saved = []
    shared = None
    for l in range(DEPTH):
        s = {"h": h, "xn": xn}
        if l >= 1:
            arrived = exchange_wait(f"gather_wait_mixer{l}", handles[2 * l], h)
            take_mixer(l, arrived[:2])
            if l == N_A:
                wkvf = jnp.pad(_cols_from_slots(arrived[2]), ((0, 0), (0, KVF_PAD - KVF)))
        if l < N_A:
            qkv = matmul(f"qkv{l}", xn, wqkv[l], tm=1024, tn=768)
            s["qkvr"] = rope(f"rope{l}", qkv, tables, inverse=False)
            os_, lses = [], []
            for g, (_, r) in enumerate(A_GROUPS):
                o, lse = attn_a_fwd(f"attn_a{l}_{g}", s["qkvr"], g, r)
                os_.append(o)
                lses.append(lse)
            s["o"], s["lse"] = os_, lses
            s["oc"] = combine_fwd(f"combine{l}", os_, lses)
            mix = matmul(f"wo_a{l}", s["oc"], woa[l], tm=1024, tn=512)
        else:
            if l == N_A:
                kvf = matmul("kvf", xkv, wkvf, tm=1024, tn=768)
                k_h = _heads(kvf[:, :D].astype(BF16))
                v_h = _heads(kvf[:, D:2 * D].astype(BF16))
                fgt = kvf[:, 2 * D:KVF].T
                ct = gates_fwd("gates", fgt, b_col).reshape(B_HEADS, T // TK, TK)
                shared = {"xkv": xkv, "k": k_h, "v": v_h, "fgt": fgt, "ct": ct, "h": h}
            j = l - N_A
            s["q"] = _heads(matmul(f"wq_b{j}", xn, wqb[j], out_dtype=BF16, tm=1024, tn=512))
            s["o"], s["lse"] = fox_fwd(f"fox{j}", s["q"], shared["k"], shared["v"], shared["ct"])
            s["oc"] = _unheads(s["o"])
            mix = matmul(f"wo_b{j}", s["oc"], wob[j], tm=1024, tn=512)
        s["mix"] = mix
        take_ffn(l, exchange_wait(f"gather_wait_ffn{l}", handles[2 * l + 1], mix))
        s["h1"], (s["xn2"],) = resid_norm(f"norm_mid{l}", h, mix, gains[l, 1], [gains[l, 2]])
        s["a"] = matmul(f"up{l}", s["xn2"], wup[l], tm=1024, tn=SLAB, batch="b_out")
        s["u"] = convglu_fwd(f"convglu{l}", s["a"], cws[l], cbs[l])
        s["f"] = matmul(f"down{l}", s["u"], wdown[l], tm=1024, tn=512, batch="reduce")
        nxt = [gains[l + 1, 0]] if l + 1 < DEPTH else []
        if l == N_A - 1:
            nxt.append(kv_norm)
        h, normed = resid_norm(f"norm_out{l}", s["h1"], s["f"], gains[l, 3], nxt)
        if l + 1 < DEPTH:
            xn = normed[0]
        if l == N_A - 1:
            xkv = normed[1]
        saved.append(s)

    dh, loss_part = loss_head("loss", h, loss_target.reshape(T, D))

    d_gains = [[None] * 4 for _ in range(DEPTH)]
    d_cw, d_cb = [None] * DEPTH, [None] * DEPTH
    gw = {"qkv": [None] * N_A, "oa": [None] * N_A, "qb": [None] * n_b, "ob": [None] * n_b, "up": [None] * DEPTH,
          "down": [None] * DEPTH}
    kv_acc = None
    d_rows = []
    sent = []
    tok = None

    def slots_rows(g):
        return g.reshape(NDEV, g.shape[0] // NDEV, g.shape[1])

    for l in reversed(range(DEPTH)):
        s = saved[l]
        df, d_gains[l][3] = rms_bwd(f"bwd_norm_out{l}", s["f"], gains[l, 3], dh, out_dtype=BF16, dep=tok)
        du = matmul(f"bwd_down_x{l}", df, wdown[l], tb=True, out_dtype=BF16, tm=1024, tn=SLAB, batch="b_out")
        gw["down"][l] = matmul(f"bwd_down_w{l}", s["u"], df, ta=True, out_dtype=BF16, tm=SLAB, tn=512, batch="a_out")
        dac, d_cw[l], d_cb[l] = convglu_bwd(f"bwd_convglu{l}", du, s["a"], cws[l], cbs[l])
        da = conv_bwd_input(f"bwd_conv{l}", dac, cws[l])
        dxn2 = matmul(f"bwd_up_x{l}", da, wup[l], tb=True, tm=1024, tn=512, batch="reduce")
        gw["up"][l] = matmul(f"bwd_up_w{l}", s["xn2"], da, ta=True, out_dtype=BF16, tm=512, tn=SLAB, batch="b_out")
        dh1, d_gains[l][2] = rms_bwd(f"bwd_norm_mid{l}", s["h1"], gains[l, 2], dxn2, add=dh)
        ffn_items = [gw["up"][l], slots_rows(gw["down"][l].reshape(DFF, D))]
        ffn_what = [("up", l), ("down", l)]
        tok = None
        if l == 0:
            hd, tok = exchange_start("scatter_start_ffn0", ffn_items, True)
            sent.append((hd, ffn_what))
            ffn_items, ffn_what = [], []
        dmix, d_gains[l][1] = rms_bwd(f"bwd_norm_mix{l}", s["mix"], gains[l, 1], dh1, out_dtype=BF16, dep=tok)
        if l < N_A:
            doc = matmul(f"bwd_wo_a_x{l}", dmix, woa[l], tb=True, tm=1024, tn=A_W)
            gw["oa"][l] = matmul(f"bwd_wo_a_w{l}", s["oc"], dmix, ta=True, out_dtype=BF16, tm=A_W, tn=512)
            dos, dds = combine_bwd(f"bwd_combine{l}", doc, s["o"], s["lse"])
            cols = [None] * 9
            for g, (_, r) in enumerate(A_GROUPS):
                dq, dk, dv = attn_a_bwd(f"bwd_attn_a{l}_{g}", s["qkvr"], dos[g], s["lse"][g], dds[g], g, r)
                cols[g], cols[3 + g], cols[6 + g] = dq, dk, dv
            dqkv = rope(f"bwd_rope{l}", jnp.concatenate(cols, axis=1), tables, inverse=True)
            dxn = matmul(f"bwd_qkv_x{l}", dqkv, wqkv[l], tb=True, tm=1024, tn=512)
            gw["qkv"][l] = matmul(f"bwd_qkv_w{l}", s["xn"], dqkv, ta=True, out_dtype=BF16, tm=512, tn=768)
        else:
            j = l - N_A
            do = _heads(matmul(f"bwd_wo_b_x{j}", dmix, wob[j], tb=True, out_dtype=BF16, tm=1024, tn=512))
            gw["ob"][j] = matmul(f"bwd_wo_b_w{j}", s["oc"], dmix, ta=True, out_dtype=BF16, tm=512, tn=512)
            args = (s["q"], shared["k"], shared["v"], shared["ct"], s["o"], do, s["lse"])
            dq_h, drow = fox_bwd_q(f"bwd_fox_q{j}", *args)
            dq = _unheads(dq_h)
            d_rows.append(drow[:, :, 0])
            kv_acc = fox_bwd_kv(f"bwd_fox_kv{j}", *args, kv_acc)
            dxn = matmul(f"bwd_wq_b_x{j}", dq, wqb[j], tb=True, tm=1024, tn=512)
            gw["qb"][j] = matmul(f"bwd_wq_b_w{j}", s["xn"], dq, ta=True, out_dtype=BF16, tm=512, tn=512)
        dh, d_gains[l][0] = rms_bwd(f"bwd_norm_in{l}", s["h"], gains[l, 0], dxn, add=dh1)
        if l == N_A:
            dk_h, dv_h, dct = kv_acc
            dfgt, d_bf = gates_bwd("bwd_gates", shared["fgt"], b_col, [dct.reshape(B_HEADS, T)] + d_rows)
            dkvf = jnp.concatenate(
                [_unheads(dk_h).astype(BF16), _unheads(dv_h).astype(BF16), dfgt.T.astype(BF16),
                 jnp.zeros((T, KVF_PAD - KVF), BF16)], axis=1)
            dxkv = matmul("bwd_kvf_x", dkvf, wkvf, tb=True, tm=1024, tn=512)
            g_kvf_full = matmul("bwd_kvf_w", shared["xkv"], dkvf, ta=True, out_dtype=BF16, tm=512, tn=768)
            dh, d_kvn = rms_bwd("bwd_norm_kv", shared["h"], kv_norm, dxkv, add=dh)
        if l < N_A:
            items = [_slots_from_cols(gw["qkv"][l]), _slots_from_cols(gw["oa"][l])]
            what = [("qkv", l), ("oa", l)]
        else:
            items = [slots_rows(gw["qb"][l - N_A]), slots_rows(gw["ob"][l - N_A])]
            what = [("qb", l - N_A), ("ob", l - N_A)]
        if l == N_A:
            items.append(_slots_from_cols(g_kvf_full[:, :KVF]))
            what.append(("kvf", 0))
        hd, tok = exchange_start(f"scatter_start{l}", items + ffn_items, True)
        sent.append((hd, what + ffn_what))

    small_shapes = [(DEPTH, 4, D), (D,), (B_HEADS,), (DEPTH, 3, NDEV * SLAB), (DEPTH, NDEV * SLAB), (1,)]
    small = [
        jnp.stack([jnp.concatenate(row, axis=0) for row in d_gains]),
        d_kvn, d_bf,
        jnp.stack([d.transpose(1, 0, 2).reshape(3, NDEV * SLAB) for d in d_cw]),
        jnp.stack([d.reshape(NDEV * SLAB) for d in d_cb]),
        loss_part,
    ]
    small_rows = 848
    (small_all,) = exchange("gather_small_grads", [_pack(small, small_rows)], scatter=False)
    g_gains_full, g_kvn, g_bf, g_cw_full, g_cb, loss = _unpack(sum_slots("sum_small", small_all), small_shapes)
    g_gains_mine = lax.dynamic_slice_in_dim(g_gains_full, me * (D // NDEV), D // NDEV, axis=2)
    g_cw_mine = lax.dynamic_slice_in_dim(g_cw_full, me * SLAB, SLAB, axis=2)

    small_w = [norm_gains, kv_norm, b_f, conv_w, conv_b]
    small_m = [m_norm_gains, m_kv_norm, m_b_f, m_conv_w, m_conv_b]
    small_v = [v_norm_gains, v_kv_norm, v_b_f, v_conv_w, v_conv_b]
    small_g = [g_gains_mine, g_kvn, g_bf, g_cw_mine, g_cb]
    shapes = [w.shape for w in small_w]
    rows = 320
    res = adamw("adamw_small", _pack(small_g, rows)[None], _pack(small_w, rows), _pack(small_m, rows), _pack(small_v, rows))
    _, s_delta, s_m, s_v = [_unpack(r, shapes) for r in res]

    big = {"qkv": (w_qkv_a, m_w_qkv_a, v_w_qkv_a), "oa": (w_o_a, m_w_o_a, v_w_o_a), "qb": (w_q_b, m_w_q_b, v_w_q_b),
           "ob": (w_o_b, m_w_o_b, v_w_o_b), "kvf": (w_kvf, m_w_kvf, v_w_kvf), "up": (w_up, m_w_up, v_w_up),
           "down": (w_down, m_w_down, v_w_down)}
    updated = {name: None for name in big}
    after = [s_delta[0]]
    for handle, what in sent:
        received = exchange_wait(f"scatter_wait_{what[0][0]}{what[0][1]}", handle, after)
        for (name, layer), rec in zip(what, received):
            cols = rec.shape[-1]
            rows = rec.size // (NDEV * cols)
            w, m, v = (a.reshape(-1, cols) for a in big[name])
            updated[name] = adamw(f"adamw_{name}{layer}", rec.reshape(NDEV, rows, cols), w, m, v, row0=layer * rows,
                                  prev=updated[name])
        after = [updated[name][0] for name, _ in what]
    big_out = [[r.reshape(big[name][0].shape) for r in updated[name]] for name in ("qkv", "oa", "qb", "ob", "kvf", "up", "down")]

    def pick(k):
        b = [o[k] for o in big_out]
        sm = {0: small_g, 1: s_delta, 2: s_m, 3: s_v}[k]
        return [sm[0], b[0], b[1], b[2], b[3], sm[1], b[4], sm[2], b[5], sm[3], sm[4], b[6]]

    return (loss.reshape(()), dh.reshape(1, T, D), *pick(0), *pick(1), *pick(2), *pick(3))
```

```python
import functools
import math

import jax
import jax.numpy as jnp
from jax import lax
from jax.experimental import pallas as pl
from jax.experimental.pallas import tpu as pltpu

F32 = jnp.float32
BF16 = jnp.bfloat16

T = 2048
D = 1024
DEPTH = 4
N_A = 2
HD = 64
A_GROUPS = ((128, 1), (512, 4), (2048, 16))
A_W = 768
B_HEADS = 16
DFF = 2816
NDEV = 8
SLAB = 2 * DFF // NDEV
KVF = 2 * D + B_HEADS
KVF_PAD = 2304
ROPE_DIM = 16
ROPE_THETA = 500000.0
EPS = 1e-6
NEG = -1e30
BLK = 128
LANES = 128
SCALE = HD ** -0.5
VMEM_LIMIT = 56 * 1024 * 1024

ADAM_LR = 0.001
ADAM_B1 = 0.9
ADAM_B2 = 0.999
ADAM_EPS = 1e-08
ADAM_WD = 0.01
ADAM_STEP = 10
HIGHEST = lax.Precision.HIGHEST


def _params(*sem):
    return pltpu.CompilerParams(dimension_semantics=sem or None, vmem_limit_bytes=VMEM_LIMIT)


def _bf(x):
    return x if x.dtype == BF16 else x.astype(BF16)


def matmul(name, a, b, *, ta=False, tb=False, out_dtype=F32, tm=512, tn=512, batch=None):
    a_b = batch in ("a_out", "reduce")
    b_b = batch in ("b_out", "reduce")
    o_b = batch in ("a_out", "b_out")
    nb = a.shape[0] if a_b else (b.shape[0] if b_b else 1)
    ash = a.shape[1:] if a_b else a.shape
    bsh = b.shape[1:] if b_b else b.shape
    m, k = (ash[1], ash[0]) if ta else ash
    k2, n = (bsh[1], bsh[0]) if tb else bsh
    assert k == k2, (name, a.shape, b.shape)
    tm, tn = min(tm, m), min(tn, n)
    assert m % tm == 0 and n % tn == 0, (name, m, n, tm, tn)
    nbr = nb if batch == "reduce" else 1
    grid = (nb if o_b else 1, n // tn, m // tm, nbr)

    def bidx(bo, br):
        return bo if o_b else br

    def spec(batched, block, idx):
        if batched:
            return pl.BlockSpec((None,) + block, lambda bo, j, i, br: (bidx(bo, br),) + idx(i, j))
        return pl.BlockSpec(block, lambda bo, j, i, br: idx(i, j))

    a_spec = spec(a_b, (k, tm) if ta else (tm, k), (lambda i, j: (0, i)) if ta else (lambda i, j: (i, 0)))
    b_spec = spec(b_b, (tn, k) if tb else (k, tn), (lambda i, j: (j, 0)) if tb else (lambda i, j: (0, j)))
    o_spec = spec(o_b, (tm, tn), lambda i, j: (i, j))
    dims = (((0 if ta else 1,), (1 if tb else 0,)), ((), ()))

    def body(a_ref, b_ref, o_ref, *acc):
        p = lax.dot_general(_bf(a_ref[...]), _bf(b_ref[...]), dims, preferred_element_type=F32)
        if nbr == 1:
            o_ref[...] = p.astype(out_dtype)
        else:
            r = pl.program_id(3)

            @pl.when(r == 0)
            def _():
                acc[0][...] = p

            @pl.when(r > 0)
            def _():
                acc[0][...] += p

            @pl.when(r == nbr - 1)
            def _():
                o_ref[...] = acc[0][...].astype(out_dtype)

    out_shape = ((nb,) if o_b else ()) + (m, n)
    return pl.pallas_call(
        body, name=name, grid=grid, in_specs=[a_spec, b_spec], out_specs=o_spec,
        out_shape=jax.ShapeDtypeStruct(out_shape, out_dtype),
        scratch_shapes=[pltpu.VMEM((tm, tn), F32)] if nbr > 1 else [],
        compiler_params=_params("parallel", "parallel", "parallel", "arbitrary"),
    )(a, b)


def _rms(x, g):
    return x * lax.rsqrt(jnp.mean(x * x, axis=-1, keepdims=True) + EPS) * g


def resid_norm(name, h, y, gy, gains, tb=256, dep=None):
    n_g = len(gains)
    has_y = y is not None
    has_dep = dep is not None
    row = pl.BlockSpec((tb, D), lambda i: (i, 0))
    vec = pl.BlockSpec((1, D), lambda i: (0, 0))

    def body(*refs):
        h_ref = refs[0]
        pos = 1
        hn = h_ref[...]
        if has_y:
            hn = hn + _rms(refs[1][...], refs[2][...])
            pos = 3
        g_refs = refs[pos:pos + n_g]
        outs = refs[pos + n_g + has_dep:]
        if has_y:
            outs[0][...] = hn
            outs = outs[1:]
        for g_ref, o_ref in zip(g_refs, outs):
            o_ref[...] = _rms(hn, g_ref[...]).astype(BF16)

    ins = [h] + ([y, gy.reshape(1, D)] if has_y else []) + [g.reshape(1, D) for g in gains] + ([dep] if has_dep else [])
    in_specs = [row] + ([row, vec] if has_y else []) + [vec] * n_g + [pl.BlockSpec(memory_space=pl.ANY)] * has_dep
    out_shape = ([jax.ShapeDtypeStruct((T, D), F32)] if has_y else []) + [jax.ShapeDtypeStruct((T, D), BF16)] * n_g
    res = pl.pallas_call(
        body, name=name, grid=(T // tb,), in_specs=in_specs, out_specs=[row] * len(out_shape),
        out_shape=out_shape, compiler_params=_params("parallel"),
    )(*ins)
    return (res[0], list(res[1:])) if has_y else (h, list(res))


def rms_bwd(name, x, g, dy, add=None, out_dtype=F32, tb=256, dep=None):
    has_add = add is not None
    has_dep = dep is not None
    row = pl.BlockSpec((tb, D), lambda i: (i, 0))
    vec = pl.BlockSpec((1, D), lambda i: (0, 0))

    def body(*refs):
        x_ref, g_ref, dy_ref = refs[:3]
        dx_ref, dg_ref = refs[-2:]
        xv = x_ref[...]
        dyv = dy_ref[...].astype(F32)
        r = lax.rsqrt(jnp.mean(xv * xv, axis=-1, keepdims=True) + EPS)
        gdy = dyv * g_ref[...]
        dx = r * gdy - xv * (r * r * r * jnp.mean(xv * gdy, axis=-1, keepdims=True))
        if has_add:
            dx = dx + refs[3][...]
        dx_ref[...] = dx.astype(out_dtype)
        part = jnp.sum(dyv * xv * r, axis=0, keepdims=True)

        @pl.when(pl.program_id(0) == 0)
        def _():
            dg_ref[...] = part

        @pl.when(pl.program_id(0) > 0)
        def _():
            dg_ref[...] += part

    ins = [x, g.reshape(1, D), dy] + ([add] if has_add else []) + ([dep] if has_dep else [])
    return pl.pallas_call(
        body, name=name, grid=(T // tb,),
        in_specs=[row, vec, row] + ([row] if has_add else []) + [pl.BlockSpec(memory_space=pl.ANY)] * has_dep,
        out_specs=[row, vec],
        out_shape=[jax.ShapeDtypeStruct((T, D), out_dtype), jax.ShapeDtypeStruct((1, D), F32)],
        compiler_params=_params("arbitrary"),
    )(*ins)


def rope_tables():
    pos = jnp.arange(T, dtype=F32)
    inv = ROPE_THETA ** (-jnp.arange(0, ROPE_DIM, 2, dtype=F32) / ROPE_DIM)
    ang = pos[:, None] * inv[None, :]
    cos, sin = jnp.cos(ang), jnp.sin(ang)
    half = ROPE_DIM // 2
    one = jnp.ones((T, HD - ROPE_DIM), F32)
    zero = jnp.zeros((T, HD - ROPE_DIM), F32)
    zh = jnp.zeros((T, half), F32)
    c = jnp.concatenate([cos, cos, one], axis=1)
    s_up = jnp.concatenate([zh, sin, zero], axis=1)
    s_dn = jnp.concatenate([-sin, zh, zero], axis=1)
    rep = LANES // HD
    return tuple(jnp.tile(t, (1, rep)) for t in (c, s_up, s_dn))


def rope(name, t, tables, inverse, out_dtype=BF16, tb=512):
    c, s_up, s_dn = tables
    n_rot = 2 * A_W // LANES
    half = ROPE_DIM // 2
    blk = pl.BlockSpec((tb, LANES), lambda i, j: (i, j))
    tab = pl.BlockSpec((tb, LANES), lambda i, j: (i, 0))

    def body(t_ref, c_ref, su_ref, sd_ref, o_ref):
        x = t_ref[...].astype(F32)
        sgn = -1.0 if inverse else 1.0
        rot = (x * c_ref[...] + pltpu.roll(x, half, 1) * (sgn * su_ref[...])
               + pltpu.roll(x, LANES - half, 1) * (sgn * sd_ref[...]))
        o_ref[...] = jnp.where(pl.program_id(1) < n_rot, rot, x).astype(out_dtype)

    return pl.pallas_call(
        body, name=name, grid=(T // tb, 3 * A_W // LANES), in_specs=[blk, tab, tab, tab], out_specs=blk,
        out_shape=jax.ShapeDtypeStruct((T, 3 * A_W), out_dtype), compiler_params=_params("parallel", "parallel"),
    )(t, c, s_up, s_dn)


GW = 4 * HD


def _band_mask(b):
    qi = lax.broadcasted_iota(jnp.int32, (BLK, 2 * BLK), 0)
    kj = lax.broadcasted_iota(jnp.int32, (BLK, 2 * BLK), 1)
    return (kj <= qi + BLK) & (kj >= qi) & ((kj >= BLK) | (b > 0))


def attn_a_fwd(name, qkvr, g, r):
    length = T // r
    nblk = length // BLK
    view = qkvr.reshape(length, r * 3 * A_W)
    ncol = 3 * A_W // GW

    def col(section, prev):
        def idx(j, b):
            return (jnp.maximum(b - 1, 0) if prev else b, j * ncol + 3 * section + g)
        return pl.BlockSpec((BLK, GW), idx)

    out = pl.BlockSpec((BLK, GW), lambda j, b: (b, j))

    def body(q_ref, kp_ref, kc_ref, vp_ref, vc_ref, o_ref, lse_ref):
        mask = _band_mask(pl.program_id(1))
        k2 = jnp.concatenate([kp_ref[...], kc_ref[...]], axis=0)
        v2 = jnp.concatenate([vp_ref[...], vc_ref[...]], axis=0)
        q = q_ref[...]
        for h in range(4):
            sl = slice(h * HD, (h + 1) * HD)
            s = lax.dot_general(q[:, sl], k2[:, sl], (((1,), (1,)), ((), ())), preferred_element_type=F32) * SCALE
            s = jnp.where(mask, s, NEG)
            m = jnp.max(s, axis=1, keepdims=True)
            p = jnp.exp(s - m)
            l = jnp.sum(p, axis=1, keepdims=True)
            o_ref[:, sl] = jnp.dot((p / l).astype(BF16), v2[:, sl], preferred_element_type=F32)
            lse_ref[:, sl] = jnp.broadcast_to(m + jnp.log(l), (BLK, HD))

    shape = jax.ShapeDtypeStruct((length, r * GW), F32)
    o, lse = pl.pallas_call(
        body, name=name, grid=(r, nblk),
        in_specs=[col(0, False), col(1, True), col(1, False), col(2, True), col(2, False)],
        out_specs=[out, out], out_shape=[shape, shape], compiler_params=_params("parallel", "parallel"),
    )(view, view, view, view, view)
    return o.reshape(T, GW), lse.reshape(T, GW)


def attn_a_bwd(name, qkvr, do, lse, dd, g, r):
    length = T // r
    nblk = length // BLK
    view = qkvr.reshape(length, r * 3 * A_W)
    ncol = 3 * A_W // GW

    def col(section, shift):
        def idx(j, b):
            return (jnp.clip(b + shift, 0, nblk - 1), j * ncol + 3 * section + g)
        return pl.BlockSpec((BLK, GW), idx)

    def tok(shift):
        return pl.BlockSpec((BLK, GW), lambda j, b: (jnp.clip(b + shift, 0, nblk - 1), j))

    def body(q_ref, qn_ref, kp_ref, kc_ref, vp_ref, vc_ref, do_ref, don_ref, lse_ref, lsen_ref, dd_ref, ddn_ref,
             dq_ref, dk_ref, dv_ref):
        b = pl.program_id(1)
        mask = _band_mask(b)
        qi = lax.broadcasted_iota(jnp.int32, (2 * BLK, BLK), 0)
        kj = lax.broadcasted_iota(jnp.int32, (2 * BLK, BLK), 1)
        kmask = ((qi < BLK) & (kj <= qi)) | ((qi >= BLK) & (kj >= qi - BLK) & (b + 1 < nblk))
        k2 = jnp.concatenate([kp_ref[...], kc_ref[...]], axis=0)
        v2 = jnp.concatenate([vp_ref[...], vc_ref[...]], axis=0)
        q2 = jnp.concatenate([q_ref[...], qn_ref[...]], axis=0)
        do2 = jnp.concatenate([do_ref[...], don_ref[...]], axis=0)
        lse2 = jnp.concatenate([lse_ref[...], lsen_ref[...]], axis=0)
        dd2 = jnp.concatenate([dd_ref[...], ddn_ref[...]], axis=0)
        nt = (((1,), (1,)), ((), ()))
        tn = (((0,), (0,)), ((), ()))
        for h in range(4):
            sl = slice(h * HD, (h + 1) * HD)
            one = slice(h * HD, h * HD + 1)
            qh, kh, vh, doh = q2[:, sl], k2[:, sl], v2[:, sl], do2[:, sl]
            s = lax.dot_general(qh[:BLK], kh, nt, preferred_element_type=F32) * SCALE
            p = jnp.where(mask, jnp.exp(s - lse2[:BLK, one]), 0.0)
            dp = lax.dot_general(doh[:BLK], vh, nt, preferred_element_type=F32)
            ds = p * (dp + dd2[:BLK, one])
            dq_ref[:, sl] = jnp.dot(ds.astype(BF16), kh, preferred_element_type=F32) * SCALE
            kc, vc = kh[BLK:], vh[BLK:]
            s = lax.dot_general(qh, kc, nt, preferred_element_type=F32) * SCALE
            p = jnp.where(kmask, jnp.exp(s - lse2[:, one]), 0.0)
            dp = lax.dot_general(doh, vc, nt, preferred_element_type=F32)
            ds = p * (dp + dd2[:, one])
            dk_ref[:, sl] = lax.dot_general(ds.astype(BF16), qh, tn, preferred_element_type=F32) * SCALE
            dv_ref[:, sl] = lax.dot_general(p.astype(BF16), doh, tn, preferred_element_type=F32)

    dov = do.reshape(length, r * GW)
    lsev = lse.reshape(length, r * GW)
    ddv = dd.reshape(length, r * GW)
    shape = jax.ShapeDtypeStruct((length, r * GW), F32)
    dq, dk, dv = pl.pallas_call(
        body, name=name, grid=(r, nblk),
        in_specs=[col(0, 0), col(0, 1), col(1, -1), col(1, 0), col(2, -1), col(2, 0),
                  tok(0), tok(1), tok(0), tok(1), tok(0), tok(1)],
        out_specs=[tok(0)] * 3, out_shape=[shape] * 3, compiler_params=_params("parallel", "parallel"),
    )(view, view, view, view, view, view, dov, dov, lsev, lsev, ddv, ddv)
    return dq.reshape(T, GW), dk.reshape(T, GW), dv.reshape(T, GW)


def _head_sum(x):
    i = lax.div(lax.broadcasted_iota(jnp.int32, (GW, GW), 0), jnp.int32(HD))
    j = lax.div(lax.broadcasted_iota(jnp.int32, (GW, GW), 1), jnp.int32(HD))
    return jnp.dot(x, (i == j).astype(F32), precision=HIGHEST, preferred_element_type=F32)


def _alphas(lses):
    m = jnp.maximum(jnp.maximum(lses[0], lses[1]), lses[2])
    e = [jnp.exp(l - m) for l in lses]
    z = e[0] + e[1] + e[2]
    return [x / z for x in e]


def combine_fwd(name, os_, lses, tb=256):
    blk = pl.BlockSpec((tb, GW), lambda i: (i, 0))

    def body(o0, o1, o2, l0, l1, l2, oc_ref):
        al = _alphas([l0[...], l1[...], l2[...]])
        for g, o_ref in enumerate((o0, o1, o2)):
            oc_ref[:, g * GW:(g + 1) * GW] = (o_ref[...] * al[g]).astype(BF16)

    return pl.pallas_call(
        body, name=name, grid=(T // tb,), in_specs=[blk] * 6, out_specs=pl.BlockSpec((tb, A_W), lambda i: (i, 0)),
        out_shape=jax.ShapeDtypeStruct((T, A_W), BF16), compiler_params=_params("parallel"),
    )(*os_, *lses)


def combine_bwd(name, doc, os_, lses, tb=256):
    blk = pl.BlockSpec((tb, GW), lambda i: (i, 0))

    def body(doc_ref, o0, o1, o2, l0, l1, l2, d0, d1, d2, e0, e1, e2):
        al = _alphas([l0[...], l1[...], l2[...]])
        dal = [_head_sum(doc_ref[:, g * GW:(g + 1) * GW] * o_ref[...]) for g, o_ref in enumerate((o0, o1, o2))]
        mean = al[0] * dal[0] + al[1] * dal[1] + al[2] * dal[2]
        for g, (do_ref, dd_ref) in enumerate(((d0, e0), (d1, e1), (d2, e2))):
            do_ref[...] = (doc_ref[:, g * GW:(g + 1) * GW] * al[g]).astype(BF16)
            dd_ref[...] = al[g] * (dal[g] - mean) - al[g] * dal[g]

    res = pl.pallas_call(
        body, name=name, grid=(T // tb,), in_specs=[pl.BlockSpec((tb, A_W), lambda i: (i, 0))] + [blk] * 6,
        out_specs=[blk] * 6,
        out_shape=[jax.ShapeDtypeStruct((T, GW), BF16)] * 3 + [jax.ShapeDtypeStruct((T, GW), F32)] * 3,
        compiler_params=_params("parallel"),
    )(doc, *os_, *lses)
    return res[:3], res[3:]


TQ = 512
TK = TQ
FOX_HEADS = 1
NT = (((1,), (1,)), ((), ()))
TN = (((0,), (0,)), ((), ()))


def _fox_scores(q, kb, cj, row0, col0, masked):
    s = lax.dot_general(q, kb, NT, preferred_element_type=F32) * SCALE - cj
    if masked:
        qi = row0 + lax.broadcasted_iota(jnp.int32, s.shape, 0)
        kj = col0 + lax.broadcasted_iota(jnp.int32, s.shape, 1)
        s = jnp.where(kj <= qi, s, NEG)
    return s


def fox_fwd(name, q, k, v, ct):
    def body(q_ref, k_ref, v_ref, c_ref, o_ref, lse_ref):
        i = pl.program_id(1)
        n_full = i

        def step(jb, carry, masked):
            start = pl.multiple_of(jb * TK, TK)
            out = []
            for h in range(FOX_HEADS):
                m, l, acc = carry[h]
                kb = k_ref[h, pl.ds(start, TK), :]
                vb = v_ref[h, pl.ds(start, TK), :]
                s = _fox_scores(q_ref[h], kb, c_ref[h, pl.ds(jb, 1), :], i * TQ, jb * TK, masked)
                m_new = jnp.maximum(m, jnp.max(s, axis=1, keepdims=True))
                a = jnp.exp(m - m_new)
                p = jnp.exp(s - m_new)
                l = a * l + jnp.sum(p, axis=1, keepdims=True)
                acc = a * acc + jnp.dot(p.astype(BF16), vb, preferred_element_type=F32)
                out.append((m_new, l, acc))
            return tuple(out)

        init = tuple((jnp.full((TQ, 1), NEG, F32), jnp.zeros((TQ, 1), F32), jnp.zeros((TQ, HD), F32))
                     for _ in range(FOX_HEADS))
        carry = lax.fori_loop(0, n_full, lambda jb, c: step(jb, c, False), init)
        carry = step(n_full, carry, True)
        for h in range(FOX_HEADS):
            m, l, acc = carry[h]
            o_ref[h] = (acc / l).astype(BF16)
            lse_ref[h] = jnp.broadcast_to(m + jnp.log(l), (TQ, LANES))

    head = lambda h, i: (h, 0, 0)
    return pl.pallas_call(
        body, name=name, grid=(B_HEADS // FOX_HEADS, T // TQ),
        in_specs=[pl.BlockSpec((FOX_HEADS, TQ, HD), lambda h, i: (h, i, 0)), pl.BlockSpec((FOX_HEADS, T, HD), head),
                  pl.BlockSpec((FOX_HEADS, T, HD), head), pl.BlockSpec((FOX_HEADS, T // TK, TK), head)],
        out_specs=[pl.BlockSpec((FOX_HEADS, TQ, HD), lambda h, i: (h, i, 0)),
                   pl.BlockSpec((FOX_HEADS, TQ, LANES), lambda h, i: (h, i, 0))],
        out_shape=[jax.ShapeDtypeStruct((B_HEADS, T, HD), BF16), jax.ShapeDtypeStruct((B_HEADS, T, LANES), F32)],
        compiler_params=_params("parallel", "parallel"),
    )(q, k, v, ct)


def fox_bwd_q(name, q, k, v, ct, o, do, lse):
    def body(q_ref, k_ref, v_ref, c_ref, o_ref, do_ref, lse_ref, dq_ref, drow_ref):
        i = pl.program_id(1)
        n_full = i
        delta = [jnp.sum(do_ref[h].astype(F32) * o_ref[h].astype(F32), axis=1, keepdims=True)
                 for h in range(FOX_HEADS)]

        def step(jb, carry, masked):
            start = pl.multiple_of(jb * TK, TK)
            out = []
            for h in range(FOX_HEADS):
                dq, drow = carry[h]
                kb = k_ref[h, pl.ds(start, TK), :]
                vb = v_ref[h, pl.ds(start, TK), :]
                s = _fox_scores(q_ref[h], kb, c_ref[h, pl.ds(jb, 1), :], i * TQ, jb * TK, masked)
                p = jnp.exp(s - lse_ref[h, :, 0:1])
                dp = lax.dot_general(do_ref[h], vb, NT, preferred_element_type=F32)
                ds = p * (dp - delta[h])
                out.append((dq + jnp.dot(ds.astype(BF16), kb, preferred_element_type=F32),
                            drow + jnp.sum(ds, axis=1, keepdims=True)))
            return tuple(out)

        init = tuple((jnp.zeros((TQ, HD), F32), jnp.zeros((TQ, 1), F32)) for _ in range(FOX_HEADS))
        carry = lax.fori_loop(0, n_full, lambda jb, c: step(jb, c, False), init)
        carry = step(n_full, carry, True)
        for h in range(FOX_HEADS):
            dq, drow = carry[h]
            dq_ref[h] = (dq * SCALE).astype(BF16)
            drow_ref[h] = jnp.broadcast_to(drow, (TQ, LANES))

    head = lambda h, i: (h, 0, 0)
    blk = pl.BlockSpec((FOX_HEADS, TQ, HD), lambda h, i: (h, i, 0))
    stat = pl.BlockSpec((FOX_HEADS, TQ, LANES), lambda h, i: (h, i, 0))
    return pl.pallas_call(
        body, name=name, grid=(B_HEADS // FOX_HEADS, T // TQ),
        in_specs=[blk, pl.BlockSpec((FOX_HEADS, T, HD), head), pl.BlockSpec((FOX_HEADS, T, HD), head),
                  pl.BlockSpec((FOX_HEADS, T // TK, TK), head), blk, blk, stat],
        out_specs=[blk, stat],
        out_shape=[jax.ShapeDtypeStruct((B_HEADS, T, HD), BF16), jax.ShapeDtypeStruct((B_HEADS, T, LANES), F32)],
        compiler_params=_params("parallel", "parallel"),
    )(q, k, v, ct, o, do, lse)


def fox_bwd_kv(name, q, k, v, ct, o, do, lse, prev):
    has_prev = prev is not None
    nq = T // TQ

    def body(*refs):
        q_ref, k_ref, v_ref, c_ref, o_ref, do_ref, lse_ref = refs[:7]
        dk_ref, dv_ref, dc_ref = refs[-3:]
        jb = pl.program_id(1)
        dk_ref[...] = jnp.zeros_like(dk_ref)
        dv_ref[...] = jnp.zeros_like(dv_ref)

        def step(i, dcs, masked):
            start = pl.multiple_of(i * TQ, TQ)
            out = []
            for h in range(FOX_HEADS):
                qv = q_ref[h, pl.ds(start, TQ), :]
                dov = do_ref[h, pl.ds(start, TQ), :]
                ov = o_ref[h, pl.ds(start, TQ), :]
                lsev = lse_ref[h, pl.ds(start, TQ), 0:1]
                delta = jnp.sum(dov.astype(F32) * ov.astype(F32), axis=1, keepdims=True)
                s = _fox_scores(qv, k_ref[h], c_ref[h, pl.ds(jb, 1), :], i * TQ, jb * TK, masked)
                p = jnp.exp(s - lsev)
                dp = lax.dot_general(dov, v_ref[h], NT, preferred_element_type=F32)
                ds = p * (dp - delta)
                dv_ref[h] += lax.dot_general(p.astype(BF16), dov, TN, preferred_element_type=F32)
                dk_ref[h] += lax.dot_general(ds.astype(BF16), qv, TN, preferred_element_type=F32)
                out.append(dcs[h] - jnp.sum(ds, axis=0, keepdims=True))
            return tuple(out)

        dcs = tuple(jnp.zeros((1, TK), F32) for _ in range(FOX_HEADS))
        dcs = step(jb, dcs, True)
        dcs = lax.fori_loop(jb + 1, nq, lambda i, c: step(i, c, False), dcs)
        for h in range(FOX_HEADS):
            dk = dk_ref[h] * SCALE
            dc = dcs[h]
            if has_prev:
                dk = dk + refs[7][h]
                dv_ref[h] += refs[8][h]
                dc = dc + refs[9][h, pl.ds(jb, 1), :]
            dk_ref[h] = dk
            dc_ref[h, pl.ds(jb, 1), :] = dc

    head = lambda h, j: (h, 0, 0)
    full = pl.BlockSpec((FOX_HEADS, T, HD), head)
    blk = pl.BlockSpec((FOX_HEADS, TK, HD), lambda h, j: (h, j, 0))
    cspec = pl.BlockSpec((FOX_HEADS, T // TK, TK), head)
    ins = [q, k, v, ct, o, do, lse] + (list(prev) if has_prev else [])
    in_specs = [full, blk, blk, cspec, full, full, pl.BlockSpec((FOX_HEADS, T, LANES), head)] + ([blk, blk, cspec] if has_prev else [])
    return pl.pallas_call(
        body, name=name, grid=(B_HEADS // FOX_HEADS, T // TK), in_specs=in_specs, out_specs=[blk, blk, cspec],
        out_shape=[jax.ShapeDtypeStruct((B_HEADS, T, HD), F32)] * 2 + [jax.ShapeDtypeStruct((B_HEADS, T // TK, TK), F32)],
        compiler_params=_params("parallel", "arbitrary"),
    )(*ins)


def _tri(upper):
    i = lax.broadcasted_iota(jnp.int32, (BLK, BLK), 0)
    j = lax.broadcasted_iota(jnp.int32, (BLK, BLK), 1)
    return ((i <= j) if upper else (i >= j)).astype(F32)


def gates_fwd(name, fgt, b_f):
    def body(f_ref, b_ref, c_ref):
        tri = _tri(True)
        carry = jnp.zeros((B_HEADS, 1), F32)
        for blk in range(T // BLK):
            sl = slice(blk * BLK, (blk + 1) * BLK)
            z = f_ref[:, sl] + b_ref[...]
            logf = jnp.minimum(z, 0.0) - jnp.log(1.0 + jnp.exp(-jnp.abs(z)))
            cs = jnp.dot(logf, tri, precision=HIGHEST, preferred_element_type=F32) + carry
            c_ref[:, sl] = cs
            carry = cs[:, BLK - 1:BLK]

    return pl.pallas_call(
        body, name=name, out_shape=jax.ShapeDtypeStruct((B_HEADS, T), F32), compiler_params=_params(),
    )(fgt, b_f)


def gates_bwd(name, fgt, b_f, dcs):
    n_dc = len(dcs)

    def body(*refs):
        f_ref, b_ref = refs[:2]
        dc_refs = refs[2:2 + n_dc]
        dz_ref, db_ref = refs[2 + n_dc:]
        tri = _tri(False)
        carry = jnp.zeros((B_HEADS, 1), F32)
        db = jnp.zeros((B_HEADS, 1), F32)
        for blk in reversed(range(T // BLK)):
            sl = slice(blk * BLK, (blk + 1) * BLK)
            dc = dc_refs[0][:, sl]
            for r in dc_refs[1:]:
                dc = dc + r[:, sl]
            rc = jnp.dot(dc, tri, precision=HIGHEST, preferred_element_type=F32) + carry
            carry = rc[:, 0:1]
            z = f_ref[:, sl] + b_ref[...]
            e = jnp.exp(-jnp.abs(z))
            dz = rc * jnp.where(z >= 0.0, e, 1.0) / (1.0 + e)
            dz_ref[:, sl] = dz
            db = db + jnp.sum(dz, axis=1, keepdims=True)
        db_ref[...] = db

    return pl.pallas_call(
        body, name=name,
        out_shape=[jax.ShapeDtypeStruct((B_HEADS, T), F32), jax.ShapeDtypeStruct((B_HEADS, 1), F32)],
        compiler_params=_params(),
    )(fgt, b_f, *dcs)


CONV_TB = 256
GELU_K = math.sqrt(2.0 / math.pi)
GELU_C = 0.044715


def _shift_down(x, halo_ref, n):
    rows = lax.broadcasted_iota(jnp.int32, x.shape, 0)
    y = pltpu.roll(x, n, 0)
    for k in range(n):
        y = jnp.where(rows == k, halo_ref[pl.ds(8 - n + k, 1), :], y)
    return y


def _conv(x, halo_ref, cw_ref, cb_ref, first):
    x1 = _shift_down(x, halo_ref, 1)
    x2 = _shift_down(x, halo_ref, 2)
    rows = lax.broadcasted_iota(jnp.int32, x.shape, 0)
    x1 = jnp.where(first & (rows < 1), 0.0, x1)
    x2 = jnp.where(first & (rows < 2), 0.0, x2)
    y = x2 * cw_ref[0:1, :] + x1 * cw_ref[1:2, :] + x * cw_ref[2:3, :] + cb_ref[...]
    return y, x1, x2


def _gelu_parts(x):
    th = jnp.tanh(GELU_K * (x + GELU_C * x * x * x))
    val = 0.5 * x * (1.0 + th)
    grad = 0.5 * (1.0 + th) + 0.5 * x * (1.0 - th * th) * GELU_K * (1.0 + 3.0 * GELU_C * x * x)
    return val, grad


def _conv_specs(tb):
    def slab(off):
        return pl.BlockSpec((None, tb, SLAB), lambda d, i: (d + off, i, 0))

    def halo(off):
        return pl.BlockSpec((None, 8, SLAB), lambda d, i: (d + off, jnp.maximum(i * (tb // 8) - 1, 0), 0))

    def par(rows, off):
        return pl.BlockSpec((None, rows, SLAB), lambda d, i: (d + off, 0, 0))

    return slab, halo, par


def convglu_fwd(name, a, cw, cb, tb=CONV_TB):
    slab, halo, par = _conv_specs(tb)

    def body(ag, hg, av, hv, cwg, cbg, cwv, cbv, u_ref):
        first = pl.program_id(1) == 0
        gate, _, _ = _conv(ag[...], hg, cwg, cbg, first)
        val, _, _ = _conv(av[...], hv, cwv, cbv, first)
        u_ref[...] = (_gelu_parts(gate)[0] * val).astype(BF16)

    return pl.pallas_call(
        body, name=name, grid=(4, T // tb),
        in_specs=[slab(0), halo(0), slab(4), halo(4), par(3, 0), par(1, 0), par(3, 4), par(1, 4)],
        out_specs=slab(0), out_shape=jax.ShapeDtypeStruct((4, T, SLAB), BF16),
        compiler_params=_params("parallel", "parallel"),
    )(a, a, a, a, cw, cb, cw, cb)


def convglu_bwd(name, du, a, cw, cb, tb=CONV_TB):
    slab, halo, par = _conv_specs(tb)

    def body(du_ref, ag, hg, av, hv, cwg, cbg, cwv, cbv, dg_ref, dv_ref, dcwg, dcbg, dcwv, dcbv):
        first = pl.program_id(1) == 0
        gate, g1, g2 = _conv(ag[...], hg, cwg, cbg, first)
        val, v1, v2 = _conv(av[...], hv, cwv, cbv, first)
        act, dact = _gelu_parts(gate)
        duv = du_ref[...].astype(F32)
        dgate = duv * val * dact
        dval = duv * act
        dg_ref[...] = dgate
        dv_ref[...] = dval
        for dy, xs, dcw_ref, dcb_ref in ((dgate, (g2, g1, ag[...]), dcwg, dcbg), (dval, (v2, v1, av[...]), dcwv, dcbv)):
            parts = [jnp.sum(dy * x, axis=0, keepdims=True) for x in xs]
            bias = jnp.sum(dy, axis=0, keepdims=True)

            @pl.when(first)
            def _():
                for k in range(3):
                    dcw_ref[k:k + 1, :] = parts[k]
                dcb_ref[...] = bias

            @pl.when(jnp.logical_not(first))
            def _():
                for k in range(3):
                    dcw_ref[k:k + 1, :] += parts[k]
                dcb_ref[...] += bias

    res = pl.pallas_call(
        body, name=name, grid=(4, T // tb),
        in_specs=[slab(0), slab(0), halo(0), slab(4), halo(4), par(3, 0), par(1, 0), par(3, 4), par(1, 4)],
        out_specs=[slab(0), slab(0), par(3, 0), par(1, 0), par(3, 0), par(1, 0)],
        out_shape=[jax.ShapeDtypeStruct((4, T, SLAB), F32)] * 2
        + [jax.ShapeDtypeStruct((4, 3, SLAB), F32), jax.ShapeDtypeStruct((4, 1, SLAB), F32)] * 2,
        compiler_params=_params("parallel", "arbitrary"),
    )(du, a, a, a, a, cw, cb, cw, cb)
    dgate, dval, dcwg, dcbg, dcwv, dcbv = res
    return (dgate, dval), jnp.concatenate([dcwg, dcwv], axis=0), jnp.concatenate([dcbg, dcbv], axis=0)


def conv_bwd_input(name, dac_pair, cw, tb=CONV_TB):
    nblk = T // tb

    def run(x, off):
        def body(x_ref, nx_ref, cw_ref, da_ref):
            last = pl.program_id(1) == nblk - 1
            xv = x_ref[...]
            rows = lax.broadcasted_iota(jnp.int32, xv.shape, 0)

            def up(n):
                y = pltpu.roll(xv, tb - n, 0)
                for k in range(n):
                    y = jnp.where(rows == tb - n + k, nx_ref[pl.ds(k, 1), :], y)
                return jnp.where(last & (rows >= tb - n), 0.0, y)

            da_ref[...] = (xv * cw_ref[2:3, :] + up(1) * cw_ref[1:2, :] + up(2) * cw_ref[0:1, :]).astype(BF16)

        return pl.pallas_call(
            body, name=f"{name}_{off}", grid=(4, nblk),
            in_specs=[pl.BlockSpec((None, tb, SLAB), lambda d, i: (d, i, 0)),
                      pl.BlockSpec((None, 8, SLAB), lambda d, i: (d, jnp.minimum((i + 1) * (tb // 8), T // 8 - 1), 0)),
                      pl.BlockSpec((None, 3, SLAB), lambda d, i: (d + off, 0, 0))],
            out_specs=pl.BlockSpec((None, tb, SLAB), lambda d, i: (d, i, 0)),
            out_shape=jax.ShapeDtypeStruct((4, T, SLAB), BF16), compiler_params=_params("parallel", "parallel"),
        )(x, x, cw)

    return jnp.concatenate([run(dac_pair[0], 0), run(dac_pair[1], 4)], axis=0)


def loss_head(name, y, target, tb=256):
    row = pl.BlockSpec((tb, D), lambda i: (i, 0))

    def body(y_ref, t_ref, dy_ref, loss_ref):
        diff = y_ref[...] - t_ref[...]
        dy_ref[...] = diff * (1.0 / D)
        part = jnp.sum(jnp.sum(diff * diff, axis=1, keepdims=True), axis=0, keepdims=True) * (0.5 / D)

        @pl.when(pl.program_id(0) == 0)
        def _():
            loss_ref[...] = part

        @pl.when(pl.program_id(0) > 0)
        def _():
            loss_ref[...] += part

    return pl.pallas_call(
        body, name=name, grid=(T // tb,), in_specs=[row, row],
        out_specs=[row, pl.BlockSpec((1, 1), lambda i: (0, 0))],
        out_shape=[jax.ShapeDtypeStruct((T, D), F32), jax.ShapeDtypeStruct((1, 1), F32)],
        compiler_params=_params("arbitrary"),
    )(y, target)


def _row_tile(rows, cols, bytes_per_elem, budget=6 * 1024 * 1024):
    for tr in (1024, 512, 256, 128, 64, 32, 16, 8):
        if rows % tr == 0 and tr * cols * bytes_per_elem <= budget:
            return tr
    return rows


def adamw(name, parts, w, m, v, row0=0, prev=None):
    n_parts, rows, cols = parts.shape
    rows_all = w.shape[0]
    tr = _row_tile(math.gcd(rows, row0) if row0 else rows, cols, n_parts * parts.dtype.itemsize + 28)
    blk = pl.BlockSpec((tr, cols), lambda i: (row0 // tr + i, 0))
    b1c = 1.0 - ADAM_B1 ** ADAM_STEP
    b2c = 1.0 - ADAM_B2 ** ADAM_STEP
    n_prev = 0 if prev is None else 4

    def body(p_ref, w_ref, m_ref, v_ref, *rest):
        g_ref, d_ref, nm_ref, nv_ref = rest[n_prev:]
        g = p_ref[0].astype(F32)
        for k in range(1, n_parts):
            g = g + p_ref[k].astype(F32)
        nm = ADAM_B1 * m_ref[...] + (1.0 - ADAM_B1) * g
        nv = ADAM_B2 * v_ref[...] + (1.0 - ADAM_B2) * (g * g)
        g_ref[...] = g
        nm_ref[...] = nm
        nv_ref[...] = nv
        d_ref[...] = -ADAM_LR * ((nm / b1c) / (jnp.sqrt(nv / b2c) + ADAM_EPS) + ADAM_WD * w_ref[...])

    return pl.pallas_call(
        body, name=name, grid=(rows // tr,),
        in_specs=[pl.BlockSpec((n_parts, tr, cols), lambda i: (0, i, 0)), blk, blk, blk]
        + [pl.BlockSpec(memory_space=pl.ANY)] * n_prev,
        out_specs=[blk] * 4, out_shape=[jax.ShapeDtypeStruct((rows_all, cols), F32)] * 4,
        input_output_aliases={4 + k: k for k in range(n_prev)}, compiler_params=_params("parallel"),
    )(parts, w, m, v, *(prev or []))


def exchange(name, items, scatter):
    n = len(items)
    hbm = pl.BlockSpec(memory_space=pltpu.HBM)

    def body(*refs):
        ins, outs = refs[:n], refs[n:2 * n]
        send_sems, recv_sems, local_sems = refs[2 * n:]
        x, y, c = lax.axis_index("x"), lax.axis_index("y"), lax.axis_index("c")
        me = 4 * x + 2 * y + c
        copies = []
        for t in range(n):
            own = pltpu.make_async_copy(ins[t].at[me] if scatter else ins[t], outs[t].at[me], local_sems.at[t])
            own.start()
            copies.append(own)
            for rel in range(1, NDEV):
                px = 1 - x if rel & 4 else x
                py = 1 - y if rel & 2 else y
                pc = 1 - c if rel & 1 else c
                src = ins[t].at[4 * px + 2 * py + pc] if scatter else ins[t]
                cp = pltpu.make_async_remote_copy(
                    src_ref=src, dst_ref=outs[t].at[me], send_sem=send_sems.at[t, rel - 1],
                    recv_sem=recv_sems.at[t, rel - 1], device_id=(px, py, pc), device_id_type=pl.DeviceIdType.MESH)
                cp.start()
                copies.append(cp)
        for cp in copies:
            cp.wait()

    out_shape = [jax.ShapeDtypeStruct(it.shape if scatter else (NDEV,) + it.shape, it.dtype) for it in items]
    return pl.pallas_call(
        body, name=name, in_specs=[hbm] * n, out_specs=[hbm] * n, out_shape=out_shape,
        scratch_shapes=[pltpu.SemaphoreType.DMA((n, NDEV - 1)), pltpu.SemaphoreType.DMA((n, NDEV - 1)),
                        pltpu.SemaphoreType.DMA((n,))],
    )(*items)


def _peer(rel, x, y, c):
    return (1 - x if rel & 4 else x, 1 - y if rel & 2 else y, 1 - c if rel & 1 else c)


def _split_copies(ins, lands, send_sems, recv_sems, scatter):
    x, y, c = lax.axis_index("x"), lax.axis_index("y"), lax.axis_index("c")
    me = 4 * x + 2 * y + c
    copies = []
    for t in range(len(ins)):
        for rel in range(1, NDEV):
            px, py, pc = _peer(rel, x, y, c)
            src = ins[t].at[4 * px + 2 * py + pc] if scatter else ins[t]
            copies.append(pltpu.make_async_remote_copy(
                src_ref=src, dst_ref=lands[t].at[me], send_sem=send_sems.at[t * (NDEV - 1) + rel - 1],
                recv_sem=recv_sems.at[t * (NDEV - 1) + rel - 1], device_id=(px, py, pc),
                device_id_type=pl.DeviceIdType.MESH))
    return me, copies


def exchange_start(name, items, scatter, dep=None):
    n = len(items)
    hbm = pl.BlockSpec(memory_space=pltpu.HBM)
    sem = pl.BlockSpec(memory_space=pltpu.SEMAPHORE)
    has_dep = dep is not None
    land_shapes = [it.shape if scatter else (NDEV,) + it.shape for it in items]

    def body(*refs):
        ins, lands = refs[:n], refs[n:2 * n]
        outs = refs[2 * n + has_dep:]
        send_sems, recv_sems, token = outs[0], outs[1], outs[2 + 2 * n]
        _, copies = _split_copies(ins, lands, send_sems, recv_sems, scatter)
        for cp in copies:
            cp.start()
        token[...] = jnp.zeros_like(token)

    sems = pltpu.SemaphoreType.DMA((n * (NDEV - 1),))
    out_shape = ([sems, sems] + [pltpu.HBM(it.shape, it.dtype) for it in items]
                 + [pltpu.HBM(sh, it.dtype) for sh, it in zip(land_shapes, items)] + [jax.ShapeDtypeStruct((8, LANES), F32)])
    operands = ([pltpu.with_memory_space_constraint(it, pltpu.HBM) for it in items]
                + [pltpu.with_memory_space_constraint(lax.empty(sh, it.dtype), pltpu.HBM) for sh, it in zip(land_shapes, items)]
                + ([dep] if has_dep else []))
    res = pl.pallas_call(
        body, name=name, in_specs=[hbm] * (2 * n) + [pl.BlockSpec(memory_space=pl.ANY)] * has_dep,
        out_specs=[sem, sem] + [hbm] * (2 * n) + [pl.BlockSpec(memory_space=pltpu.VMEM)], out_shape=out_shape,
        input_output_aliases={t: 2 + t for t in range(2 * n)},
        compiler_params=pltpu.CompilerParams(has_side_effects=pltpu.SideEffectType.DATAFLOW_SIDE_EFFECTING),
    )(*operands)
    return (res[0], res[1], list(res[2:2 + n]), list(res[2 + n:2 + 2 * n]), scatter), res[2 + 2 * n]


def exchange_wait(name, handle, after):
    send_sems, recv_sems, ins, lands, scatter = handle
    n = len(ins)
    after = list(after) if isinstance(after, (list, tuple)) else [after]
    hbm = pl.BlockSpec(memory_space=pltpu.HBM)
    sem = pl.BlockSpec(memory_space=pltpu.SEMAPHORE)

    def body(*refs):
        _, copies = _split_copies(refs[:n], refs[n:2 * n], refs[2 * n], refs[2 * n + 1], scatter)
        for cp in copies:
            cp.wait_send()
            cp.wait_recv()

    res = pl.pallas_call(
        body, name=name, in_specs=[hbm] * (2 * n) + [sem, sem] + [pl.BlockSpec(memory_space=pl.ANY)] * len(after),
        out_specs=[hbm] * (2 * n), out_shape=[pltpu.HBM(a.shape, a.dtype) for a in ins + lands],
        input_output_aliases={t: t for t in range(2 * n)},
        compiler_params=pltpu.CompilerParams(has_side_effects=pltpu.SideEffectType.DATAFLOW_SIDE_EFFECTING),
    )(*ins, *lands, send_sems, recv_sems, *after)
    me = 4 * lax.axis_index("x") + 2 * lax.axis_index("y") + lax.axis_index("c")
    out = []
    for src, landed in zip(res[:n], res[n:]):
        own = lax.dynamic_index_in_dim(src, me, axis=0, keepdims=True) if scatter else src[None]
        out.append(lax.dynamic_update_slice_in_dim(landed, own, me, axis=0))
    return out


def sum_slots(name, parts):
    _, rows, cols = parts.shape

    def body(p_ref, o_ref):
        s = p_ref[0]
        for k in range(1, NDEV):
            s = s + p_ref[k]
        o_ref[...] = s

    return pl.pallas_call(body, name=name, out_shape=jax.ShapeDtypeStruct((rows, cols), F32), compiler_params=_params())(parts)


def _heads(t):
    return t.reshape(T, B_HEADS, HD).transpose(1, 0, 2)


def _unheads(t):
    return t.transpose(1, 0, 2).reshape(T, B_HEADS * HD)


def _cols_from_slots(g):
    return g.transpose(1, 0, 2).reshape(g.shape[1], NDEV * g.shape[2])


def _slots_from_cols(w):
    return w.reshape(w.shape[0], NDEV, w.shape[1] // NDEV).transpose(1, 0, 2)


def _pack(arrays, rows):
    flat = jnp.concatenate([a.reshape(-1).astype(F32) for a in arrays])
    return jnp.pad(flat, (0, rows * LANES - flat.shape[0])).reshape(rows, LANES)


def _unpack(buf, shapes):
    flat = buf.reshape(-1)
    out, pos = [], 0
    for sh in shapes:
        size = math.prod(sh)
        out.append(flat[pos:pos + size].reshape(sh))
        pos += size
    return out


def kernel(x, norm_gains, w_qkv_a, w_o_a, w_q_b, w_o_b, kv_norm, w_kvf, b_f, w_up, conv_w, conv_b, w_down, loss_target, m_norm_gains, m_w_qkv_a, m_w_o_a, m_w_q_b, m_w_o_b, m_kv_norm, m_w_kvf, m_b_f, m_w_up, m_conv_w, m_conv_b, m_w_down, v_norm_gains, v_w_qkv_a, v_w_o_a, v_w_q_b, v_w_o_b, v_kv_norm, v_w_kvf, v_b_f, v_w_up, v_conv_w, v_conv_b, v_w_down):
    me = 4 * lax.axis_index("x") + 2 * lax.axis_index("y") + lax.axis_index("c")
    n_b = DEPTH - N_A

    def bf(a):
        return a.astype(BF16)

    mixer_w = [[bf(w_qkv_a[l]), bf(w_o_a[l])] if l < N_A else [bf(w_q_b[l - N_A]), bf(w_o_b[l - N_A])] for l in range(DEPTH)]
    ffn_w = [[bf(w_up[l]), bf(w_down[l])] for l in range(DEPTH)]
    mixer_w[0] += [norm_gains, conv_w]
    mixer_w[N_A] += [bf(w_kvf)]
    handles, tok = [], None
    for l in range(DEPTH):
        for part, items in (("mixer", mixer_w[l]), ("ffn", ffn_w[l])):
            hd, tok = exchange_start(f"gather_start_{part}{l}", items, False, dep=tok)
            handles.append(hd)

    wqkv, woa, wqb, wob, wup, wdown = {}, {}, {}, {}, {}, {}

    def take_mixer(l, arrived):
        if l < N_A:
            wqkv[l] = _cols_from_slots(arrived[0])
            woa[l] = _cols_from_slots(arrived[1])
        else:
            wqb[l - N_A] = arrived[0].reshape(D, D)
            wob[l - N_A] = arrived[1].reshape(D, D)

    def take_ffn(l, arrived):
        wup[l] = arrived[0]
        wdown[l] = arrived[1].reshape(4, SLAB, D)

    arrived = exchange_wait("gather_wait_mixer0", handles[0], tok)
    take_mixer(0, arrived)
    gains = arrived[2].transpose(1, 2, 0, 3).reshape(DEPTH, 4, D)
    cws = [arrived[3][:, l] for l in range(DEPTH)]
    cbs = [conv_b[l].reshape(NDEV, 1, SLAB) for l in range(DEPTH)]
    tables = rope_tables()
    b_col = b_f.reshape(B_HEADS, 1)

    h = x.reshape(T, D)
    _, (xn,) = resid_norm("norm_in", h, None, None, [gains[0, 0]])
    saved = []
    shared = None
    for l in range(DEPTH):
        s = {"h": h, "xn": xn}
        if l >= 1:
            arrived = exchange_wait(f"gather_wait_mixer{l}", handles[2 * l], h)
            take_mixer(l, arrived[:2])
            if l == N_A:
                wkvf = jnp.pad(_cols_from_slots(arrived[2]), ((0, 0), (0, KVF_PAD - KVF)))
        if l < N_A:
            qkv = matmul(f"qkv{l}", xn, wqkv[l], tm=1024, tn=768)
            s["qkvr"] = rope(f"rope{l}", qkv, tables, inverse=False)
            os_, lses = [], []
            for g, (_, r) in enumerate(A_GROUPS):
                o, lse = attn_a_fwd(f"attn_a{l}_{g}", s["qkvr"], g, r)
                os_.append(o)
                lses.append(lse)
            s["o"], s["lse"] = os_, lses
            s["oc"] = combine_fwd(f"combine{l}", os_, lses)
            mix = matmul(f"wo_a{l}", s["oc"], woa[l], tm=1024, tn=512)
        else:
            if l == N_A:
                kvf = matmul("kvf", xkv, wkvf, tm=1024, tn=768)
                k_h = _heads(kvf[:, :D].astype(BF16))
                v_h = _heads(kvf[:, D:2 * D].astype(BF16))
                fgt = kvf[:, 2 * D:KVF].T
                ct = gates_fwd("gates", fgt, b_col).reshape(B_HEADS, T // TK, TK)
                shared = {"xkv": xkv, "k": k_h, "v": v_h, "fgt": fgt, "ct": ct, "h": h}
            j = l - N_A
            s["q"] = _heads(matmul(f"wq_b{j}", xn, wqb[j], out_dtype=BF16, tm=1024, tn=512))
            s["o"], s["lse"] = fox_fwd(f"fox{j}", s["q"], shared["k"], shared["v"], shared["ct"])
            s["oc"] = _unheads(s["o"])
            mix = matmul(f"wo_b{j}", s["oc"], wob[j], tm=1024, tn=512)
        s["mix"] = mix
        take_ffn(l, exchange_wait(f"gather_wait_ffn{l}", handles[2 * l + 1], mix))
        s["h1"], (s["xn2"],) = resid_norm(f"norm_mid{l}", h, mix, gains[l, 1], [gains[l, 2]])
        s["a"] = matmul(f"up{l}", s["xn2"], wup[l], tm=1024, tn=SLAB, batch="b_out")
        s["u"] = convglu_fwd(f"convglu{l}", s["a"], cws[l], cbs[l])
        s["f"] = matmul(f"down{l}", s["u"], wdown[l], tm=1024, tn=512, batch="reduce")
        nxt = [gains[l + 1, 0]] if l + 1 < DEPTH else []
        if l == N_A - 1:
            nxt.append(kv_norm)
        h, normed = resid_norm(f"norm_out{l}", s["h1"], s["f"], gains[l, 3], nxt)
        if l + 1 < DEPTH:
            xn = normed[0]
        if l == N_A - 1:
            xkv = normed[1]
        saved.append(s)

    dh, loss_part = loss_head("loss", h, loss_target.reshape(T, D))

    d_gains = [[None] * 4 for _ in range(DEPTH)]
    d_cw, d_cb = [None] * DEPTH, [None] * DEPTH
    gw = {"qkv": [None] * N_A, "oa": [None] * N_A, "qb": [None] * n_b, "ob": [None] * n_b, "up": [None] * DEPTH,
          "down": [None] * DEPTH}
    kv_acc = None
    d_rows = []
    sent = []
    tok = None

    def slots_rows(g):
        return g.reshape(NDEV, g.shape[0] // NDEV, g.shape[1])

    for l in reversed(range(DEPTH)):
        s = saved[l]
        df, d_gains[l][3] = rms_bwd(f"bwd_norm_out{l}", s["f"], gains[l, 3], dh, out_dtype=BF16, dep=tok)
        du = matmul(f"bwd_down_x{l}", df, wdown[l], tb=True, out_dtype=BF16, tm=1024, tn=SLAB, batch="b_out")
        gw["down"][l] = matmul(f"bwd_down_w{l}", s["u"], df, ta=True, out_dtype=BF16, tm=SLAB, tn=512, batch="a_out")
        dac, d_cw[l], d_cb[l] = convglu_bwd(f"bwd_convglu{l}", du, s["a"], cws[l], cbs[l])
        da = conv_bwd_input(f"bwd_conv{l}", dac, cws[l])
        dxn2 = matmul(f"bwd_up_x{l}", da, wup[l], tb=True, tm=1024, tn=512, batch="reduce")
        gw["up"][l] = matmul(f"bwd_up_w{l}", s["xn2"], da, ta=True, out_dtype=BF16, tm=512, tn=SLAB, batch="b_out")
        dh1, d_gains[l][2] = rms_bwd(f"bwd_norm_mid{l}", s["h1"], gains[l, 2], dxn2, add=dh)
        ffn_items = [gw["up"][l], slots_rows(gw["down"][l].reshape(DFF, D))]
        ffn_what = [("up", l), ("down", l)]
        tok = None
        if l == 0:
            hd, tok = exchange_start("scatter_start_ffn0", ffn_items, True)
            sent.append((hd, ffn_what))
            ffn_items, ffn_what = [], []
        dmix, d_gains[l][1] = rms_bwd(f"bwd_norm_mix{l}", s["mix"], gains[l, 1], dh1, out_dtype=BF16, dep=tok)
        if l < N_A:
            doc = matmul(f"bwd_wo_a_x{l}", dmix, woa[l], tb=True, tm=1024, tn=A_W)
            gw["oa"][l] = matmul(f"bwd_wo_a_w{l}", s["oc"], dmix, ta=True, out_dtype=BF16, tm=A_W, tn=512)
            dos, dds = combine_bwd(f"bwd_combine{l}", doc, s["o"], s["lse"])
            cols = [None] * 9
            for g, (_, r) in enumerate(A_GROUPS):
                dq, dk, dv = attn_a_bwd(f"bwd_attn_a{l}_{g}", s["qkvr"], dos[g], s["lse"][g], dds[g], g, r)
                cols[g], cols[3 + g], cols[6 + g] = dq, dk, dv
            dqkv = rope(f"bwd_rope{l}", jnp.concatenate(cols, axis=1), tables, inverse=True)
            dxn = matmul(f"bwd_qkv_x{l}", dqkv, wqkv[l], tb=True, tm=1024, tn=512)
            gw["qkv"][l] = matmul(f"bwd_qkv_w{l}", s["xn"], dqkv, ta=True, out_dtype=BF16, tm=512, tn=768)
        else:
            j = l - N_A
            do = _heads(matmul(f"bwd_wo_b_x{j}", dmix, wob[j], tb=True, out_dtype=BF16, tm=1024, tn=512))
            gw["ob"][j] = matmul(f"bwd_wo_b_w{j}", s["oc"], dmix, ta=True, out_dtype=BF16, tm=512, tn=512)
            args = (s["q"], shared["k"], shared["v"], shared["ct"], s["o"], do, s["lse"])
            dq_h, drow = fox_bwd_q(f"bwd_fox_q{j}", *args)
            dq = _unheads(dq_h)
            d_rows.append(drow[:, :, 0])
            kv_acc = fox_bwd_kv(f"bwd_fox_kv{j}", *args, kv_acc)
            dxn = matmul(f"bwd_wq_b_x{j}", dq, wqb[j], tb=True, tm=1024, tn=512)
            gw["qb"][j] = matmul(f"bwd_wq_b_w{j}", s["xn"], dq, ta=True, out_dtype=BF16, tm=512, tn=512)
        dh, d_gains[l][0] = rms_bwd(f"bwd_norm_in{l}", s["h"], gains[l, 0], dxn, add=dh1)
        if l == N_A:
            dk_h, dv_h, dct = kv_acc
            dfgt, d_bf = gates_bwd("bwd_gates", shared["fgt"], b_col, [dct.reshape(B_HEADS, T)] + d_rows)
            dkvf = jnp.concatenate(
                [_unheads(dk_h).astype(BF16), _unheads(dv_h).astype(BF16), dfgt.T.astype(BF16),
                 jnp.zeros((T, KVF_PAD - KVF), BF16)], axis=1)
            dxkv = matmul("bwd_kvf_x", dkvf, wkvf, tb=True, tm=1024, tn=512)
            g_kvf_full = matmul("bwd_kvf_w", shared["xkv"], dkvf, ta=True, out_dtype=BF16, tm=512, tn=768)
            dh, d_kvn = rms_bwd("bwd_norm_kv", shared["h"], kv_norm, dxkv, add=dh)
        if l < N_A:
            items = [_slots_from_cols(gw["qkv"][l]), _slots_from_cols(gw["oa"][l])]
            what = [("qkv", l), ("oa", l)]
        else:
            items = [slots_rows(gw["qb"][l - N_A]), slots_rows(gw["ob"][l - N_A])]
            what = [("qb", l - N_A), ("ob", l - N_A)]
        if l == N_A:
            items.append(_slots_from_cols(g_kvf_full[:, :KVF]))
            what.append(("kvf", 0))
        hd, tok = exchange_start(f"scatter_start{l}", items + ffn_items, True)
        sent.append((hd, what + ffn_what))

    small_shapes = [(DEPTH, 4, D), (D,), (B_HEADS,), (DEPTH, 3, NDEV * SLAB), (DEPTH, NDEV * SLAB), (1,)]
    small = [
        jnp.stack([jnp.concatenate(row, axis=0) for row in d_gains]),
        d_kvn, d_bf,
        jnp.stack([d.transpose(1, 0, 2).reshape(3, NDEV * SLAB) for d in d_cw]),
        jnp.stack([d.reshape(NDEV * SLAB) for d in d_cb]),
        loss_part + tok[0:1, 0:1],
    ]
    small_rows = 848
    (small_all,) = exchange("gather_small_grads", [_pack(small, small_rows)], scatter=False)
    g_gains_full, g_kvn, g_bf, g_cw_full, g_cb, loss = _unpack(sum_slots("sum_small", small_all), small_shapes)
    g_gains_mine = lax.dynamic_slice_in_dim(g_gains_full, me * (D // NDEV), D // NDEV, axis=2)
    g_cw_mine = lax.dynamic_slice_in_dim(g_cw_full, me * SLAB, SLAB, axis=2)

    small_w = [norm_gains, kv_norm, b_f, conv_w, conv_b]
    small_m = [m_norm_gains, m_kv_norm, m_b_f, m_conv_w, m_conv_b]
    small_v = [v_norm_gains, v_kv_norm, v_b_f, v_conv_w, v_conv_b]
    small_g = [g_gains_mine, g_kvn, g_bf, g_cw_mine, g_cb]
    shapes = [w.shape for w in small_w]
    rows = 320
    res = adamw("adamw_small", _pack(small_g, rows)[None], _pack(small_w, rows), _pack(small_m, rows), _pack(small_v, rows))
    _, s_delta, s_m, s_v = [_unpack(r, shapes) for r in res]

    big = {"qkv": (w_qkv_a, m_w_qkv_a, v_w_qkv_a), "oa": (w_o_a, m_w_o_a, v_w_o_a), "qb": (w_q_b, m_w_q_b, v_w_q_b),
           "ob": (w_o_b, m_w_o_b, v_w_o_b), "kvf": (w_kvf, m_w_kvf, v_w_kvf), "up": (w_up, m_w_up, v_w_up),
           "down": (w_down, m_w_down, v_w_down)}
    updated = {name: None for name in big}
    after = [s_delta[0]]
    for handle, what in sent:
        received = exchange_wait(f"scatter_wait_{what[0][0]}{what[0][1]}", handle, after)
        for (name, layer), rec in zip(what, received):
            cols = rec.shape[-1]
            rows = rec.size // (NDEV * cols)
            w, m, v = (a.reshape(-1, cols) for a in big[name])
            updated[name] = adamw(f"adamw_{name}{layer}", rec.reshape(NDEV, rows, cols), w, m, v, row0=layer * rows,
                                  prev=updated[name])
        after = [updated[name][0] for name, _ in what]
    big_out = [[r.reshape(big[name][0].shape) for r in updated[name]] for name in ("qkv", "oa", "qb", "ob", "kvf", "up", "down")]

    def pick(k):
        b = [o[k] for o in big_out]
        sm = {0: small_g, 1: s_delta, 2: s_m, 3: s_v}[k]
        return [sm[0], b[0], b[1], b[2], b[3], sm[1], b[4], sm[2], b[5], sm[3], sm[4], b[6]]

    return (loss.reshape(()), dh.reshape(1, T, D), *pick(0), *pick(1), *pick(2), *pick(3))
```

```python
import functools
import math

import jax
import jax.numpy as jnp
from jax import lax
from jax.experimental import pallas as pl
from jax.experimental.pallas import tpu as pltpu

F32 = jnp.float32
BF16 = jnp.bfloat16

T = 2048
D = 1024
DEPTH = 4
N_A = 2
HD = 64
A_GROUPS = ((128, 1), (512, 4), (2048, 16))
A_W = 768
B_HEADS = 16
DFF = 2816
NDEV = 8
SLAB = 2 * DFF // NDEV
KVF = 2 * D + B_HEADS
KVF_PAD = 2304
ROPE_DIM = 16
ROPE_THETA = 500000.0
EPS = 1e-6
NEG = -1e30
BLK = 128
LANES = 128
SCALE = HD ** -0.5
VMEM_LIMIT = 56 * 1024 * 1024

ADAM_LR = 0.001
ADAM_B1 = 0.9
ADAM_B2 = 0.999
ADAM_EPS = 1e-08
ADAM_WD = 0.01
ADAM_STEP = 10
HIGHEST = lax.Precision.HIGHEST


def _params(*sem):
    return pltpu.CompilerParams(dimension_semantics=sem or None, vmem_limit_bytes=VMEM_LIMIT)


def _bf(x):
    return x if x.dtype == BF16 else x.astype(BF16)


def matmul(name, a, b, *, ta=False, tb=False, out_dtype=F32, tm=512, tn=512, batch=None):
    a_b = batch in ("a_out", "reduce")
    b_b = batch in ("b_out", "reduce")
    o_b = batch in ("a_out", "b_out")
    nb = a.shape[0] if a_b else (b.shape[0] if b_b else 1)
    ash = a.shape[1:] if a_b else a.shape
    bsh = b.shape[1:] if b_b else b.shape
    m, k = (ash[1], ash[0]) if ta else ash
    k2, n = (bsh[1], bsh[0]) if tb else bsh
    assert k == k2, (name, a.shape, b.shape)
    tm, tn = min(tm, m), min(tn, n)
    assert m % tm == 0 and n % tn == 0, (name, m, n, tm, tn)
    nbr = nb if batch == "reduce" else 1
    grid = (nb if o_b else 1, n // tn, m // tm, nbr)

    def bidx(bo, br):
        return bo if o_b else br

    def spec(batched, block, idx):
        if batched:
            return pl.BlockSpec((None,) + block, lambda bo, j, i, br: (bidx(bo, br),) + idx(i, j))
        return pl.BlockSpec(block, lambda bo, j, i, br: idx(i, j))

    a_spec = spec(a_b, (k, tm) if ta else (tm, k), (lambda i, j: (0, i)) if ta else (lambda i, j: (i, 0)))
    b_spec = spec(b_b, (tn, k) if tb else (k, tn), (lambda i, j: (j, 0)) if tb else (lambda i, j: (0, j)))
    o_spec = spec(o_b, (tm, tn), lambda i, j: (i, j))
    dims = (((0 if ta else 1,), (1 if tb else 0,)), ((), ()))

    def body(a_ref, b_ref, o_ref, *acc):
        p = lax.dot_general(_bf(a_ref[...]), _bf(b_ref[...]), dims, preferred_element_type=F32)
        if nbr == 1:
            o_ref[...] = p.astype(out_dtype)
        else:
            r = pl.program_id(3)

            @pl.when(r == 0)
            def _():
                acc[0][...] = p

            @pl.when(r > 0)
            def _():
                acc[0][...] += p

            @pl.when(r == nbr - 1)
            def _():
                o_ref[...] = acc[0][...].astype(out_dtype)

    out_shape = ((nb,) if o_b else ()) + (m, n)
    return pl.pallas_call(
        body, name=name, grid=grid, in_specs=[a_spec, b_spec], out_specs=o_spec,
        out_shape=jax.ShapeDtypeStruct(out_shape, out_dtype),
        scratch_shapes=[pltpu.VMEM((tm, tn), F32)] if nbr > 1 else [],
        compiler_params=_params("parallel", "parallel", "parallel", "arbitrary"),
    )(a, b)


def _rms(x, g):
    return x * lax.rsqrt(jnp.mean(x * x, axis=-1, keepdims=True) + EPS) * g


def resid_norm(name, h, y, gy, gains, tb=256, dep=None):
    n_g = len(gains)
    has_y = y is not None
    has_dep = dep is not None
    row = pl.BlockSpec((tb, D), lambda i: (i, 0))
    vec = pl.BlockSpec((1, D), lambda i: (0, 0))

    def body(*refs):
        h_ref = refs[0]
        pos = 1
        hn = h_ref[...]
        if has_y:
            hn = hn + _rms(refs[1][...], refs[2][...])
            pos = 3
        g_refs = refs[pos:pos + n_g]
        outs = refs[pos + n_g + has_dep:]
        if has_y:
            outs[0][...] = hn
            outs = outs[1:]
        for g_ref, o_ref in zip(g_refs, outs):
            o_ref[...] = _rms(hn, g_ref[...]).astype(BF16)

    ins = [h] + ([y, gy.reshape(1, D)] if has_y else []) + [g.reshape(1, D) for g in gains] + ([dep] if has_dep else [])
    in_specs = [row] + ([row, vec] if has_y else []) + [vec] * n_g + [pl.BlockSpec(memory_space=pl.ANY)] * has_dep
    out_shape = ([jax.ShapeDtypeStruct((T, D), F32)] if has_y else []) + [jax.ShapeDtypeStruct((T, D), BF16)] * n_g
    res = pl.pallas_call(
        body, name=name, grid=(T // tb,), in_specs=in_specs, out_specs=[row] * len(out_shape),
        out_shape=out_shape, compiler_params=_params("parallel"),
    )(*ins)
    return (res[0], list(res[1:])) if has_y else (h, list(res))


def rms_bwd(name, x, g, dy, add=None, out_dtype=F32, tb=256, dep=None):
    has_add = add is not None
    has_dep = dep is not None
    row = pl.BlockSpec((tb, D), lambda i: (i, 0))
    vec = pl.BlockSpec((1, D), lambda i: (0, 0))

    def body(*refs):
        x_ref, g_ref, dy_ref = refs[:3]
        dx_ref, dg_ref = refs[-2:]
        xv = x_ref[...]
        dyv = dy_ref[...].astype(F32)
        r = lax.rsqrt(jnp.mean(xv * xv, axis=-1, keepdims=True) + EPS)
        gdy = dyv * g_ref[...]
        dx = r * gdy - xv * (r * r * r * jnp.mean(xv * gdy, axis=-1, keepdims=True))
        if has_add:
            dx = dx + refs[3][...]
        dx_ref[...] = dx.astype(out_dtype)
        part = jnp.sum(dyv * xv * r, axis=0, keepdims=True)

        @pl.when(pl.program_id(0) == 0)
        def _():
            dg_ref[...] = part

        @pl.when(pl.program_id(0) > 0)
        def _():
            dg_ref[...] += part

    ins = [x, g.reshape(1, D), dy] + ([add] if has_add else []) + ([dep] if has_dep else [])
    return pl.pallas_call(
        body, name=name, grid=(T // tb,),
        in_specs=[row, vec, row] + ([row] if has_add else []) + [pl.BlockSpec(memory_space=pl.ANY)] * has_dep,
        out_specs=[row, vec],
        out_shape=[jax.ShapeDtypeStruct((T, D), out_dtype), jax.ShapeDtypeStruct((1, D), F32)],
        compiler_params=_params("arbitrary"),
    )(*ins)


def rope_tables():
    pos = jnp.arange(T, dtype=F32)
    inv = ROPE_THETA ** (-jnp.arange(0, ROPE_DIM, 2, dtype=F32) / ROPE_DIM)
    ang = pos[:, None] * inv[None, :]
    cos, sin = jnp.cos(ang), jnp.sin(ang)
    half = ROPE_DIM // 2
    one = jnp.ones((T, HD - ROPE_DIM), F32)
    zero = jnp.zeros((T, HD - ROPE_DIM), F32)
    zh = jnp.zeros((T, half), F32)
    c = jnp.concatenate([cos, cos, one], axis=1)
    s_up = jnp.concatenate([zh, sin, zero], axis=1)
    s_dn = jnp.concatenate([-sin, zh, zero], axis=1)
    rep = LANES // HD
    return tuple(jnp.tile(t, (1, rep)) for t in (c, s_up, s_dn))


def rope(name, t, tables, inverse, out_dtype=BF16, tb=256):
    c, s_up, s_dn = tables
    n_rot = 2 * A_W // LANES
    half = ROPE_DIM // 2
    blk = pl.BlockSpec((tb, 3 * A_W), lambda i: (i, 0))
    tab = pl.BlockSpec((tb, LANES), lambda i: (i, 0))

    def body(t_ref, c_ref, su_ref, sd_ref, o_ref):
        sgn = -1.0 if inverse else 1.0
        cos, up, dn = c_ref[...], sgn * su_ref[...], sgn * sd_ref[...]
        for j in range(n_rot):
            sl = slice(j * LANES, (j + 1) * LANES)
            x = t_ref[:, sl].astype(F32)
            o_ref[:, sl] = (x * cos + pltpu.roll(x, half, 1) * up + pltpu.roll(x, LANES - half, 1) * dn).astype(out_dtype)
        o_ref[:, 2 * A_W:] = t_ref[:, 2 * A_W:].astype(out_dtype)

    return pl.pallas_call(
        body, name=name, grid=(T // tb,), in_specs=[blk, tab, tab, tab], out_specs=blk,
        out_shape=jax.ShapeDtypeStruct((T, 3 * A_W), out_dtype), compiler_params=_params("parallel"),
    )(t, c, s_up, s_dn)


GW = 4 * HD


def _band_mask(b):
    qi = lax.broadcasted_iota(jnp.int32, (BLK, 2 * BLK), 0)
    kj = lax.broadcasted_iota(jnp.int32, (BLK, 2 * BLK), 1)
    return (kj <= qi + BLK) & (kj >= qi) & ((kj >= BLK) | (b > 0))


def attn_a_fwd(name, qkvr, g, r):
    length = T // r
    nblk = length // BLK
    view = qkvr.reshape(length, r * 3 * A_W)
    ncol = 3 * A_W // GW

    def col(section, prev):
        def idx(j, b):
            return (jnp.maximum(b - 1, 0) if prev else b, j * ncol + 3 * section + g)
        return pl.BlockSpec((BLK, GW), idx)

    out = pl.BlockSpec((BLK, GW), lambda j, b: (b, j))

    def body(q_ref, kp_ref, kc_ref, vp_ref, vc_ref, o_ref, lse_ref):
        mask = _band_mask(pl.program_id(1))
        k2 = jnp.concatenate([kp_ref[...], kc_ref[...]], axis=0)
        v2 = jnp.concatenate([vp_ref[...], vc_ref[...]], axis=0)
        q = q_ref[...]
        for h in range(4):
            sl = slice(h * HD, (h + 1) * HD)
            s = lax.dot_general(q[:, sl], k2[:, sl], (((1,), (1,)), ((), ())), preferred_element_type=F32) * SCALE
            s = jnp.where(mask, s, NEG)
            m = jnp.max(s, axis=1, keepdims=True)
            p = jnp.exp(s - m)
            l = jnp.sum(p, axis=1, keepdims=True)
            o_ref[:, sl] = jnp.dot((p / l).astype(BF16), v2[:, sl], preferred_element_type=F32)
            lse_ref[:, sl] = jnp.broadcast_to(m + jnp.log(l), (BLK, HD))

    shape = jax.ShapeDtypeStruct((length, r * GW), F32)
    o, lse = pl.pallas_call(
        body, name=name, grid=(r, nblk),
        in_specs=[col(0, False), col(1, True), col(1, False), col(2, True), col(2, False)],
        out_specs=[out, out], out_shape=[shape, shape], compiler_params=_params("parallel", "parallel"),
    )(view, view, view, view, view)
    return o.reshape(T, GW), lse.reshape(T, GW)


def attn_a_bwd(name, qkvr, do, lse, dd, g, r):
    length = T // r
    nblk = length // BLK
    view = qkvr.reshape(length, r * 3 * A_W)
    ncol = 3 * A_W // GW

    def col(section, shift):
        def idx(j, b):
            return (jnp.clip(b + shift, 0, nblk - 1), j * ncol + 3 * section + g)
        return pl.BlockSpec((BLK, GW), idx)

    def tok(shift):
        return pl.BlockSpec((BLK, GW), lambda j, b: (jnp.clip(b + shift, 0, nblk - 1), j))

    def body(q_ref, qn_ref, kp_ref, kc_ref, vp_ref, vc_ref, do_ref, don_ref, lse_ref, lsen_ref, dd_ref, ddn_ref,
             dq_ref, dk_ref, dv_ref):
        b = pl.program_id(1)
        mask = _band_mask(b)
        qi = lax.broadcasted_iota(jnp.int32, (2 * BLK, BLK), 0)
        kj = lax.broadcasted_iota(jnp.int32, (2 * BLK, BLK), 1)
        kmask = ((qi < BLK) & (kj <= qi)) | ((qi >= BLK) & (kj >= qi - BLK) & (b + 1 < nblk))
        k2 = jnp.concatenate([kp_ref[...], kc_ref[...]], axis=0)
        v2 = jnp.concatenate([vp_ref[...], vc_ref[...]], axis=0)
        q2 = jnp.concatenate([q_ref[...], qn_ref[...]], axis=0)
        do2 = jnp.concatenate([do_ref[...], don_ref[...]], axis=0)
        lse2 = jnp.concatenate([lse_ref[...], lsen_ref[...]], axis=0)
        dd2 = jnp.concatenate([dd_ref[...], ddn_ref[...]], axis=0)
        nt = (((1,), (1,)), ((), ()))
        tn = (((0,), (0,)), ((), ()))
        for h in range(4):
            sl = slice(h * HD, (h + 1) * HD)
            one = slice(h * HD, h * HD + 1)
            qh, kh, vh, doh = q2[:, sl], k2[:, sl], v2[:, sl], do2[:, sl]
            s = lax.dot_general(qh[:BLK], kh, nt, preferred_element_type=F32) * SCALE
            p = jnp.where(mask, jnp.exp(s - lse2[:BLK, one]), 0.0)
            dp = lax.dot_general(doh[:BLK], vh, nt, preferred_element_type=F32)
            ds = p * (dp + dd2[:BLK, one])
            dq_ref[:, sl] = jnp.dot(ds.astype(BF16), kh, preferred_element_type=F32) * SCALE
            kc, vc = kh[BLK:], vh[BLK:]
            s = lax.dot_general(qh, kc, nt, preferred_element_type=F32) * SCALE
            p = jnp.where(kmask, jnp.exp(s - lse2[:, one]), 0.0)
            dp = lax.dot_general(doh, vc, nt, preferred_element_type=F32)
            ds = p * (dp + dd2[:, one])
            dk_ref[:, sl] = lax.dot_general(ds.astype(BF16), qh, tn, preferred_element_type=F32) * SCALE
            dv_ref[:, sl] = lax.dot_general(p.astype(BF16), doh, tn, preferred_element_type=F32)

    dov = do.reshape(length, r * GW)
    lsev = lse.reshape(length, r * GW)
    ddv = dd.reshape(length, r * GW)
    shape = jax.ShapeDtypeStruct((length, r * GW), F32)
    dq, dk, dv = pl.pallas_call(
        body, name=name, grid=(r, nblk),
        in_specs=[col(0, 0), col(0, 1), col(1, -1), col(1, 0), col(2, -1), col(2, 0),
                  tok(0), tok(1), tok(0), tok(1), tok(0), tok(1)],
        out_specs=[tok(0)] * 3, out_shape=[shape] * 3, compiler_params=_params("parallel", "parallel"),
    )(view, view, view, view, view, view, dov, dov, lsev, lsev, ddv, ddv)
    return dq.reshape(T, GW), dk.reshape(T, GW), dv.reshape(T, GW)


def _head_sum(x):
    i = lax.div(lax.broadcasted_iota(jnp.int32, (GW, GW), 0), jnp.int32(HD))
    j = lax.div(lax.broadcasted_iota(jnp.int32, (GW, GW), 1), jnp.int32(HD))
    return jnp.dot(x, (i == j).astype(F32), precision=HIGHEST, preferred_element_type=F32)


def _alphas(lses):
    m = jnp.maximum(jnp.maximum(lses[0], lses[1]), lses[2])
    e = [jnp.exp(l - m) for l in lses]
    z = e[0] + e[1] + e[2]
    return [x / z for x in e]


def combine_fwd(name, os_, lses, tb=256):
    blk = pl.BlockSpec((tb, GW), lambda i: (i, 0))

    def body(o0, o1, o2, l0, l1, l2, oc_ref):
        al = _alphas([l0[...], l1[...], l2[...]])
        for g, o_ref in enumerate((o0, o1, o2)):
            oc_ref[:, g * GW:(g + 1) * GW] = (o_ref[...] * al[g]).astype(BF16)

    return pl.pallas_call(
        body, name=name, grid=(T // tb,), in_specs=[blk] * 6, out_specs=pl.BlockSpec((tb, A_W), lambda i: (i, 0)),
        out_shape=jax.ShapeDtypeStruct((T, A_W), BF16), compiler_params=_params("parallel"),
    )(*os_, *lses)


def combine_bwd(name, doc, os_, lses, tb=256):
    blk = pl.BlockSpec((tb, GW), lambda i: (i, 0))

    def body(doc_ref, o0, o1, o2, l0, l1, l2, d0, d1, d2, e0, e1, e2):
        al = _alphas([l0[...], l1[...], l2[...]])
        dal = [_head_sum(doc_ref[:, g * GW:(g + 1) * GW] * o_ref[...]) for g, o_ref in enumerate((o0, o1, o2))]
        mean = al[0] * dal[0] + al[1] * dal[1] + al[2] * dal[2]
        for g, (do_ref, dd_ref) in enumerate(((d0, e0), (d1, e1), (d2, e2))):
            do_ref[...] = (doc_ref[:, g * GW:(g + 1) * GW] * al[g]).astype(BF16)
            dd_ref[...] = al[g] * (dal[g] - mean) - al[g] * dal[g]

    res = pl.pallas_call(
        body, name=name, grid=(T // tb,), in_specs=[pl.BlockSpec((tb, A_W), lambda i: (i, 0))] + [blk] * 6,
        out_specs=[blk] * 6,
        out_shape=[jax.ShapeDtypeStruct((T, GW), BF16)] * 3 + [jax.ShapeDtypeStruct((T, GW), F32)] * 3,
        compiler_params=_params("parallel"),
    )(doc, *os_, *lses)
    return res[:3], res[3:]


TQ = 512
TK = TQ
FOX_HEADS = 1
NT = (((1,), (1,)), ((), ()))
TN = (((0,), (0,)), ((), ()))


def _fox_scores(q, kb, cj, row0, col0, masked):
    s = lax.dot_general(q, kb, NT, preferred_element_type=F32) * SCALE - cj
    if masked:
        qi = row0 + lax.broadcasted_iota(jnp.int32, s.shape, 0)
        kj = col0 + lax.broadcasted_iota(jnp.int32, s.shape, 1)
        s = jnp.where(kj <= qi, s, NEG)
    return s


def fox_fwd(name, q, k, v, ct):
    def body(q_ref, k_ref, v_ref, c_ref, o_ref, lse_ref):
        i = pl.program_id(1)
        n_full = i

        def step(jb, carry, masked):
            start = pl.multiple_of(jb * TK, TK)
            out = []
            for h in range(FOX_HEADS):
                m, l, acc = carry[h]
                kb = k_ref[h, pl.ds(start, TK), :]
                vb = v_ref[h, pl.ds(start, TK), :]
                s = _fox_scores(q_ref[h], kb, c_ref[h, pl.ds(jb, 1), :], i * TQ, jb * TK, masked)
                m_new = jnp.maximum(m, jnp.max(s, axis=1, keepdims=True))
                a = jnp.exp(m - m_new)
                p = jnp.exp(s - m_new)
                l = a * l + jnp.sum(p, axis=1, keepdims=True)
                acc = a * acc + jnp.dot(p.astype(BF16), vb, preferred_element_type=F32)
                out.append((m_new, l, acc))
            return tuple(out)

        init = tuple((jnp.full((TQ, 1), NEG, F32), jnp.zeros((TQ, 1), F32), jnp.zeros((TQ, HD), F32))
                     for _ in range(FOX_HEADS))
        carry = lax.fori_loop(0, n_full, lambda jb, c: step(jb, c, False), init)
        carry = step(n_full, carry, True)
        for h in range(FOX_HEADS):
            m, l, acc = carry[h]
            o_ref[h] = (acc / l).astype(BF16)
            lse_ref[h] = jnp.broadcast_to(m + jnp.log(l), (TQ, LANES))

    head = lambda h, i: (h, 0, 0)
    return pl.pallas_call(
        body, name=name, grid=(B_HEADS // FOX_HEADS, T // TQ),
        in_specs=[pl.BlockSpec((FOX_HEADS, TQ, HD), lambda h, i: (h, i, 0)), pl.BlockSpec((FOX_HEADS, T, HD), head),
                  pl.BlockSpec((FOX_HEADS, T, HD), head), pl.BlockSpec((FOX_HEADS, T // TK, TK), head)],
        out_specs=[pl.BlockSpec((FOX_HEADS, TQ, HD), lambda h, i: (h, i, 0)),
                   pl.BlockSpec((FOX_HEADS, TQ, LANES), lambda h, i: (h, i, 0))],
        out_shape=[jax.ShapeDtypeStruct((B_HEADS, T, HD), BF16), jax.ShapeDtypeStruct((B_HEADS, T, LANES), F32)],
        compiler_params=_params("parallel", "parallel"),
    )(q, k, v, ct)


def fox_bwd_q(name, q, k, v, ct, o, do, lse):
    def body(q_ref, k_ref, v_ref, c_ref, o_ref, do_ref, lse_ref, dq_ref, drow_ref):
        i = pl.program_id(1)
        n_full = i
        delta = [jnp.sum(do_ref[h].astype(F32) * o_ref[h].astype(F32), axis=1, keepdims=True)
                 for h in range(FOX_HEADS)]

        def step(jb, carry, masked):
            start = pl.multiple_of(jb * TK, TK)
            out = []
            for h in range(FOX_HEADS):
                dq, drow = carry[h]
                kb = k_ref[h, pl.ds(start, TK), :]
                vb = v_ref[h, pl.ds(start, TK), :]
                s = _fox_scores(q_ref[h], kb, c_ref[h, pl.ds(jb, 1), :], i * TQ, jb * TK, masked)
                p = jnp.exp(s - lse_ref[h, :, 0:1])
                dp = lax.dot_general(do_ref[h], vb, NT, preferred_element_type=F32)
                ds = p * (dp - delta[h])
                out.append((dq + jnp.dot(ds.astype(BF16), kb, preferred_element_type=F32),
                            drow + jnp.sum(ds, axis=1, keepdims=True)))
            return tuple(out)

        init = tuple((jnp.zeros((TQ, HD), F32), jnp.zeros((TQ, 1), F32)) for _ in range(FOX_HEADS))
        carry = lax.fori_loop(0, n_full, lambda jb, c: step(jb, c, False), init)
        carry = step(n_full, carry, True)
        for h in range(FOX_HEADS):
            dq, drow = carry[h]
            dq_ref[h] = (dq * SCALE).astype(BF16)
            drow_ref[h, pl.ds(i, 1), :] = jnp.transpose(jnp.broadcast_to(drow, (TQ, LANES)))[0:1, :]

    head = lambda h, i: (h, 0, 0)
    blk = pl.BlockSpec((FOX_HEADS, TQ, HD), lambda h, i: (h, i, 0))
    stat = pl.BlockSpec((FOX_HEADS, TQ, LANES), lambda h, i: (h, i, 0))
    rows = pl.BlockSpec((FOX_HEADS, T // TQ, TQ), head)
    return pl.pallas_call(
        body, name=name, grid=(B_HEADS // FOX_HEADS, T // TQ),
        in_specs=[blk, pl.BlockSpec((FOX_HEADS, T, HD), head), pl.BlockSpec((FOX_HEADS, T, HD), head),
                  pl.BlockSpec((FOX_HEADS, T // TK, TK), head), blk, blk, stat],
        out_specs=[blk, rows],
        out_shape=[jax.ShapeDtypeStruct((B_HEADS, T, HD), BF16), jax.ShapeDtypeStruct((B_HEADS, T // TQ, TQ), F32)],
        compiler_params=_params("parallel", "arbitrary"),
    )(q, k, v, ct, o, do, lse)


def fox_bwd_kv(name, q, k, v, ct, o, do, lse, prev):
    has_prev = prev is not None
    nq = T // TQ

    def body(*refs):
        q_ref, k_ref, v_ref, c_ref, o_ref, do_ref, lse_ref = refs[:7]
        dk_ref, dv_ref, dc_ref = refs[-3:]
        jb = pl.program_id(1)
        dk_ref[...] = jnp.zeros_like(dk_ref)
        dv_ref[...] = jnp.zeros_like(dv_ref)

        def step(i, dcs, masked):
            start = pl.multiple_of(i * TQ, TQ)
            out = []
            for h in range(FOX_HEADS):
                qv = q_ref[h, pl.ds(start, TQ), :]
                dov = do_ref[h, pl.ds(start, TQ), :]
                ov = o_ref[h, pl.ds(start, TQ), :]
                lsev = lse_ref[h, pl.ds(start, TQ), 0:1]
                delta = jnp.sum(dov.astype(F32) * ov.astype(F32), axis=1, keepdims=True)
                s = _fox_scores(qv, k_ref[h], c_ref[h, pl.ds(jb, 1), :], i * TQ, jb * TK, masked)
                p = jnp.exp(s - lsev)
                dp = lax.dot_general(dov, v_ref[h], NT, preferred_element_type=F32)
                ds = p * (dp - delta)
                dv_ref[h] += lax.dot_general(p.astype(BF16), dov, TN, preferred_element_type=F32)
                dk_ref[h] += lax.dot_general(ds.astype(BF16), qv, TN, preferred_element_type=F32)
                out.append(dcs[h] - jnp.sum(ds, axis=0, keepdims=True))
            return tuple(out)

        dcs = tuple(jnp.zeros((1, TK), F32) for _ in range(FOX_HEADS))
        dcs = step(jb, dcs, True)
        dcs = lax.fori_loop(jb + 1, nq, lambda i, c: step(i, c, False), dcs)
        for h in range(FOX_HEADS):
            dk = dk_ref[h] * SCALE
            dc = dcs[h]
            if has_prev:
                dk = dk + refs[7][h]
                dv_ref[h] += refs[8][h]
                dc = dc + refs[9][h, pl.ds(jb, 1), :]
            dk_ref[h] = dk
            dc_ref[h, pl.ds(jb, 1), :] = dc

    head = lambda h, j: (h, 0, 0)
    full = pl.BlockSpec((FOX_HEADS, T, HD), head)
    blk = pl.BlockSpec((FOX_HEADS, TK, HD), lambda h, j: (h, j, 0))
    cspec = pl.BlockSpec((FOX_HEADS, T // TK, TK), head)
    ins = [q, k, v, ct, o, do, lse] + (list(prev) if has_prev else [])
    in_specs = [full, blk, blk, cspec, full, full, pl.BlockSpec((FOX_HEADS, T, LANES), head)] + ([blk, blk, cspec] if has_prev else [])
    return pl.pallas_call(
        body, name=name, grid=(B_HEADS // FOX_HEADS, T // TK), in_specs=in_specs, out_specs=[blk, blk, cspec],
        out_shape=[jax.ShapeDtypeStruct((B_HEADS, T, HD), F32)] * 2 + [jax.ShapeDtypeStruct((B_HEADS, T // TK, TK), F32)],
        compiler_params=_params("parallel", "arbitrary"),
    )(*ins)


def _tri(upper):
    i = lax.broadcasted_iota(jnp.int32, (BLK, BLK), 0)
    j = lax.broadcasted_iota(jnp.int32, (BLK, BLK), 1)
    return ((i <= j) if upper else (i >= j)).astype(F32)


def gates_fwd(name, fgt, b_f):
    def body(f_ref, b_ref, c_ref):
        tri = _tri(True)
        carry = jnp.zeros((B_HEADS, 1), F32)
        for blk in range(T // BLK):
            sl = slice(blk * BLK, (blk + 1) * BLK)
            z = f_ref[:, sl] + b_ref[...]
            logf = jnp.minimum(z, 0.0) - jnp.log(1.0 + jnp.exp(-jnp.abs(z)))
            cs = jnp.dot(logf, tri, precision=HIGHEST, preferred_element_type=F32) + carry
            c_ref[:, sl] = cs
            carry = cs[:, BLK - 1:BLK]

    return pl.pallas_call(
        body, name=name, out_shape=jax.ShapeDtypeStruct((B_HEADS, T), F32), compiler_params=_params(),
    )(fgt, b_f)


def gates_bwd(name, fgt, b_f, dcs):
    n_dc = len(dcs)

    def body(*refs):
        f_ref, b_ref = refs[:2]
        dc_refs = refs[2:2 + n_dc]
        dz_ref, db_ref = refs[2 + n_dc:]
        tri = _tri(False)
        carry = jnp.zeros((B_HEADS, 1), F32)
        db = jnp.zeros((B_HEADS, 1), F32)
        for blk in reversed(range(T // BLK)):
            sl = slice(blk * BLK, (blk + 1) * BLK)
            dc = dc_refs[0][:, sl]
            for r in dc_refs[1:]:
                dc = dc + r[:, sl]
            rc = jnp.dot(dc, tri, precision=HIGHEST, preferred_element_type=F32) + carry
            carry = rc[:, 0:1]
            z = f_ref[:, sl] + b_ref[...]
            e = jnp.exp(-jnp.abs(z))
            dz = rc * jnp.where(z >= 0.0, e, 1.0) / (1.0 + e)
            dz_ref[:, sl] = dz
            db = db + jnp.sum(dz, axis=1, keepdims=True)
        db_ref[...] = db

    return pl.pallas_call(
        body, name=name,
        out_shape=[jax.ShapeDtypeStruct((B_HEADS, T), F32), jax.ShapeDtypeStruct((B_HEADS, 1), F32)],
        compiler_params=_params(),
    )(fgt, b_f, *dcs)


CONV_TB = 256
GELU_K = math.sqrt(2.0 / math.pi)
GELU_C = 0.044715


def _shift_down(x, halo_ref, n):
    rows = lax.broadcasted_iota(jnp.int32, x.shape, 0)
    y = pltpu.roll(x, n, 0)
    for k in range(n):
        y = jnp.where(rows == k, halo_ref[pl.ds(8 - n + k, 1), :], y)
    return y


def _conv(x, halo_ref, cw_ref, cb_ref, first):
    x1 = _shift_down(x, halo_ref, 1)
    x2 = _shift_down(x, halo_ref, 2)
    rows = lax.broadcasted_iota(jnp.int32, x.shape, 0)
    x1 = jnp.where(first & (rows < 1), 0.0, x1)
    x2 = jnp.where(first & (rows < 2), 0.0, x2)
    y = x2 * cw_ref[0:1, :] + x1 * cw_ref[1:2, :] + x * cw_ref[2:3, :] + cb_ref[...]
    return y, x1, x2


def _gelu_parts(x):
    th = jnp.tanh(GELU_K * (x + GELU_C * x * x * x))
    val = 0.5 * x * (1.0 + th)
    grad = 0.5 * (1.0 + th) + 0.5 * x * (1.0 - th * th) * GELU_K * (1.0 + 3.0 * GELU_C * x * x)
    return val, grad


def _conv_specs(tb):
    def slab(off):
        return pl.BlockSpec((None, tb, SLAB), lambda d, i: (d + off, i, 0))

    def halo(off):
        return pl.BlockSpec((None, 8, SLAB), lambda d, i: (d + off, jnp.maximum(i * (tb // 8) - 1, 0), 0))

    def par(rows, off):
        return pl.BlockSpec((None, rows, SLAB), lambda d, i: (d + off, 0, 0))

    return slab, halo, par


def convglu_fwd(name, a, cw, cb, tb=CONV_TB):
    slab, halo, par = _conv_specs(tb)

    def body(ag, hg, av, hv, cwg, cbg, cwv, cbv, u_ref):
        first = pl.program_id(1) == 0
        gate, _, _ = _conv(ag[...], hg, cwg, cbg, first)
        val, _, _ = _conv(av[...], hv, cwv, cbv, first)
        u_ref[...] = (_gelu_parts(gate)[0] * val).astype(BF16)

    return pl.pallas_call(
        body, name=name, grid=(4, T // tb),
        in_specs=[slab(0), halo(0), slab(4), halo(4), par(3, 0), par(1, 0), par(3, 4), par(1, 4)],
        out_specs=slab(0), out_shape=jax.ShapeDtypeStruct((4, T, SLAB), BF16),
        compiler_params=_params("parallel", "parallel"),
    )(a, a, a, a, cw, cb, cw, cb)


def convglu_bwd(name, du, a, cw, cb, tb=CONV_TB):
    nblk = T // tb

    def rev(i):
        return nblk - 1 - i

    def pair(rows, idx):
        return pl.BlockSpec((2, None, rows, SLAB), lambda d, i: (0, d, idx(i), 0))

    def body(du_ref, a_ref, h_ref, cw_ref, cb_ref, da_ref, dcw_ref, dcb_ref, nxt_ref):
        step = pl.program_id(1)
        first = step == nblk - 1

        @pl.when(step == 0)
        def _():
            nxt_ref[...] = jnp.zeros_like(nxt_ref)

        gate, g1, g2 = _conv(a_ref[0], h_ref.at[0], cw_ref.at[0], cb_ref.at[0], first)
        val, v1, v2 = _conv(a_ref[1], h_ref.at[1], cw_ref.at[1], cb_ref.at[1], first)
        act, dact = _gelu_parts(gate)
        duv = du_ref[...].astype(F32)
        dys = (duv * val * dact, duv * act)
        rows = lax.broadcasted_iota(jnp.int32, duv.shape, 0)
        for k, (dy, xs) in enumerate(((dys[0], (g2, g1, a_ref[0])), (dys[1], (v2, v1, a_ref[1])))):
            def up(n):
                y = pltpu.roll(dy, tb - n, 0)
                for r in range(n):
                    y = jnp.where(rows == tb - n + r, nxt_ref[k, pl.ds(r, 1), :], y)
                return y

            da_ref[k] = (dy * cw_ref[k, 2:3, :] + up(1) * cw_ref[k, 1:2, :] + up(2) * cw_ref[k, 0:1, :]).astype(BF16)
            parts = [jnp.sum(dy * x, axis=0, keepdims=True) for x in xs]
            bias = jnp.sum(dy, axis=0, keepdims=True)

            @pl.when(step == 0)
            def _():
                for r in range(3):
                    dcw_ref[k, r:r + 1, :] = parts[r]
                dcb_ref[k] = bias

            @pl.when(step > 0)
            def _():
                for r in range(3):
                    dcw_ref[k, r:r + 1, :] += parts[r]
                dcb_ref[k] += bias
        for k in range(2):
            nxt_ref[k] = dys[k][0:8]

    da, dcw, dcb = pl.pallas_call(
        body, name=name, grid=(4, nblk),
        in_specs=[pl.BlockSpec((None, tb, SLAB), lambda d, i: (d, rev(i), 0)), pair(tb, rev),
                  pair(8, lambda i: jnp.maximum(rev(i) * (tb // 8) - 1, 0)), pair(3, lambda i: 0), pair(1, lambda i: 0)],
        out_specs=[pair(tb, rev), pair(3, lambda i: 0), pair(1, lambda i: 0)],
        out_shape=[jax.ShapeDtypeStruct((2, 4, T, SLAB), BF16), jax.ShapeDtypeStruct((2, 4, 3, SLAB), F32),
                   jax.ShapeDtypeStruct((2, 4, 1, SLAB), F32)],
        scratch_shapes=[pltpu.VMEM((2, 8, SLAB), F32)],
        compiler_params=_params("parallel", "arbitrary"),
    )(du, a.reshape(2, 4, T, SLAB), a.reshape(2, 4, T, SLAB), cw.reshape(2, 4, 3, SLAB), cb.reshape(2, 4, 1, SLAB))
    return da.reshape(NDEV, T, SLAB), dcw.reshape(NDEV, 3, SLAB), dcb.reshape(NDEV, 1, SLAB)


def loss_head(name, y, target, tb=256):
    row = pl.BlockSpec((tb, D), lambda i: (i, 0))

    def body(y_ref, t_ref, dy_ref, loss_ref):
        diff = y_ref[...] - t_ref[...]
        dy_ref[...] = diff * (1.0 / D)
        part = jnp.sum(jnp.sum(diff * diff, axis=1, keepdims=True), axis=0, keepdims=True) * (0.5 / D)

        @pl.when(pl.program_id(0) == 0)
        def _():
            loss_ref[...] = part

        @pl.when(pl.program_id(0) > 0)
        def _():
            loss_ref[...] += part

    return pl.pallas_call(
        body, name=name, grid=(T // tb,), in_specs=[row, row],
        out_specs=[row, pl.BlockSpec((1, 1), lambda i: (0, 0))],
        out_shape=[jax.ShapeDtypeStruct((T, D), F32), jax.ShapeDtypeStruct((1, 1), F32)],
        compiler_params=_params("arbitrary"),
    )(y, target)


def _row_tile(rows, cols, bytes_per_elem, budget=6 * 1024 * 1024):
    for tr in (1024, 512, 256, 128, 64, 32, 16, 8):
        if rows % tr == 0 and tr * cols * bytes_per_elem <= budget:
            return tr
    return rows


def adamw(name, parts, w, m, v, row0=0, prev=None):
    n_parts, rows, cols = parts.shape
    rows_all = w.shape[0]
    tr = _row_tile(math.gcd(rows, row0) if row0 else rows, cols, n_parts * parts.dtype.itemsize + 28)
    blk = pl.BlockSpec((tr, cols), lambda i: (row0 // tr + i, 0))
    b1c = 1.0 - ADAM_B1 ** ADAM_STEP
    b2c = 1.0 - ADAM_B2 ** ADAM_STEP
    n_prev = 0 if prev is None else 4

    def body(p_ref, w_ref, m_ref, v_ref, *rest):
        g_ref, d_ref, nm_ref, nv_ref = rest[n_prev:]
        g = p_ref[0].astype(F32)
        for k in range(1, n_parts):
            g = g + p_ref[k].astype(F32)
        nm = ADAM_B1 * m_ref[...] + (1.0 - ADAM_B1) * g
        nv = ADAM_B2 * v_ref[...] + (1.0 - ADAM_B2) * (g * g)
        g_ref[...] = g
        nm_ref[...] = nm
        nv_ref[...] = nv
        d_ref[...] = -ADAM_LR * ((nm / b1c) / (jnp.sqrt(nv / b2c) + ADAM_EPS) + ADAM_WD * w_ref[...])

    return pl.pallas_call(
        body, name=name, grid=(rows // tr,),
        in_specs=[pl.BlockSpec((n_parts, tr, cols), lambda i: (0, i, 0)), blk, blk, blk]
        + [pl.BlockSpec(memory_space=pl.ANY)] * n_prev,
        out_specs=[blk] * 4, out_shape=[jax.ShapeDtypeStruct((rows_all, cols), F32)] * 4,
        input_output_aliases={4 + k: k for k in range(n_prev)}, compiler_params=_params("parallel"),
    )(parts, w, m, v, *(prev or []))


def exchange(name, items, scatter):
    n = len(items)
    hbm = pl.BlockSpec(memory_space=pltpu.HBM)

    def body(*refs):
        ins, outs = refs[:n], refs[n:2 * n]
        send_sems, recv_sems, local_sems = refs[2 * n:]
        x, y, c = lax.axis_index("x"), lax.axis_index("y"), lax.axis_index("c")
        me = 4 * x + 2 * y + c
        copies = []
        for t in range(n):
            own = pltpu.make_async_copy(ins[t].at[me] if scatter else ins[t], outs[t].at[me], local_sems.at[t])
            own.start()
            copies.append(own)
            for rel in range(1, NDEV):
                px = 1 - x if rel & 4 else x
                py = 1 - y if rel & 2 else y
                pc = 1 - c if rel & 1 else c
                src = ins[t].at[4 * px + 2 * py + pc] if scatter else ins[t]
                cp = pltpu.make_async_remote_copy(
                    src_ref=src, dst_ref=outs[t].at[me], send_sem=send_sems.at[t, rel - 1],
                    recv_sem=recv_sems.at[t, rel - 1], device_id=(px, py, pc), device_id_type=pl.DeviceIdType.MESH)
                cp.start()
                copies.append(cp)
        for cp in copies:
            cp.wait()

    out_shape = [jax.ShapeDtypeStruct(it.shape if scatter else (NDEV,) + it.shape, it.dtype) for it in items]
    return pl.pallas_call(
        body, name=name, in_specs=[hbm] * n, out_specs=[hbm] * n, out_shape=out_shape,
        scratch_shapes=[pltpu.SemaphoreType.DMA((n, NDEV - 1)), pltpu.SemaphoreType.DMA((n, NDEV - 1)),
                        pltpu.SemaphoreType.DMA((n,))],
    )(*items)


def _peer(rel, x, y, c):
    return (1 - x if rel & 4 else x, 1 - y if rel & 2 else y, 1 - c if rel & 1 else c)


def _split_copies(ins, lands, send_sems, recv_sems, scatter):
    x, y, c = lax.axis_index("x"), lax.axis_index("y"), lax.axis_index("c")
    me = 4 * x + 2 * y + c
    copies = []
    for t in range(len(ins)):
        for rel in range(1, NDEV):
            px, py, pc = _peer(rel, x, y, c)
            src = ins[t].at[4 * px + 2 * py + pc] if scatter else ins[t]
            copies.append(pltpu.make_async_remote_copy(
                src_ref=src, dst_ref=lands[t].at[me], send_sem=send_sems.at[t * (NDEV - 1) + rel - 1],
                recv_sem=recv_sems.at[t * (NDEV - 1) + rel - 1], device_id=(px, py, pc),
                device_id_type=pl.DeviceIdType.MESH))
    return me, copies


def exchange_start(name, items, scatter, dep=None):
    n = len(items)
    hbm = pl.BlockSpec(memory_space=pltpu.HBM)
    sem = pl.BlockSpec(memory_space=pltpu.SEMAPHORE)
    has_dep = dep is not None
    land_shapes = [it.shape if scatter else (NDEV,) + it.shape for it in items]

    def body(*refs):
        ins, lands = refs[:n], refs[n:2 * n]
        outs = refs[2 * n + has_dep:]
        send_sems, recv_sems, token = outs[0], outs[1], outs[2 + 2 * n]
        _, copies = _split_copies(ins, lands, send_sems, recv_sems, scatter)
        for cp in copies:
            cp.start()
        token[...] = jnp.zeros_like(token)

    sems = pltpu.SemaphoreType.DMA((n * (NDEV - 1),))
    out_shape = ([sems, sems] + [pltpu.HBM(it.shape, it.dtype) for it in items]
                 + [pltpu.HBM(sh, it.dtype) for sh, it in zip(land_shapes, items)] + [jax.ShapeDtypeStruct((8, LANES), F32)])
    operands = ([pltpu.with_memory_space_constraint(it, pltpu.HBM) for it in items]
                + [pltpu.with_memory_space_constraint(lax.empty(sh, it.dtype), pltpu.HBM) for sh, it in zip(land_shapes, items)]
                + ([dep] if has_dep else []))
    res = pl.pallas_call(
        body, name=name, in_specs=[hbm] * (2 * n) + [pl.BlockSpec(memory_space=pl.ANY)] * has_dep,
        out_specs=[sem, sem] + [hbm] * (2 * n) + [pl.BlockSpec(memory_space=pltpu.VMEM)], out_shape=out_shape,
        input_output_aliases={t: 2 + t for t in range(2 * n)},
        compiler_params=pltpu.CompilerParams(has_side_effects=pltpu.SideEffectType.DATAFLOW_SIDE_EFFECTING),
    )(*operands)
    return (res[0], res[1], list(res[2:2 + n]), list(res[2 + n:2 + 2 * n]), scatter), res[2 + 2 * n]


def exchange_wait(name, handle, after):
    send_sems, recv_sems, ins, lands, scatter = handle
    n = len(ins)
    after = list(after) if isinstance(after, (list, tuple)) else [after]
    hbm = pl.BlockSpec(memory_space=pltpu.HBM)
    sem = pl.BlockSpec(memory_space=pltpu.SEMAPHORE)

    def body(*refs):
        _, copies = _split_copies(refs[:n], refs[n:2 * n], refs[2 * n], refs[2 * n + 1], scatter)
        for cp in copies:
            cp.wait_send()
            cp.wait_recv()

    res = pl.pallas_call(
        body, name=name, in_specs=[hbm] * (2 * n) + [sem, sem] + [pl.BlockSpec(memory_space=pl.ANY)] * len(after),
        out_specs=[hbm] * (2 * n), out_shape=[pltpu.HBM(a.shape, a.dtype) for a in ins + lands],
        input_output_aliases={t: t for t in range(2 * n)},
        compiler_params=pltpu.CompilerParams(has_side_effects=pltpu.SideEffectType.DATAFLOW_SIDE_EFFECTING),
    )(*ins, *lands, send_sems, recv_sems, *after)
    me = 4 * lax.axis_index("x") + 2 * lax.axis_index("y") + lax.axis_index("c")
    out = []
    for src, landed in zip(res[:n], res[n:]):
        own = lax.dynamic_index_in_dim(src, me, axis=0, keepdims=True) if scatter else src[None]
        out.append(lax.dynamic_update_slice_in_dim(landed, own, me, axis=0))
    return out


def sum_slots(name, parts):
    _, rows, cols = parts.shape

    def body(p_ref, o_ref):
        s = p_ref[0]
        for k in range(1, NDEV):
            s = s + p_ref[k]
        o_ref[...] = s

    return pl.pallas_call(body, name=name, out_shape=jax.ShapeDtypeStruct((rows, cols), F32), compiler_params=_params())(parts)


def _heads(t):
    return t.reshape(T, B_HEADS, HD).transpose(1, 0, 2)


def _unheads(t):
    return t.transpose(1, 0, 2).reshape(T, B_HEADS * HD)


def _cols_from_slots(g):
    return g.transpose(1, 0, 2).reshape(g.shape[1], NDEV * g.shape[2])


def _slots_from_cols(w):
    return w.reshape(w.shape[0], NDEV, w.shape[1] // NDEV).transpose(1, 0, 2)


def _pack(arrays, rows):
    flat = jnp.concatenate([a.reshape(-1).astype(F32) for a in arrays])
    return jnp.pad(flat, (0, rows * LANES - flat.shape[0])).reshape(rows, LANES)


def _unpack(buf, shapes):
    flat = buf.reshape(-1)
    out, pos = [], 0
    for sh in shapes:
        size = math.prod(sh)
        out.append(flat[pos:pos + size].reshape(sh))
        pos += size
    return out


def kernel(x, norm_gains, w_qkv_a, w_o_a, w_q_b, w_o_b, kv_norm, w_kvf, b_f, w_up, conv_w, conv_b, w_down, loss_target, m_norm_gains, m_w_qkv_a, m_w_o_a, m_w_q_b, m_w_o_b, m_kv_norm, m_w_kvf, m_b_f, m_w_up, m_conv_w, m_conv_b, m_w_down, v_norm_gains, v_w_qkv_a, v_w_o_a, v_w_q_b, v_w_o_b, v_kv_norm, v_w_kvf, v_b_f, v_w_up, v_conv_w, v_conv_b, v_w_down):
    me = 4 * lax.axis_index("x") + 2 * lax.axis_index("y") + lax.axis_index("c")
    n_b = DEPTH - N_A

    def bf(a):
        return a.astype(BF16)

    mixer_w = [[bf(w_qkv_a[l]), bf(w_o_a[l])] if l < N_A else [bf(w_q_b[l - N_A]), bf(w_o_b[l - N_A])] for l in range(DEPTH)]
    ffn_w = [[bf(w_up[l]), bf(w_down[l])] for l in range(DEPTH)]
    mixer_w[0] += [norm_gains, conv_w]
    mixer_w[N_A] += [bf(w_kvf)]
    handles, tok = [], None
    for l in range(DEPTH):
        for part, items in (("mixer", mixer_w[l]), ("ffn", ffn_w[l])):
            hd, tok = exchange_start(f"gather_start_{part}{l}", items, False, dep=tok)
            handles.append(hd)

    wqkv, woa, wqb, wob, wup, wdown = {}, {}, {}, {}, {}, {}

    def take_mixer(l, arrived):
        if l < N_A:
            wqkv[l] = _cols_from_slots(arrived[0])
            woa[l] = _cols_from_slots(arrived[1])
        else:
            wqb[l - N_A] = arrived[0].reshape(D, D)
            wob[l - N_A] = arrived[1].reshape(D, D)

    def take_ffn(l, arrived):
        wup[l] = arrived[0]
        wdown[l] = arrived[1].reshape(4, SLAB, D)

    arrived = exchange_wait("gather_wait_mixer0", handles[0], tok)
    take_mixer(0, arrived)
    gains = arrived[2].transpose(1, 2, 0, 3).reshape(DEPTH, 4, D)
    cws = [arrived[3][:, l] for l in range(DEPTH)]
    cbs = [conv_b[l].reshape(NDEV, 1, SLAB) for l in range(DEPTH)]
    tables = rope_tables()
    b_col = b_f.reshape(B_HEADS, 1)

    h = x.reshape(T, D)
    _, (xn,) = resid_norm("norm_in", h, None, None, [gains[0, 0]])
    saved = []
    shared = None
    for l in range(DEPTH):
        s = {"h": h, "xn": xn}
        if l >= 1:
            arrived = exchange_wait(f"gather_wait_mixer{l}", handles[2 * l], h)
            take_mixer(l, arrived[:2])
            if l == N_A:
                wkvf = jnp.pad(_cols_from_slots(arrived[2]), ((0, 0), (0, KVF_PAD - KVF)))
        if l < N_A:
            qkv = matmul(f"qkv{l}", xn, wqkv[l], tm=1024, tn=768)
            s["qkvr"] = rope(f"rope{l}", qkv, tables, inverse=False)
            os_, lses = [], []
            for g, (_, r) in enumerate(A_GROUPS):
                o, lse = attn_a_fwd(f"attn_a{l}_{g}", s["qkvr"], g, r)
                os_.append(o)
                lses.append(lse)
            s["o"], s["lse"] = os_, lses
            s["oc"] = combine_fwd(f"combine{l}", os_, lses)
            mix = matmul(f"wo_a{l}", s["oc"], woa[l], tm=1024, tn=512)
        else:
            if l == N_A:
                kvf = matmul("kvf", xkv, wkvf, tm=1024, tn=768)
                k_h = _heads(kvf[:, :D].astype(BF16))
                v_h = _heads(kvf[:, D:2 * D].astype(BF16))
                fgt = kvf[:, 2 * D:KVF].T
                ct = gates_fwd("gates", fgt, b_col).reshape(B_HEADS, T // TK, TK)
                shared = {"xkv": xkv, "k": k_h, "v": v_h, "fgt": fgt, "ct": ct, "h": h}
            j = l - N_A
            s["q"] = _heads(matmul(f"wq_b{j}", xn, wqb[j], out_dtype=BF16, tm=1024, tn=512))
            s["o"], s["lse"] = fox_fwd(f"fox{j}", s["q"], shared["k"], shared["v"], shared["ct"])
            s["oc"] = _unheads(s["o"])
            mix = matmul(f"wo_b{j}", s["oc"], wob[j], tm=1024, tn=512)
        s["mix"] = mix
        take_ffn(l, exchange_wait(f"gather_wait_ffn{l}", handles[2 * l + 1], mix))
        s["h1"], (s["xn2"],) = resid_norm(f"norm_mid{l}", h, mix, gains[l, 1], [gains[l, 2]])
        s["a"] = matmul(f"up{l}", s["xn2"], wup[l], tm=1024, tn=SLAB, batch="b_out")
        s["u"] = convglu_fwd(f"convglu{l}", s["a"], cws[l], cbs[l])
        s["f"] = matmul(f"down{l}", s["u"], wdown[l], tm=1024, tn=512, batch="reduce")
        nxt = [gains[l + 1, 0]] if l + 1 < DEPTH else []
        if l == N_A - 1:
            nxt.append(kv_norm)
        h, normed = resid_norm(f"norm_out{l}", s["h1"], s["f"], gains[l, 3], nxt)
        if l + 1 < DEPTH:
            xn = normed[0]
        if l == N_A - 1:
            xkv = normed[1]
        saved.append(s)

    dh, loss_part = loss_head("loss", h, loss_target.reshape(T, D))

    d_gains = [[None] * 4 for _ in range(DEPTH)]
    d_cw, d_cb = [None] * DEPTH, [None] * DEPTH
    gw = {"qkv": [None] * N_A, "oa": [None] * N_A, "qb": [None] * n_b, "ob": [None] * n_b, "up": [None] * DEPTH,
          "down": [None] * DEPTH}
    kv_acc = None
    d_rows = []
    sent = []
    tok = None

    def slots_rows(g):
        return g.reshape(NDEV, g.shape[0] // NDEV, g.shape[1])

    for l in reversed(range(DEPTH)):
        s = saved[l]
        df, d_gains[l][3] = rms_bwd(f"bwd_norm_out{l}", s["f"], gains[l, 3], dh, out_dtype=BF16, dep=tok)
        du = matmul(f"bwd_down_x{l}", df, wdown[l], tb=True, out_dtype=BF16, tm=1024, tn=SLAB, batch="b_out")
        gw["down"][l] = matmul(f"bwd_down_w{l}", s["u"], df, ta=True, out_dtype=BF16, tm=SLAB, tn=512, batch="a_out")
        da, d_cw[l], d_cb[l] = convglu_bwd(f"bwd_convglu{l}", du, s["a"], cws[l], cbs[l])
        dxn2 = matmul(f"bwd_up_x{l}", da, wup[l], tb=True, tm=1024, tn=512, batch="reduce")
        gw["up"][l] = matmul(f"bwd_up_w{l}", s["xn2"], da, ta=True, out_dtype=BF16, tm=512, tn=SLAB, batch="b_out")
        dh1, d_gains[l][2] = rms_bwd(f"bwd_norm_mid{l}", s["h1"], gains[l, 2], dxn2, add=dh)
        ffn_items = [gw["up"][l], slots_rows(gw["down"][l].reshape(DFF, D))]
        ffn_what = [("up", l), ("down", l)]
        tok = None
        if l == 0:
            hd, tok = exchange_start("scatter_start_ffn0", ffn_items, True)
            sent.append((hd, ffn_what))
            ffn_items, ffn_what = [], []
        dmix, d_gains[l][1] = rms_bwd(f"bwd_norm_mix{l}", s["mix"], gains[l, 1], dh1, out_dtype=BF16, dep=tok)
        if l < N_A:
            doc = matmul(f"bwd_wo_a_x{l}", dmix, woa[l], tb=True, tm=1024, tn=A_W)
            gw["oa"][l] = matmul(f"bwd_wo_a_w{l}", s["oc"], dmix, ta=True, out_dtype=BF16, tm=A_W, tn=512)
            dos, dds = combine_bwd(f"bwd_combine{l}", doc, s["o"], s["lse"])
            cols = [None] * 9
            for g, (_, r) in enumerate(A_GROUPS):
                dq, dk, dv = attn_a_bwd(f"bwd_attn_a{l}_{g}", s["qkvr"], dos[g], s["lse"][g], dds[g], g, r)
                cols[g], cols[3 + g], cols[6 + g] = dq, dk, dv
            dqkv = rope(f"bwd_rope{l}", jnp.concatenate(cols, axis=1), tables, inverse=True)
            dxn = matmul(f"bwd_qkv_x{l}", dqkv, wqkv[l], tb=True, tm=1024, tn=512)
            gw["qkv"][l] = matmul(f"bwd_qkv_w{l}", s["xn"], dqkv, ta=True, out_dtype=BF16, tm=512, tn=768)
        else:
            j = l - N_A
            do = _heads(matmul(f"bwd_wo_b_x{j}", dmix, wob[j], tb=True, out_dtype=BF16, tm=1024, tn=512))
            gw["ob"][j] = matmul(f"bwd_wo_b_w{j}", s["oc"], dmix, ta=True, out_dtype=BF16, tm=512, tn=512)
            args = (s["q"], shared["k"], shared["v"], shared["ct"], s["o"], do, s["lse"])
            dq_h, drow = fox_bwd_q(f"bwd_fox_q{j}", *args)
            dq = _unheads(dq_h)
            d_rows.append(drow.reshape(B_HEADS, T))
            kv_acc = fox_bwd_kv(f"bwd_fox_kv{j}", *args, kv_acc)
            dxn = matmul(f"bwd_wq_b_x{j}", dq, wqb[j], tb=True, tm=1024, tn=512)
            gw["qb"][j] = matmul(f"bwd_wq_b_w{j}", s["xn"], dq, ta=True, out_dtype=BF16, tm=512, tn=512)
        dh, d_gains[l][0] = rms_bwd(f"bwd_norm_in{l}", s["h"], gains[l, 0], dxn, add=dh1)
        if l == N_A:
            dk_h, dv_h, dct = kv_acc
            dfgt, d_bf = gates_bwd("bwd_gates", shared["fgt"], b_col, [dct.reshape(B_HEADS, T)] + d_rows)
            dkvf = jnp.concatenate(
                [_unheads(dk_h).astype(BF16), _unheads(dv_h).astype(BF16), dfgt.T.astype(BF16),
                 jnp.zeros((T, KVF_PAD - KVF), BF16)], axis=1)
            dxkv = matmul("bwd_kvf_x", dkvf, wkvf, tb=True, tm=1024, tn=512)
            g_kvf_full = matmul("bwd_kvf_w", shared["xkv"], dkvf, ta=True, out_dtype=BF16, tm=512, tn=768)
            dh, d_kvn = rms_bwd("bwd_norm_kv", shared["h"], kv_norm, dxkv, add=dh)
        if l < N_A:
            items = [_slots_from_cols(gw["qkv"][l]), _slots_from_cols(gw["oa"][l])]
            what = [("qkv", l), ("oa", l)]
        else:
            items = [slots_rows(gw["qb"][l - N_A]), slots_rows(gw["ob"][l - N_A])]
            what = [("qb", l - N_A), ("ob", l - N_A)]
        if l == N_A:
            items.append(_slots_from_cols(g_kvf_full[:, :KVF]))
            what.append(("kvf", 0))
        hd, tok = exchange_start(f"scatter_start{l}", items + ffn_items, True)
        sent.append((hd, what + ffn_what))

    small_shapes = [(DEPTH, 4, D), (D,), (B_HEADS,), (DEPTH, 3, NDEV * SLAB), (DEPTH, NDEV * SLAB), (1,)]
    small = [
        jnp.stack([jnp.concatenate(row, axis=0) for row in d_gains]),
        d_kvn, d_bf,
        jnp.stack([d.transpose(1, 0, 2).reshape(3, NDEV * SLAB) for d in d_cw]),
        jnp.stack([d.reshape(NDEV * SLAB) for d in d_cb]),
        loss_part + tok[0:1, 0:1],
    ]
    small_rows = 848
    (small_all,) = exchange("gather_small_grads", [_pack(small, small_rows)], scatter=False)
    g_gains_full, g_kvn, g_bf, g_cw_full, g_cb, loss = _unpack(sum_slots("sum_small", small_all), small_shapes)
    g_gains_mine = lax.dynamic_slice_in_dim(g_gains_full, me * (D // NDEV), D // NDEV, axis=2)
    g_cw_mine = lax.dynamic_slice_in_dim(g_cw_full, me * SLAB, SLAB, axis=2)

    small_w = [norm_gains, kv_norm, b_f, conv_w, conv_b]
    small_m = [m_norm_gains, m_kv_norm, m_b_f, m_conv_w, m_conv_b]
    small_v = [v_norm_gains, v_kv_norm, v_b_f, v_conv_w, v_conv_b]
    small_g = [g_gains_mine, g_kvn, g_bf, g_cw_mine, g_cb]
    shapes = [w.shape for w in small_w]
    rows = 320
    res = adamw("adamw_small", _pack(small_g, rows)[None], _pack(small_w, rows), _pack(small_m, rows), _pack(small_v, rows))
    _, s_delta, s_m, s_v = [_unpack(r, shapes) for r in res]

    big = {"qkv": (w_qkv_a, m_w_qkv_a, v_w_qkv_a), "oa": (w_o_a, m_w_o_a, v_w_o_a), "qb": (w_q_b, m_w_q_b, v_w_q_b),
           "ob": (w_o_b, m_w_o_b, v_w_o_b), "kvf": (w_kvf, m_w_kvf, v_w_kvf), "up": (w_up, m_w_up, v_w_up),
           "down": (w_down, m_w_down, v_w_down)}
    updated = {name: None for name in big}
    after = [s_delta[0]]
    for handle, what in sent:
        received = exchange_wait(f"scatter_wait_{what[0][0]}{what[0][1]}", handle, after)
        for (name, layer), rec in zip(what, received):
            cols = rec.shape[-1]
            rows = rec.size // (NDEV * cols)
            w, m, v = (a.reshape(-1, cols) for a in big[name])
            updated[name] = adamw(f"adamw_{name}{layer}", rec.reshape(NDEV, rows, cols), w, m, v, row0=layer * rows,
                                  prev=updated[name])
        after = [updated[name][0] for name, _ in what]
    big_out = [[r.reshape(big[name][0].shape) for r in updated[name]] for name in ("qkv", "oa", "qb", "ob", "kvf", "up", "down")]

    def pick(k):
        b = [o[k] for o in big_out]
        sm = {0: small_g, 1: s_delta, 2: s_m, 3: s_v}[k]
        return [sm[0], b[0], b[1], b[2], b[3], sm[1], b[4], sm[2], b[5], sm[3], sm[4], b[6]]

    return (loss.reshape(()), dh.reshape(1, T, D), *pick(0), *pick(1), *pick(2), *pick(3))
```

```python
import functools
import math

import jax
import jax.numpy as jnp
from jax import lax
from jax.experimental import pallas as pl
from jax.experimental.pallas import tpu as pltpu

F32 = jnp.float32
BF16 = jnp.bfloat16

T = 2048
D = 1024
DEPTH = 4
N_A = 2
HD = 64
A_GROUPS = ((128, 1), (512, 4), (2048, 16))
A_W = 768
B_HEADS = 16
DFF = 2816
NDEV = 8
SLAB = 2 * DFF // NDEV
KVF = 2 * D + B_HEADS
KVF_PAD = 2304
ROPE_DIM = 16
ROPE_THETA = 500000.0
EPS = 1e-6
NEG = -1e30
BLK = 128
LANES = 128
SCALE = HD ** -0.5
VMEM_LIMIT = 56 * 1024 * 1024

ADAM_LR = 0.001
ADAM_B1 = 0.9
ADAM_B2 = 0.999
ADAM_EPS = 1e-08
ADAM_WD = 0.01
ADAM_STEP = 10
HIGHEST = lax.Precision.HIGHEST


def _params(*sem):
    return pltpu.CompilerParams(dimension_semantics=sem or None, vmem_limit_bytes=VMEM_LIMIT)


def _bf(x):
    return x if x.dtype == BF16 else x.astype(BF16)


def matmul(name, a, b, *, ta=False, tb=False, out_dtype=F32, tm=512, tn=512, batch=None):
    a_b = batch in ("a_out", "reduce")
    b_b = batch in ("b_out", "reduce")
    o_b = batch in ("a_out", "b_out")
    nb = a.shape[0] if a_b else (b.shape[0] if b_b else 1)
    ash = a.shape[1:] if a_b else a.shape
    bsh = b.shape[1:] if b_b else b.shape
    m, k = (ash[1], ash[0]) if ta else ash
    k2, n = (bsh[1], bsh[0]) if tb else bsh
    assert k == k2, (name, a.shape, b.shape)
    tm, tn = min(tm, m), min(tn, n)
    assert m % tm == 0 and n % tn == 0, (name, m, n, tm, tn)
    nbr = nb if batch == "reduce" else 1
    grid = (nb if o_b else 1, n // tn, m // tm, nbr)

    def bidx(bo, br):
        return bo if o_b else br

    def spec(batched, block, idx):
        if batched:
            return pl.BlockSpec((None,) + block, lambda bo, j, i, br: (bidx(bo, br),) + idx(i, j))
        return pl.BlockSpec(block, lambda bo, j, i, br: idx(i, j))

    a_spec = spec(a_b, (k, tm) if ta else (tm, k), (lambda i, j: (0, i)) if ta else (lambda i, j: (i, 0)))
    b_spec = spec(b_b, (tn, k) if tb else (k, tn), (lambda i, j: (j, 0)) if tb else (lambda i, j: (0, j)))
    o_spec = spec(o_b, (tm, tn), lambda i, j: (i, j))
    dims = (((0 if ta else 1,), (1 if tb else 0,)), ((), ()))

    def body(a_ref, b_ref, o_ref, *acc):
        p = lax.dot_general(_bf(a_ref[...]), _bf(b_ref[...]), dims, preferred_element_type=F32)
        if nbr == 1:
            o_ref[...] = p.astype(out_dtype)
        else:
            r = pl.program_id(3)

            @pl.when(r == 0)
            def _():
                acc[0][...] = p

            @pl.when(r > 0)
            def _():
                acc[0][...] += p

            @pl.when(r == nbr - 1)
            def _():
                o_ref[...] = acc[0][...].astype(out_dtype)

    out_shape = ((nb,) if o_b else ()) + (m, n)
    return pl.pallas_call(
        body, name=name, grid=grid, in_specs=[a_spec, b_spec], out_specs=o_spec,
        out_shape=jax.ShapeDtypeStruct(out_shape, out_dtype),
        scratch_shapes=[pltpu.VMEM((tm, tn), F32)] if nbr > 1 else [],
        compiler_params=_params("parallel", "parallel", "parallel", "arbitrary"),
    )(a, b)


def _rms(x, g):
    return x * lax.rsqrt(jnp.mean(x * x, axis=-1, keepdims=True) + EPS) * g


def resid_norm(name, h, y, gy, gains, tb=256, dep=None):
    n_g = len(gains)
    has_y = y is not None
    has_dep = dep is not None
    row = pl.BlockSpec((tb, D), lambda i: (i, 0))
    vec = pl.BlockSpec((1, D), lambda i: (0, 0))

    def body(*refs):
        h_ref = refs[0]
        pos = 1
        hn = h_ref[...]
        if has_y:
            hn = hn + _rms(refs[1][...], refs[2][...])
            pos = 3
        g_refs = refs[pos:pos + n_g]
        outs = refs[pos + n_g + has_dep:]
        if has_y:
            outs[0][...] = hn
            outs = outs[1:]
        for g_ref, o_ref in zip(g_refs, outs):
            o_ref[...] = _rms(hn, g_ref[...]).astype(BF16)

    ins = [h] + ([y, gy.reshape(1, D)] if has_y else []) + [g.reshape(1, D) for g in gains] + ([dep] if has_dep else [])
    in_specs = [row] + ([row, vec] if has_y else []) + [vec] * n_g + [pl.BlockSpec(memory_space=pl.ANY)] * has_dep
    out_shape = ([jax.ShapeDtypeStruct((T, D), F32)] if has_y else []) + [jax.ShapeDtypeStruct((T, D), BF16)] * n_g
    res = pl.pallas_call(
        body, name=name, grid=(T // tb,), in_specs=in_specs, out_specs=[row] * len(out_shape),
        out_shape=out_shape, compiler_params=_params("parallel"),
    )(*ins)
    return (res[0], list(res[1:])) if has_y else (h, list(res))


def rms_bwd(name, x, g, dy, add=None, out_dtype=F32, tb=256, dep=None):
    has_add = add is not None
    has_dep = dep is not None
    row = pl.BlockSpec((tb, D), lambda i: (i, 0))
    vec = pl.BlockSpec((1, D), lambda i: (0, 0))

    def body(*refs):
        x_ref, g_ref, dy_ref = refs[:3]
        dx_ref, dg_ref = refs[-2:]
        xv = x_ref[...]
        dyv = dy_ref[...].astype(F32)
        r = lax.rsqrt(jnp.mean(xv * xv, axis=-1, keepdims=True) + EPS)
        gdy = dyv * g_ref[...]
        dx = r * gdy - xv * (r * r * r * jnp.mean(xv * gdy, axis=-1, keepdims=True))
        if has_add:
            dx = dx + refs[3][...]
        dx_ref[...] = dx.astype(out_dtype)
        part = jnp.sum(dyv * xv * r, axis=0, keepdims=True)

        @pl.when(pl.program_id(0) == 0)
        def _():
            dg_ref[...] = part

        @pl.when(pl.program_id(0) > 0)
        def _():
            dg_ref[...] += part

    ins = [x, g.reshape(1, D), dy] + ([add] if has_add else []) + ([dep] if has_dep else [])
    return pl.pallas_call(
        body, name=name, grid=(T // tb,),
        in_specs=[row, vec, row] + ([row] if has_add else []) + [pl.BlockSpec(memory_space=pl.ANY)] * has_dep,
        out_specs=[row, vec],
        out_shape=[jax.ShapeDtypeStruct((T, D), out_dtype), jax.ShapeDtypeStruct((1, D), F32)],
        compiler_params=_params("arbitrary"),
    )(*ins)


def rope_tables():
    pos = jnp.arange(T, dtype=F32)
    inv = ROPE_THETA ** (-jnp.arange(0, ROPE_DIM, 2, dtype=F32) / ROPE_DIM)
    ang = pos[:, None] * inv[None, :]
    cos, sin = jnp.cos(ang), jnp.sin(ang)
    half = ROPE_DIM // 2
    one = jnp.ones((T, HD - ROPE_DIM), F32)
    zero = jnp.zeros((T, HD - ROPE_DIM), F32)
    zh = jnp.zeros((T, half), F32)
    c = jnp.concatenate([cos, cos, one], axis=1)
    s_up = jnp.concatenate([zh, sin, zero], axis=1)
    s_dn = jnp.concatenate([-sin, zh, zero], axis=1)
    rep = LANES // HD
    return tuple(jnp.tile(t, (1, rep)) for t in (c, s_up, s_dn))


def rope(name, t, tables, inverse, out_dtype=BF16, tb=256):
    c, s_up, s_dn = tables
    n_rot = 2 * A_W // LANES
    half = ROPE_DIM // 2
    blk = pl.BlockSpec((tb, 3 * A_W), lambda i: (i, 0))
    tab = pl.BlockSpec((tb, LANES), lambda i: (i, 0))

    def body(t_ref, c_ref, su_ref, sd_ref, o_ref):
        sgn = -1.0 if inverse else 1.0
        cos, up, dn = c_ref[...], sgn * su_ref[...], sgn * sd_ref[...]
        for j in range(n_rot):
            sl = slice(j * LANES, (j + 1) * LANES)
            x = t_ref[:, sl].astype(F32)
            o_ref[:, sl] = (x * cos + pltpu.roll(x, half, 1) * up + pltpu.roll(x, LANES - half, 1) * dn).astype(out_dtype)
        o_ref[:, 2 * A_W:] = t_ref[:, 2 * A_W:].astype(out_dtype)

    return pl.pallas_call(
        body, name=name, grid=(T // tb,), in_specs=[blk, tab, tab, tab], out_specs=blk,
        out_shape=jax.ShapeDtypeStruct((T, 3 * A_W), out_dtype), compiler_params=_params("parallel"),
    )(t, c, s_up, s_dn)


GW = 4 * HD


def _band_mask(b):
    qi = lax.broadcasted_iota(jnp.int32, (BLK, 2 * BLK), 0)
    kj = lax.broadcasted_iota(jnp.int32, (BLK, 2 * BLK), 1)
    return (kj <= qi + BLK) & (kj >= qi) & ((kj >= BLK) | (b > 0))


def attn_a_fwd(name, qkvr, g, r):
    length = T // r
    nblk = length // BLK
    view = qkvr.reshape(length, r * 3 * A_W)
    ncol = 3 * A_W // GW

    def col(section, prev):
        def idx(j, b):
            return (jnp.maximum(b - 1, 0) if prev else b, j * ncol + 3 * section + g)
        return pl.BlockSpec((BLK, GW), idx)

    out = pl.BlockSpec((BLK, GW), lambda j, b: (b, j))

    def body(q_ref, kp_ref, kc_ref, vp_ref, vc_ref, o_ref, lse_ref):
        mask = _band_mask(pl.program_id(1))
        k2 = jnp.concatenate([kp_ref[...], kc_ref[...]], axis=0)
        v2 = jnp.concatenate([vp_ref[...], vc_ref[...]], axis=0)
        q = q_ref[...]
        for h in range(4):
            sl = slice(h * HD, (h + 1) * HD)
            s = lax.dot_general(q[:, sl], k2[:, sl], (((1,), (1,)), ((), ())), preferred_element_type=F32) * SCALE
            s = jnp.where(mask, s, NEG)
            m = jnp.max(s, axis=1, keepdims=True)
            p = jnp.exp(s - m)
            l = jnp.sum(p, axis=1, keepdims=True)
            o_ref[:, sl] = jnp.dot((p / l).astype(BF16), v2[:, sl], preferred_element_type=F32)
            lse_ref[:, sl] = jnp.broadcast_to(m + jnp.log(l), (BLK, HD))

    shape = jax.ShapeDtypeStruct((length, r * GW), F32)
    o, lse = pl.pallas_call(
        body, name=name, grid=(r, nblk),
        in_specs=[col(0, False), col(1, True), col(1, False), col(2, True), col(2, False)],
        out_specs=[out, out], out_shape=[shape, shape], compiler_params=_params("parallel", "parallel"),
    )(view, view, view, view, view)
    return o.reshape(T, GW), lse.reshape(T, GW)


def attn_a_bwd(name, qkvr, do, lse, dd, g, r):
    length = T // r
    nblk = length // BLK
    view = qkvr.reshape(length, r * 3 * A_W)
    ncol = 3 * A_W // GW

    def col(section, shift):
        def idx(j, b):
            return (jnp.clip(b + shift, 0, nblk - 1), j * ncol + 3 * section + g)
        return pl.BlockSpec((BLK, GW), idx)

    def tok(shift):
        return pl.BlockSpec((BLK, GW), lambda j, b: (jnp.clip(b + shift, 0, nblk - 1), j))

    def body(q_ref, qn_ref, kp_ref, kc_ref, vp_ref, vc_ref, do_ref, don_ref, lse_ref, lsen_ref, dd_ref, ddn_ref,
             dq_ref, dk_ref, dv_ref):
        b = pl.program_id(1)
        mask = _band_mask(b)
        qi = lax.broadcasted_iota(jnp.int32, (2 * BLK, BLK), 0)
        kj = lax.broadcasted_iota(jnp.int32, (2 * BLK, BLK), 1)
        kmask = ((qi < BLK) & (kj <= qi)) | ((qi >= BLK) & (kj >= qi - BLK) & (b + 1 < nblk))
        k2 = jnp.concatenate([kp_ref[...], kc_ref[...]], axis=0)
        v2 = jnp.concatenate([vp_ref[...], vc_ref[...]], axis=0)
        q2 = jnp.concatenate([q_ref[...], qn_ref[...]], axis=0)
        do2 = jnp.concatenate([do_ref[...], don_ref[...]], axis=0)
        lse2 = jnp.concatenate([lse_ref[...], lsen_ref[...]], axis=0)
        dd2 = jnp.concatenate([dd_ref[...], ddn_ref[...]], axis=0)
        nt = (((1,), (1,)), ((), ()))
        tn = (((0,), (0,)), ((), ()))
        for h in range(4):
            sl = slice(h * HD, (h + 1) * HD)
            one = slice(h * HD, h * HD + 1)
            qh, kh, vh, doh = q2[:, sl], k2[:, sl], v2[:, sl], do2[:, sl]
            s = lax.dot_general(qh[:BLK], kh, nt, preferred_element_type=F32) * SCALE
            p = jnp.where(mask, jnp.exp(s - lse2[:BLK, one]), 0.0)
            dp = lax.dot_general(doh[:BLK], vh, nt, preferred_element_type=F32)
            ds = p * (dp + dd2[:BLK, one])
            dq_ref[:, sl] = jnp.dot(ds.astype(BF16), kh, preferred_element_type=F32) * SCALE
            kc, vc = kh[BLK:], vh[BLK:]
            s = lax.dot_general(qh, kc, nt, preferred_element_type=F32) * SCALE
            p = jnp.where(kmask, jnp.exp(s - lse2[:, one]), 0.0)
            dp = lax.dot_general(doh, vc, nt, preferred_element_type=F32)
            ds = p * (dp + dd2[:, one])
            dk_ref[:, sl] = lax.dot_general(ds.astype(BF16), qh, tn, preferred_element_type=F32) * SCALE
            dv_ref[:, sl] = lax.dot_general(p.astype(BF16), doh, tn, preferred_element_type=F32)

    dov = do.reshape(length, r * GW)
    lsev = lse.reshape(length, r * GW)
    ddv = dd.reshape(length, r * GW)
    shape = jax.ShapeDtypeStruct((length, r * GW), F32)
    dq, dk, dv = pl.pallas_call(
        body, name=name, grid=(r, nblk),
        in_specs=[col(0, 0), col(0, 1), col(1, -1), col(1, 0), col(2, -1), col(2, 0),
                  tok(0), tok(1), tok(0), tok(1), tok(0), tok(1)],
        out_specs=[tok(0)] * 3, out_shape=[shape] * 3, compiler_params=_params("parallel", "parallel"),
    )(view, view, view, view, view, view, dov, dov, lsev, lsev, ddv, ddv)
    return dq.reshape(T, GW), dk.reshape(T, GW), dv.reshape(T, GW)


def _head_sum(x):
    i = lax.div(lax.broadcasted_iota(jnp.int32, (GW, GW), 0), jnp.int32(HD))
    j = lax.div(lax.broadcasted_iota(jnp.int32, (GW, GW), 1), jnp.int32(HD))
    return jnp.dot(x, (i == j).astype(F32), precision=HIGHEST, preferred_element_type=F32)


def _alphas(lses):
    m = jnp.maximum(jnp.maximum(lses[0], lses[1]), lses[2])
    e = [jnp.exp(l - m) for l in lses]
    z = e[0] + e[1] + e[2]
    return [x / z for x in e]


def combine_fwd(name, os_, lses, tb=256):
    blk = pl.BlockSpec((tb, GW), lambda i: (i, 0))

    def body(o0, o1, o2, l0, l1, l2, oc_ref):
        al = _alphas([l0[...], l1[...], l2[...]])
        for g, o_ref in enumerate((o0, o1, o2)):
            oc_ref[:, g * GW:(g + 1) * GW] = (o_ref[...] * al[g]).astype(BF16)

    return pl.pallas_call(
        body, name=name, grid=(T // tb,), in_specs=[blk] * 6, out_specs=pl.BlockSpec((tb, A_W), lambda i: (i, 0)),
        out_shape=jax.ShapeDtypeStruct((T, A_W), BF16), compiler_params=_params("parallel"),
    )(*os_, *lses)


def combine_bwd(name, doc, os_, lses, tb=256):
    blk = pl.BlockSpec((tb, GW), lambda i: (i, 0))

    def body(doc_ref, o0, o1, o2, l0, l1, l2, d0, d1, d2, e0, e1, e2):
        al = _alphas([l0[...], l1[...], l2[...]])
        dal = [_head_sum(doc_ref[:, g * GW:(g + 1) * GW] * o_ref[...]) for g, o_ref in enumerate((o0, o1, o2))]
        mean = al[0] * dal[0] + al[1] * dal[1] + al[2] * dal[2]
        for g, (do_ref, dd_ref) in enumerate(((d0, e0), (d1, e1), (d2, e2))):
            do_ref[...] = (doc_ref[:, g * GW:(g + 1) * GW] * al[g]).astype(BF16)
            dd_ref[...] = al[g] * (dal[g] - mean) - al[g] * dal[g]

    res = pl.pallas_call(
        body, name=name, grid=(T // tb,), in_specs=[pl.BlockSpec((tb, A_W), lambda i: (i, 0))] + [blk] * 6,
        out_specs=[blk] * 6,
        out_shape=[jax.ShapeDtypeStruct((T, GW), BF16)] * 3 + [jax.ShapeDtypeStruct((T, GW), F32)] * 3,
        compiler_params=_params("parallel"),
    )(doc, *os_, *lses)
    return res[:3], res[3:]


TQ = 512
TK = TQ
FOX_HEADS = 1
NT = (((1,), (1,)), ((), ()))
TN = (((0,), (0,)), ((), ()))


def _fox_scores(q, kb, cj, row0, col0, masked):
    s = lax.dot_general(q, kb, NT, preferred_element_type=F32) * SCALE - cj
    if masked:
        qi = row0 + lax.broadcasted_iota(jnp.int32, s.shape, 0)
        kj = col0 + lax.broadcasted_iota(jnp.int32, s.shape, 1)
        s = jnp.where(kj <= qi, s, NEG)
    return s


def fox_fwd(name, q, k, v, ct):
    def body(q_ref, k_ref, v_ref, c_ref, o_ref, lse_ref):
        i = pl.program_id(1)
        n_full = i

        def step(jb, carry, masked):
            start = pl.multiple_of(jb * TK, TK)
            out = []
            for h in range(FOX_HEADS):
                m, l, acc = carry[h]
                kb = k_ref[h, pl.ds(start, TK), :]
                vb = v_ref[h, pl.ds(start, TK), :]
                s = _fox_scores(q_ref[h], kb, c_ref[h, pl.ds(jb, 1), :], i * TQ, jb * TK, masked)
                m_new = jnp.maximum(m, jnp.max(s, axis=1, keepdims=True))
                a = jnp.exp(m - m_new)
                p = jnp.exp(s - m_new)
                l = a * l + jnp.sum(p, axis=1, keepdims=True)
                acc = a * acc + jnp.dot(p.astype(BF16), vb, preferred_element_type=F32)
                out.append((m_new, l, acc))
            return tuple(out)

        init = tuple((jnp.full((TQ, 1), NEG, F32), jnp.zeros((TQ, 1), F32), jnp.zeros((TQ, HD), F32))
                     for _ in range(FOX_HEADS))
        carry = lax.fori_loop(0, n_full, lambda jb, c: step(jb, c, False), init)
        carry = step(n_full, carry, True)
        for h in range(FOX_HEADS):
            m, l, acc = carry[h]
            o_ref[h] = (acc / l).astype(BF16)
            lse_ref[h] = jnp.broadcast_to(m + jnp.log(l), (TQ, LANES))

    head = lambda h, i: (h, 0, 0)
    return pl.pallas_call(
        body, name=name, grid=(B_HEADS // FOX_HEADS, T // TQ),
        in_specs=[pl.BlockSpec((FOX_HEADS, TQ, HD), lambda h, i: (h, i, 0)), pl.BlockSpec((FOX_HEADS, T, HD), head),
                  pl.BlockSpec((FOX_HEADS, T, HD), head), pl.BlockSpec((FOX_HEADS, T // TK, TK), head)],
        out_specs=[pl.BlockSpec((FOX_HEADS, TQ, HD), lambda h, i: (h, i, 0)),
                   pl.BlockSpec((FOX_HEADS, TQ, LANES), lambda h, i: (h, i, 0))],
        out_shape=[jax.ShapeDtypeStruct((B_HEADS, T, HD), BF16), jax.ShapeDtypeStruct((B_HEADS, T, LANES), F32)],
        compiler_params=_params("parallel", "parallel"),
    )(q, k, v, ct)


def fox_bwd(name, q, k, v, ct, o, do, lse, prev):
    has_prev = prev is not None
    n_prev = 3 if has_prev else 0
    nq = T // TQ
    nk = T // TK

    def body(*refs):
        q_ref, k_ref, v_ref, c_ref, o_ref, do_ref, lse_ref = refs[:7]
        dq_ref, drow_ref, dk_ref, dv_ref, dc_ref, dq_acc, drow_acc = refs[7 + n_prev:]
        jb = pl.program_id(1)

        @pl.when(jb == 0)
        def _():
            dq_acc[...] = jnp.zeros_like(dq_acc)
            drow_acc[...] = jnp.zeros_like(drow_acc)

        dk_ref[...] = jnp.zeros_like(dk_ref)
        dv_ref[...] = jnp.zeros_like(dv_ref)
        kb = k_ref[...]
        cj = c_ref[pl.ds(jb, 1), :]

        def step(i, dc, masked):
            rows = pl.ds(pl.multiple_of(i * TQ, TQ), TQ)
            qv = q_ref[rows, :]
            dov = do_ref[rows, :]
            delta = jnp.sum(dov.astype(F32) * o_ref[rows, :].astype(F32), axis=1, keepdims=True)
            s = _fox_scores(qv, kb, cj, i * TQ, jb * TK, masked)
            p = jnp.exp(s - lse_ref[rows, 0:1])
            dp = lax.dot_general(dov, v_ref[...], NT, preferred_element_type=F32)
            ds = p * (dp - delta)
            dsb = ds.astype(BF16)
            dv_ref[...] += lax.dot_general(p.astype(BF16), dov, TN, preferred_element_type=F32)
            dk_ref[...] += lax.dot_general(dsb, qv, TN, preferred_element_type=F32)
            dq_acc[rows, :] += jnp.dot(dsb, kb, preferred_element_type=F32)
            drow_acc[rows, :] += jnp.broadcast_to(jnp.sum(ds, axis=1, keepdims=True), (TQ, LANES))
            return dc - jnp.sum(ds, axis=0, keepdims=True)

        dc = step(jb, jnp.zeros((1, TK), F32), True)
        dc = lax.fori_loop(jb + 1, nq, lambda i, c: step(i, c, False), dc)
        dk = dk_ref[...] * SCALE
        if has_prev:
            dk = dk + refs[7][...]
            dv_ref[...] += refs[8][...]
            dc = dc + refs[9][pl.ds(jb, 1), :]
        dk_ref[...] = dk
        dc_ref[pl.ds(jb, 1), :] = dc

        @pl.when(jb == nk - 1)
        def _():
            dq_ref[...] = (dq_acc[...] * SCALE).astype(BF16)
            for i in range(nq):
                drow_ref[i:i + 1, :] = jnp.transpose(drow_acc[i * TQ:(i + 1) * TQ, :])[0:1, :]

    head = lambda h, j: (h, 0, 0)
    full = pl.BlockSpec((None, T, HD), head)
    blk = pl.BlockSpec((None, TK, HD), lambda h, j: (h, j, 0))
    cspec = pl.BlockSpec((None, T // TK, TK), head)
    rspec = pl.BlockSpec((None, nq, TQ), head)
    ins = [q, k, v, ct, o, do, lse] + (list(prev) if has_prev else [])
    in_specs = [full, blk, blk, cspec, full, full, pl.BlockSpec((None, T, LANES), head)] + ([blk, blk, cspec] if has_prev else [])
    dq, drow, dk, dv, dc = pl.pallas_call(
        body, name=name, grid=(B_HEADS, nk), in_specs=in_specs, out_specs=[full, rspec, blk, blk, cspec],
        out_shape=[jax.ShapeDtypeStruct((B_HEADS, T, HD), BF16), jax.ShapeDtypeStruct((B_HEADS, nq, TQ), F32)]
        + [jax.ShapeDtypeStruct((B_HEADS, T, HD), F32)] * 2 + [jax.ShapeDtypeStruct((B_HEADS, nk, TK), F32)],
        scratch_shapes=[pltpu.VMEM((T, HD), F32), pltpu.VMEM((T, LANES), F32)],
        compiler_params=_params("parallel", "arbitrary"),
    )(*ins)
    return dq, drow, (dk, dv, dc)


def _tri(upper):
    i = lax.broadcasted_iota(jnp.int32, (BLK, BLK), 0)
    j = lax.broadcasted_iota(jnp.int32, (BLK, BLK), 1)
    return ((i <= j) if upper else (i >= j)).astype(F32)


def gates_fwd(name, fgt, b_f):
    def body(f_ref, b_ref, c_ref):
        tri = _tri(True)
        carry = jnp.zeros((B_HEADS, 1), F32)
        for blk in range(T // BLK):
            sl = slice(blk * BLK, (blk + 1) * BLK)
            z = f_ref[:, sl] + b_ref[...]
            logf = jnp.minimum(z, 0.0) - jnp.log(1.0 + jnp.exp(-jnp.abs(z)))
            cs = jnp.dot(logf, tri, precision=HIGHEST, preferred_element_type=F32) + carry
            c_ref[:, sl] = cs
            carry = cs[:, BLK - 1:BLK]

    return pl.pallas_call(
        body, name=name, out_shape=jax.ShapeDtypeStruct((B_HEADS, T), F32), compiler_params=_params(),
    )(fgt, b_f)


def gates_bwd(name, fgt, b_f, dcs):
    n_dc = len(dcs)

    def body(*refs):
        f_ref, b_ref = refs[:2]
        dc_refs = refs[2:2 + n_dc]
        dz_ref, db_ref = refs[2 + n_dc:]
        tri = _tri(False)
        carry = jnp.zeros((B_HEADS, 1), F32)
        db = jnp.zeros((B_HEADS, 1), F32)
        for blk in reversed(range(T // BLK)):
            sl = slice(blk * BLK, (blk + 1) * BLK)
            dc = dc_refs[0][:, sl]
            for r in dc_refs[1:]:
                dc = dc + r[:, sl]
            rc = jnp.dot(dc, tri, precision=HIGHEST, preferred_element_type=F32) + carry
            carry = rc[:, 0:1]
            z = f_ref[:, sl] + b_ref[...]
            e = jnp.exp(-jnp.abs(z))
            dz = rc * jnp.where(z >= 0.0, e, 1.0) / (1.0 + e)
            dz_ref[:, sl] = dz
            db = db + jnp.sum(dz, axis=1, keepdims=True)
        db_ref[...] = db

    return pl.pallas_call(
        body, name=name,
        out_shape=[jax.ShapeDtypeStruct((B_HEADS, T), F32), jax.ShapeDtypeStruct((B_HEADS, 1), F32)],
        compiler_params=_params(),
    )(fgt, b_f, *dcs)


CONV_TB = 256
GELU_K = math.sqrt(2.0 / math.pi)
GELU_C = 0.044715


def _shift_down(x, halo_ref, n):
    rows = lax.broadcasted_iota(jnp.int32, x.shape, 0)
    y = pltpu.roll(x, n, 0)
    for k in range(n):
        y = jnp.where(rows == k, halo_ref[pl.ds(8 - n + k, 1), :], y)
    return y


def _conv(x, halo_ref, cw_ref, cb_ref, first):
    x1 = _shift_down(x, halo_ref, 1)
    x2 = _shift_down(x, halo_ref, 2)
    rows = lax.broadcasted_iota(jnp.int32, x.shape, 0)
    x1 = jnp.where(first & (rows < 1), 0.0, x1)
    x2 = jnp.where(first & (rows < 2), 0.0, x2)
    y = x2 * cw_ref[0:1, :] + x1 * cw_ref[1:2, :] + x * cw_ref[2:3, :] + cb_ref[...]
    return y, x1, x2


def _gelu_parts(x):
    th = jnp.tanh(GELU_K * (x + GELU_C * x * x * x))
    val = 0.5 * x * (1.0 + th)
    grad = 0.5 * (1.0 + th) + 0.5 * x * (1.0 - th * th) * GELU_K * (1.0 + 3.0 * GELU_C * x * x)
    return val, grad


def _conv_specs(tb):
    def slab(off):
        return pl.BlockSpec((None, tb, SLAB), lambda d, i: (d + off, i, 0))

    def halo(off):
        return pl.BlockSpec((None, 8, SLAB), lambda d, i: (d + off, jnp.maximum(i * (tb // 8) - 1, 0), 0))

    def par(rows, off):
        return pl.BlockSpec((None, rows, SLAB), lambda d, i: (d + off, 0, 0))

    return slab, halo, par


def convglu_fwd(name, a, cw, cb, tb=CONV_TB):
    slab, halo, par = _conv_specs(tb)

    def body(ag, hg, av, hv, cwg, cbg, cwv, cbv, u_ref):
        first = pl.program_id(1) == 0
        gate, _, _ = _conv(ag[...], hg, cwg, cbg, first)
        val, _, _ = _conv(av[...], hv, cwv, cbv, first)
        u_ref[...] = (_gelu_parts(gate)[0] * val).astype(BF16)

    return pl.pallas_call(
        body, name=name, grid=(4, T // tb),
        in_specs=[slab(0), halo(0), slab(4), halo(4), par(3, 0), par(1, 0), par(3, 4), par(1, 4)],
        out_specs=slab(0), out_shape=jax.ShapeDtypeStruct((4, T, SLAB), BF16),
        compiler_params=_params("parallel", "parallel"),
    )(a, a, a, a, cw, cb, cw, cb)


def convglu_bwd(name, du, a, cw, cb, tb=CONV_TB):
    nblk = T // tb

    def rev(i):
        return nblk - 1 - i

    def pair(rows, idx):
        return pl.BlockSpec((2, None, rows, SLAB), lambda d, i: (0, d, idx(i), 0))

    def body(du_ref, a_ref, h_ref, cw_ref, cb_ref, da_ref, dcw_ref, dcb_ref, nxt_ref):
        step = pl.program_id(1)
        first = step == nblk - 1

        @pl.when(step == 0)
        def _():
            nxt_ref[...] = jnp.zeros_like(nxt_ref)

        gate, g1, g2 = _conv(a_ref[0], h_ref.at[0], cw_ref.at[0], cb_ref.at[0], first)
        val, v1, v2 = _conv(a_ref[1], h_ref.at[1], cw_ref.at[1], cb_ref.at[1], first)
        act, dact = _gelu_parts(gate)
        duv = du_ref[...].astype(F32)
        dys = (duv * val * dact, duv * act)
        rows = lax.broadcasted_iota(jnp.int32, duv.shape, 0)
        for k, (dy, xs) in enumerate(((dys[0], (g2, g1, a_ref[0])), (dys[1], (v2, v1, a_ref[1])))):
            def up(n):
                y = pltpu.roll(dy, tb - n, 0)
                for r in range(n):
                    y = jnp.where(rows == tb - n + r, nxt_ref[k, pl.ds(r, 1), :], y)
                return y

            da_ref[k] = (dy * cw_ref[k, 2:3, :] + up(1) * cw_ref[k, 1:2, :] + up(2) * cw_ref[k, 0:1, :]).astype(BF16)
            parts = [jnp.sum(dy * x, axis=0, keepdims=True) for x in xs]
            bias = jnp.sum(dy, axis=0, keepdims=True)

            @pl.when(step == 0)
            def _():
                for r in range(3):
                    dcw_ref[k, r:r + 1, :] = parts[r]
                dcb_ref[k] = bias

            @pl.when(step > 0)
            def _():
                for r in range(3):
                    dcw_ref[k, r:r + 1, :] += parts[r]
                dcb_ref[k] += bias
        for k in range(2):
            nxt_ref[k] = dys[k][0:8]

    da, dcw, dcb = pl.pallas_call(
        body, name=name, grid=(4, nblk),
        in_specs=[pl.BlockSpec((None, tb, SLAB), lambda d, i: (d, rev(i), 0)), pair(tb, rev),
                  pair(8, lambda i: jnp.maximum(rev(i) * (tb // 8) - 1, 0)), pair(3, lambda i: 0), pair(1, lambda i: 0)],
        out_specs=[pair(tb, rev), pair(3, lambda i: 0), pair(1, lambda i: 0)],
        out_shape=[jax.ShapeDtypeStruct((2, 4, T, SLAB), BF16), jax.ShapeDtypeStruct((2, 4, 3, SLAB), F32),
                   jax.ShapeDtypeStruct((2, 4, 1, SLAB), F32)],
        scratch_shapes=[pltpu.VMEM((2, 8, SLAB), F32)],
        compiler_params=_params("parallel", "arbitrary"),
    )(du, a.reshape(2, 4, T, SLAB), a.reshape(2, 4, T, SLAB), cw.reshape(2, 4, 3, SLAB), cb.reshape(2, 4, 1, SLAB))
    return da.reshape(NDEV, T, SLAB), dcw.reshape(NDEV, 3, SLAB), dcb.reshape(NDEV, 1, SLAB)


def loss_head(name, y, target, tb=256):
    row = pl.BlockSpec((tb, D), lambda i: (i, 0))

    def body(y_ref, t_ref, dy_ref, loss_ref):
        diff = y_ref[...] - t_ref[...]
        dy_ref[...] = diff * (1.0 / D)
        part = jnp.sum(jnp.sum(diff * diff, axis=1, keepdims=True), axis=0, keepdims=True) * (0.5 / D)

        @pl.when(pl.program_id(0) == 0)
        def _():
            loss_ref[...] = part

        @pl.when(pl.program_id(0) > 0)
        def _():
            loss_ref[...] += part

    return pl.pallas_call(
        body, name=name, grid=(T // tb,), in_specs=[row, row],
        out_specs=[row, pl.BlockSpec((1, 1), lambda i: (0, 0))],
        out_shape=[jax.ShapeDtypeStruct((T, D), F32), jax.ShapeDtypeStruct((1, 1), F32)],
        compiler_params=_params("arbitrary"),
    )(y, target)


def _row_tile(rows, cols, bytes_per_elem, budget=6 * 1024 * 1024):
    for tr in (1024, 512, 256, 128, 64, 32, 16, 8):
        if rows % tr == 0 and tr * cols * bytes_per_elem <= budget:
            return tr
    return rows


def adamw(name, parts, w, m, v, row0=0, prev=None):
    n_parts, rows, cols = parts.shape
    rows_all = w.shape[0]
    tr = _row_tile(math.gcd(rows, row0) if row0 else rows, cols, n_parts * parts.dtype.itemsize + 28)
    blk = pl.BlockSpec((tr, cols), lambda i: (row0 // tr + i, 0))
    b1c = 1.0 - ADAM_B1 ** ADAM_STEP
    b2c = 1.0 - ADAM_B2 ** ADAM_STEP
    n_prev = 0 if prev is None else 4

    def body(p_ref, w_ref, m_ref, v_ref, *rest):
        g_ref, d_ref, nm_ref, nv_ref = rest[n_prev:]
        g = p_ref[0].astype(F32)
        for k in range(1, n_parts):
            g = g + p_ref[k].astype(F32)
        nm = ADAM_B1 * m_ref[...] + (1.0 - ADAM_B1) * g
        nv = ADAM_B2 * v_ref[...] + (1.0 - ADAM_B2) * (g * g)
        g_ref[...] = g
        nm_ref[...] = nm
        nv_ref[...] = nv
        d_ref[...] = -ADAM_LR * ((nm / b1c) / (jnp.sqrt(nv / b2c) + ADAM_EPS) + ADAM_WD * w_ref[...])

    return pl.pallas_call(
        body, name=name, grid=(rows // tr,),
        in_specs=[pl.BlockSpec((n_parts, tr, cols), lambda i: (0, i, 0)), blk, blk, blk]
        + [pl.BlockSpec(memory_space=pl.ANY)] * n_prev,
        out_specs=[blk] * 4, out_shape=[jax.ShapeDtypeStruct((rows_all, cols), F32)] * 4,
        input_output_aliases={4 + k: k for k in range(n_prev)}, compiler_params=_params("parallel"),
    )(parts, w, m, v, *(prev or []))


def _peer(rel, x, y, c):
    return (1 - x if rel & 4 else x, 1 - y if rel & 2 else y, 1 - c if rel & 1 else c)


def _split_copies(ins, lands, send_sems, recv_sems, scatter):
    x, y, c = lax.axis_index("x"), lax.axis_index("y"), lax.axis_index("c")
    me = 4 * x + 2 * y + c
    copies = []
    for t in range(len(ins)):
        for rel in range(1, NDEV):
            px, py, pc = _peer(rel, x, y, c)
            src = ins[t].at[4 * px + 2 * py + pc] if scatter else ins[t]
            copies.append(pltpu.make_async_remote_copy(
                src_ref=src, dst_ref=lands[t].at[me], send_sem=send_sems.at[t * (NDEV - 1) + rel - 1],
                recv_sem=recv_sems.at[t * (NDEV - 1) + rel - 1], device_id=(px, py, pc),
                device_id_type=pl.DeviceIdType.MESH))
    return me, copies


def exchange_start(name, items, scatter, dep=None):
    n = len(items)
    hbm = pl.BlockSpec(memory_space=pltpu.HBM)
    sem = pl.BlockSpec(memory_space=pltpu.SEMAPHORE)
    has_dep = dep is not None
    land_shapes = [it.shape if scatter else (NDEV,) + it.shape for it in items]

    def body(*refs):
        ins, lands = refs[:n], refs[n:2 * n]
        outs = refs[2 * n + has_dep:]
        send_sems, recv_sems, token = outs[0], outs[1], outs[2 + 2 * n]
        _, copies = _split_copies(ins, lands, send_sems, recv_sems, scatter)
        for cp in copies:
            cp.start()
        token[...] = jnp.zeros_like(token)

    sems = pltpu.SemaphoreType.DMA((n * (NDEV - 1),))
    out_shape = ([sems, sems] + [pltpu.HBM(it.shape, it.dtype) for it in items]
                 + [pltpu.HBM(sh, it.dtype) for sh, it in zip(land_shapes, items)] + [jax.ShapeDtypeStruct((8, LANES), F32)])
    operands = ([pltpu.with_memory_space_constraint(it, pltpu.HBM) for it in items]
                + [pltpu.with_memory_space_constraint(lax.empty(sh, it.dtype), pltpu.HBM) for sh, it in zip(land_shapes, items)]
                + ([dep] if has_dep else []))
    res = pl.pallas_call(
        body, name=name, in_specs=[hbm] * (2 * n) + [pl.BlockSpec(memory_space=pl.ANY)] * has_dep,
        out_specs=[sem, sem] + [hbm] * (2 * n) + [pl.BlockSpec(memory_space=pltpu.VMEM)], out_shape=out_shape,
        input_output_aliases={t: 2 + t for t in range(2 * n)},
        compiler_params=pltpu.CompilerParams(has_side_effects=pltpu.SideEffectType.DATAFLOW_SIDE_EFFECTING),
    )(*operands)
    return (res[0], res[1], list(res[2:2 + n]), list(res[2 + n:2 + 2 * n]), scatter), res[2 + 2 * n]


def exchange_wait(name, handle, after):
    send_sems, recv_sems, ins, lands, scatter = handle
    n = len(ins)
    after = list(after) if isinstance(after, (list, tuple)) else [after]
    hbm = pl.BlockSpec(memory_space=pltpu.HBM)
    sem = pl.BlockSpec(memory_space=pltpu.SEMAPHORE)

    def body(*refs):
        _, copies = _split_copies(refs[:n], refs[n:2 * n], refs[2 * n], refs[2 * n + 1], scatter)
        for cp in copies:
            cp.wait_send()
            cp.wait_recv()

    res = pl.pallas_call(
        body, name=name, in_specs=[hbm] * (2 * n) + [sem, sem] + [pl.BlockSpec(memory_space=pl.ANY)] * len(after),
        out_specs=[hbm] * (2 * n), out_shape=[pltpu.HBM(a.shape, a.dtype) for a in ins + lands],
        input_output_aliases={t: t for t in range(2 * n)},
        compiler_params=pltpu.CompilerParams(has_side_effects=pltpu.SideEffectType.DATAFLOW_SIDE_EFFECTING),
    )(*ins, *lands, send_sems, recv_sems, *after)
    me = 4 * lax.axis_index("x") + 2 * lax.axis_index("y") + lax.axis_index("c")
    out = []
    for src, landed in zip(res[:n], res[n:]):
        own = lax.dynamic_index_in_dim(src, me, axis=0, keepdims=True) if scatter else src[None]
        out.append(lax.dynamic_update_slice_in_dim(landed, own, me, axis=0))
    return out


def sum_slots(name, parts):
    _, rows, cols = parts.shape

    def body(p_ref, o_ref):
        s = p_ref[0]
        for k in range(1, NDEV):
            s = s + p_ref[k]
        o_ref[...] = s

    return pl.pallas_call(body, name=name, out_shape=jax.ShapeDtypeStruct((rows, cols), F32), compiler_params=_params())(parts)


def _heads(t):
    return t.reshape(T, B_HEADS, HD).transpose(1, 0, 2)


def _unheads(t):
    return t.transpose(1, 0, 2).reshape(T, B_HEADS * HD)


def _cols_from_slots(g):
    return g.transpose(1, 0, 2).reshape(g.shape[1], NDEV * g.shape[2])


def _slots_from_cols(w):
    return w.reshape(w.shape[0], NDEV, w.shape[1] // NDEV).transpose(1, 0, 2)


def _pack(arrays, rows):
    flat = jnp.concatenate([a.reshape(-1).astype(F32) for a in arrays])
    return jnp.pad(flat, (0, rows * LANES - flat.shape[0])).reshape(rows, LANES)


def _unpack(buf, shapes):
    flat = buf.reshape(-1)
    out, pos = [], 0
    for sh in shapes:
        size = math.prod(sh)
        out.append(flat[pos:pos + size].reshape(sh))
        pos += size
    return out


def kernel(x, norm_gains, w_qkv_a, w_o_a, w_q_b, w_o_b, kv_norm, w_kvf, b_f, w_up, conv_w, conv_b, w_down, loss_target, m_norm_gains, m_w_qkv_a, m_w_o_a, m_w_q_b, m_w_o_b, m_kv_norm, m_w_kvf, m_b_f, m_w_up, m_conv_w, m_conv_b, m_w_down, v_norm_gains, v_w_qkv_a, v_w_o_a, v_w_q_b, v_w_o_b, v_kv_norm, v_w_kvf, v_b_f, v_w_up, v_conv_w, v_conv_b, v_w_down):
    me = 4 * lax.axis_index("x") + 2 * lax.axis_index("y") + lax.axis_index("c")
    n_b = DEPTH - N_A

    def bf(a):
        return a.astype(BF16)

    mixer_w = [[bf(w_qkv_a[l]), bf(w_o_a[l])] if l < N_A else [bf(w_q_b[l - N_A]), bf(w_o_b[l - N_A])] for l in range(DEPTH)]
    ffn_w = [[bf(w_up[l]), bf(w_down[l])] for l in range(DEPTH)]
    mixer_w[0] += [norm_gains, conv_w]
    mixer_w[N_A] += [bf(w_kvf)]
    handles, tok = [], None
    for l in range(DEPTH):
        for part, items in (("mixer", mixer_w[l]), ("ffn", ffn_w[l])):
            hd, tok = exchange_start(f"gather_start_{part}{l}", items, False, dep=tok)
            handles.append(hd)

    wqkv, woa, wqb, wob, wup, wdown = {}, {}, {}, {}, {}, {}

    def take_mixer(l, arrived):
        if l < N_A:
            wqkv[l] = _cols_from_slots(arrived[0])
            woa[l] = _cols_from_slots(arrived[1])
        else:
            wqb[l - N_A] = arrived[0].reshape(D, D)
            wob[l - N_A] = arrived[1].reshape(D, D)

    def take_ffn(l, arrived):
        wup[l] = arrived[0]
        wdown[l] = arrived[1].reshape(4, SLAB, D)

    arrived = exchange_wait("gather_wait_mixer0", handles[0], tok)
    take_mixer(0, arrived)
    gains = arrived[2].transpose(1, 2, 0, 3).reshape(DEPTH, 4, D)
    cws = [arrived[3][:, l] for l in range(DEPTH)]
    cbs = [conv_b[l].reshape(NDEV, 1, SLAB) for l in range(DEPTH)]
    tables = rope_tables()
    b_col = b_f.reshape(B_HEADS, 1)

    h = x.reshape(T, D)
    _, (xn,) = resid_norm("norm_in", h, None, None, [gains[0, 0]])
    saved = []
    shared = None
    for l in range(DEPTH):
        s = {"h": h, "xn": xn}
        if l >= 1:
            arrived = exchange_wait(f"gather_wait_mixer{l}", handles[2 * l], h)
            take_mixer(l, arrived[:2])
            if l == N_A:
                wkvf = jnp.pad(_cols_from_slots(arrived[2]), ((0, 0), (0, KVF_PAD - KVF)))
        if l < N_A:
            qkv = matmul(f"qkv{l}", xn, wqkv[l], tm=T, tn=768)
            s["qkvr"] = rope(f"rope{l}", qkv, tables, inverse=False)
            os_, lses = [], []
            for g, (_, r) in enumerate(A_GROUPS):
                o, lse = attn_a_fwd(f"attn_a{l}_{g}", s["qkvr"], g, r)
                os_.append(o)
                lses.append(lse)
            s["o"], s["lse"] = os_, lses
            s["oc"] = combine_fwd(f"combine{l}", os_, lses)
            mix = matmul(f"wo_a{l}", s["oc"], woa[l], tm=1024, tn=512)
        else:
            if l == N_A:
                kvf = matmul("kvf", xkv, wkvf, tm=1024, tn=768)
                k_h = _heads(kvf[:, :D].astype(BF16))
                v_h = _heads(kvf[:, D:2 * D].astype(BF16))
                fgt = kvf[:, 2 * D:KVF].T
                ct = gates_fwd("gates", fgt, b_col).reshape(B_HEADS, T // TK, TK)
                shared = {"xkv": xkv, "k": k_h, "v": v_h, "fgt": fgt, "ct": ct, "h": h}
            j = l - N_A
            s["q"] = _heads(matmul(f"wq_b{j}", xn, wqb[j], out_dtype=BF16, tm=1024, tn=512))
            s["o"], s["lse"] = fox_fwd(f"fox{j}", s["q"], shared["k"], shared["v"], shared["ct"])
            s["oc"] = _unheads(s["o"])
            mix = matmul(f"wo_b{j}", s["oc"], wob[j], tm=1024, tn=512)
        s["mix"] = mix
        take_ffn(l, exchange_wait(f"gather_wait_ffn{l}", handles[2 * l + 1], mix))
        s["h1"], (s["xn2"],) = resid_norm(f"norm_mid{l}", h, mix, gains[l, 1], [gains[l, 2]])
        s["a"] = matmul(f"up{l}", s["xn2"], wup[l], tm=T, tn=SLAB, batch="b_out")
        s["u"] = convglu_fwd(f"convglu{l}", s["a"], cws[l], cbs[l])
        s["f"] = matmul(f"down{l}", s["u"], wdown[l], tm=1024, tn=1024, batch="reduce")
        nxt = [gains[l + 1, 0]] if l + 1 < DEPTH else []
        if l == N_A - 1:
            nxt.append(kv_norm)
        h, normed = resid_norm(f"norm_out{l}", s["h1"], s["f"], gains[l, 3], nxt)
        if l + 1 < DEPTH:
            xn = normed[0]
        if l == N_A - 1:
            xkv = normed[1]
        saved.append(s)

    dh, loss_part = loss_head("loss", h, loss_target.reshape(T, D))

    d_gains = [[None] * 4 for _ in range(DEPTH)]
    d_cw, d_cb = [None] * DEPTH, [None] * DEPTH
    gw = {"qkv": [None] * N_A, "oa": [None] * N_A, "qb": [None] * n_b, "ob": [None] * n_b, "up": [None] * DEPTH,
          "down": [None] * DEPTH}
    kv_acc = None
    d_rows = []
    sent = []
    tok = None

    def slots_rows(g):
        return g.reshape(NDEV, g.shape[0] // NDEV, g.shape[1])

    for l in reversed(range(DEPTH)):
        s = saved[l]
        df, d_gains[l][3] = rms_bwd(f"bwd_norm_out{l}", s["f"], gains[l, 3], dh, out_dtype=BF16, dep=tok)
        du = matmul(f"bwd_down_x{l}", df, wdown[l], tb=True, out_dtype=BF16, tm=T, tn=SLAB, batch="b_out")
        gw["down"][l] = matmul(f"bwd_down_w{l}", s["u"], df, ta=True, out_dtype=BF16, tm=SLAB, tn=1024, batch="a_out")
        da, d_cw[l], d_cb[l] = convglu_bwd(f"bwd_convglu{l}", du, s["a"], cws[l], cbs[l])
        dxn2 = matmul(f"bwd_up_x{l}", da, wup[l], tb=True, tm=1024, tn=1024, batch="reduce")
        gw["up"][l] = matmul(f"bwd_up_w{l}", s["xn2"], da, ta=True, out_dtype=BF16, tm=1024, tn=SLAB, batch="b_out")
        dh1, d_gains[l][2] = rms_bwd(f"bwd_norm_mid{l}", s["h1"], gains[l, 2], dxn2, add=dh)
        ffn_items = [gw["up"][l], slots_rows(gw["down"][l].reshape(DFF, D))]
        ffn_what = [("up", l), ("down", l)]
        tok = None
        if l == 0:
            hd, tok = exchange_start("scatter_start_ffn0", ffn_items, True)
            sent.append((hd, ffn_what))
            ffn_items, ffn_what = [], []
        dmix, d_gains[l][1] = rms_bwd(f"bwd_norm_mix{l}", s["mix"], gains[l, 1], dh1, out_dtype=BF16, dep=tok)
        if l < N_A:
            doc = matmul(f"bwd_wo_a_x{l}", dmix, woa[l], tb=True, tm=1024, tn=A_W)
            gw["oa"][l] = matmul(f"bwd_wo_a_w{l}", s["oc"], dmix, ta=True, out_dtype=BF16, tm=A_W, tn=512)
            dos, dds = combine_bwd(f"bwd_combine{l}", doc, s["o"], s["lse"])
            cols = [None] * 9
            for g, (_, r) in enumerate(A_GROUPS):
                dq, dk, dv = attn_a_bwd(f"bwd_attn_a{l}_{g}", s["qkvr"], dos[g], s["lse"][g], dds[g], g, r)
                cols[g], cols[3 + g], cols[6 + g] = dq, dk, dv
            dqkv = rope(f"bwd_rope{l}", jnp.concatenate(cols, axis=1), tables, inverse=True)
            dxn = matmul(f"bwd_qkv_x{l}", dqkv, wqkv[l], tb=True, tm=1024, tn=1024)
            gw["qkv"][l] = matmul(f"bwd_qkv_w{l}", s["xn"], dqkv, ta=True, out_dtype=BF16, tm=512, tn=768)
        else:
            j = l - N_A
            do = _heads(matmul(f"bwd_wo_b_x{j}", dmix, wob[j], tb=True, out_dtype=BF16, tm=1024, tn=512))
            gw["ob"][j] = matmul(f"bwd_wo_b_w{j}", s["oc"], dmix, ta=True, out_dtype=BF16, tm=512, tn=512)
            args = (s["q"], shared["k"], shared["v"], shared["ct"], s["o"], do, s["lse"])
            dq_h, drow, kv_acc = fox_bwd(f"bwd_fox{j}", *args, kv_acc)
            dq = _unheads(dq_h)
            d_rows.append(drow.reshape(B_HEADS, T))
            dxn = matmul(f"bwd_wq_b_x{j}", dq, wqb[j], tb=True, tm=1024, tn=512)
            gw["qb"][j] = matmul(f"bwd_wq_b_w{j}", s["xn"], dq, ta=True, out_dtype=BF16, tm=512, tn=512)
        dh, d_gains[l][0] = rms_bwd(f"bwd_norm_in{l}", s["h"], gains[l, 0], dxn, add=dh1)
        if l == N_A:
            dk_h, dv_h, dct = kv_acc
            dfgt, d_bf = gates_bwd("bwd_gates", shared["fgt"], b_col, [dct.reshape(B_HEADS, T)] + d_rows)
            dkvf = jnp.concatenate(
                [_unheads(dk_h).astype(BF16), _unheads(dv_h).astype(BF16), dfgt.T.astype(BF16),
                 jnp.zeros((T, KVF_PAD - KVF), BF16)], axis=1)
            dxkv = matmul("bwd_kvf_x", dkvf, wkvf, tb=True, tm=1024, tn=512)
            g_kvf_full = matmul("bwd_kvf_w", shared["xkv"], dkvf, ta=True, out_dtype=BF16, tm=512, tn=768)
            dh, d_kvn = rms_bwd("bwd_norm_kv", shared["h"], kv_norm, dxkv, add=dh)
        if l < N_A:
            items = [_slots_from_cols(gw["qkv"][l]), _slots_from_cols(gw["oa"][l])]
            what = [("qkv", l), ("oa", l)]
        else:
            items = [slots_rows(gw["qb"][l - N_A]), slots_rows(gw["ob"][l - N_A])]
            what = [("qb", l - N_A), ("ob", l - N_A)]
        if l == N_A:
            items.append(_slots_from_cols(g_kvf_full[:, :KVF]))
            what.append(("kvf", 0))
        if l > 0:
            hd, tok = exchange_start(f"scatter_start{l}", items + ffn_items, True)
            sent.append((hd, what + ffn_what))

    small_shapes = [(DEPTH, 4, D), (D,), (B_HEADS,), (DEPTH, 3, NDEV * SLAB), (DEPTH, NDEV * SLAB), (1,)]
    small = [
        jnp.stack([jnp.concatenate(row, axis=0) for row in d_gains]),
        d_kvn, d_bf,
        jnp.stack([d.transpose(1, 0, 2).reshape(3, NDEV * SLAB) for d in d_cw]),
        jnp.stack([d.reshape(NDEV * SLAB) for d in d_cb]),
        loss_part,
    ]
    small_rows = 848
    small_handle, tok = exchange_start("small_start", [_pack(small, small_rows)], False)
    last_handle, tok = exchange_start("scatter_start0", items, True, dep=tok)
    last_what = what

    big = {"qkv": (w_qkv_a, m_w_qkv_a, v_w_qkv_a), "oa": (w_o_a, m_w_o_a, v_w_o_a), "qb": (w_q_b, m_w_q_b, v_w_q_b),
           "ob": (w_o_b, m_w_o_b, v_w_o_b), "kvf": (w_kvf, m_w_kvf, v_w_kvf), "up": (w_up, m_w_up, v_w_up),
           "down": (w_down, m_w_down, v_w_down)}
    updated = {name: None for name in big}

    def collect(handle, what, after):
        received = exchange_wait(f"scatter_wait_{what[0][0]}{what[0][1]}", handle, after)
        for (name, layer), rec in zip(what, received):
            cols = rec.shape[-1]
            rows = rec.size // (NDEV * cols)
            w, m, v = (a.reshape(-1, cols) for a in big[name])
            updated[name] = adamw(f"adamw_{name}{layer}", rec.reshape(NDEV, rows, cols), w, m, v, row0=layer * rows,
                                  prev=updated[name])
        return [updated[name][0] for name, _ in what]

    after = [tok]
    for handle, what in sent:
        after = collect(handle, what, after)

    (small_all,) = exchange_wait("small_wait", small_handle, after)
    g_gains_full, g_kvn, g_bf, g_cw_full, g_cb, loss = _unpack(sum_slots("sum_small", small_all), small_shapes)
    g_gains_mine = lax.dynamic_slice_in_dim(g_gains_full, me * (D // NDEV), D // NDEV, axis=2)
    g_cw_mine = lax.dynamic_slice_in_dim(g_cw_full, me * SLAB, SLAB, axis=2)

    small_w = [norm_gains, kv_norm, b_f, conv_w, conv_b]
    small_m = [m_norm_gains, m_kv_norm, m_b_f, m_conv_w, m_conv_b]
    small_v = [v_norm_gains, v_kv_norm, v_b_f, v_conv_w, v_conv_b]
    small_g = [g_gains_mine, g_kvn, g_bf, g_cw_mine, g_cb]
    shapes = [w.shape for w in small_w]
    rows = 320
    res = adamw("adamw_small", _pack(small_g, rows)[None], _pack(small_w, rows), _pack(small_m, rows), _pack(small_v, rows))
    _, s_delta, s_m, s_v = [_unpack(r, shapes) for r in res]

    collect(last_handle, last_what, [s_delta[0]])
    big_out =[[r.reshape(big[name][0].shape) for r in updated[name]] for name in ("qkv", "oa", "qb", "ob", "kvf", "up", "down")]

    def pick(k):
        b = [o[k] for o in big_out]
        sm = {0: small_g, 1: s_delta, 2: s_m, 3: s_v}[k]
        return [sm[0], b[0], b[1], b[2], b[3], sm[1], b[4], sm[2], b[5], sm[3], sm[4], b[6]]

    return (loss.reshape(()), dh.reshape(1, T, D), *pick(0), *pick(1), *pick(2), *pick(3))
```

```python
import functools
import math

import jax
import jax.numpy as jnp
from jax import lax
from jax.experimental import pallas as pl
from jax.experimental.pallas import tpu as pltpu

F32 = jnp.float32
BF16 = jnp.bfloat16

T = 2048
D = 1024
DEPTH = 4
N_A = 2
HD = 64
A_GROUPS = ((128, 1), (512, 4), (2048, 16))
A_W = 768
B_HEADS = 16
DFF = 2816
NDEV = 8
SLAB = 2 * DFF // NDEV
KVF = 2 * D + B_HEADS
KVF_PAD = 2304
ROPE_DIM = 16
ROPE_THETA = 500000.0
EPS = 1e-6
NEG = -1e30
BLK = 128
LANES = 128
SCALE = HD ** -0.5
VMEM_LIMIT = 56 * 1024 * 1024

ADAM_LR = 0.001
ADAM_B1 = 0.9
ADAM_B2 = 0.999
ADAM_EPS = 1e-08
ADAM_WD = 0.01
ADAM_STEP = 10
HIGHEST = lax.Precision.HIGHEST


def _params(*sem):
    return pltpu.CompilerParams(dimension_semantics=sem or None, vmem_limit_bytes=VMEM_LIMIT)


def _bf(x):
    return x if x.dtype == BF16 else x.astype(BF16)


def matmul(name, a, b, *, ta=False, tb=False, out_dtype=F32, tm=512, tn=512, batch=None):
    a_b = batch in ("a_out", "reduce")
    b_b = batch in ("b_out", "reduce")
    o_b = batch in ("a_out", "b_out")
    nb = a.shape[0] if a_b else (b.shape[0] if b_b else 1)
    ash = a.shape[1:] if a_b else a.shape
    bsh = b.shape[1:] if b_b else b.shape
    m, k = (ash[1], ash[0]) if ta else ash
    k2, n = (bsh[1], bsh[0]) if tb else bsh
    assert k == k2, (name, a.shape, b.shape)
    tm, tn = min(tm, m), min(tn, n)
    assert m % tm == 0 and n % tn == 0, (name, m, n, tm, tn)
    nbr = nb if batch == "reduce" else 1
    grid = (nb if o_b else 1, n // tn, m // tm, nbr)

    def bidx(bo, br):
        return bo if o_b else br

    def spec(batched, block, idx):
        if batched:
            return pl.BlockSpec((None,) + block, lambda bo, j, i, br: (bidx(bo, br),) + idx(i, j))
        return pl.BlockSpec(block, lambda bo, j, i, br: idx(i, j))

    a_spec = spec(a_b, (k, tm) if ta else (tm, k), (lambda i, j: (0, i)) if ta else (lambda i, j: (i, 0)))
    b_spec = spec(b_b, (tn, k) if tb else (k, tn), (lambda i, j: (j, 0)) if tb else (lambda i, j: (0, j)))
    o_spec = spec(o_b, (tm, tn), lambda i, j: (i, j))
    dims = (((0 if ta else 1,), (1 if tb else 0,)), ((), ()))

    def body(a_ref, b_ref, o_ref, *acc):
        p = lax.dot_general(_bf(a_ref[...]), _bf(b_ref[...]), dims, preferred_element_type=F32)
        if nbr == 1:
            o_ref[...] = p.astype(out_dtype)
        else:
            r = pl.program_id(3)

            @pl.when(r == 0)
            def _():
                acc[0][...] = p

            @pl.when(r > 0)
            def _():
                acc[0][...] += p

            @pl.when(r == nbr - 1)
            def _():
                o_ref[...] = acc[0][...].astype(out_dtype)

    out_shape = ((nb,) if o_b else ()) + (m, n)
    return pl.pallas_call(
        body, name=name, grid=grid, in_specs=[a_spec, b_spec], out_specs=o_spec,
        out_shape=jax.ShapeDtypeStruct(out_shape, out_dtype),
        scratch_shapes=[pltpu.VMEM((tm, tn), F32)] if nbr > 1 else [],
        compiler_params=_params("parallel", "parallel", "parallel", "arbitrary"),
    )(a, b)


def _rms(x, g):
    return x * lax.rsqrt(jnp.mean(x * x, axis=-1, keepdims=True) + EPS) * g


def resid_norm(name, h, y, gy, gains, tb=256, dep=None):
    n_g = len(gains)
    has_y = y is not None
    has_dep = dep is not None
    row = pl.BlockSpec((tb, D), lambda i: (i, 0))
    vec = pl.BlockSpec((1, D), lambda i: (0, 0))

    def body(*refs):
        h_ref = refs[0]
        pos = 1
        hn = h_ref[...]
        if has_y:
            hn = hn + _rms(refs[1][...], refs[2][...])
            pos = 3
        g_refs = refs[pos:pos + n_g]
        outs = refs[pos + n_g + has_dep:]
        if has_y:
            outs[0][...] = hn
            outs = outs[1:]
        for g_ref, o_ref in zip(g_refs, outs):
            o_ref[...] = _rms(hn, g_ref[...]).astype(BF16)

    ins = [h] + ([y, gy.reshape(1, D)] if has_y else []) + [g.reshape(1, D) for g in gains] + ([dep] if has_dep else [])
    in_specs = [row] + ([row, vec] if has_y else []) + [vec] * n_g + [pl.BlockSpec(memory_space=pl.ANY)] * has_dep
    out_shape = ([jax.ShapeDtypeStruct((T, D), F32)] if has_y else []) + [jax.ShapeDtypeStruct((T, D), BF16)] * n_g
    res = pl.pallas_call(
        body, name=name, grid=(T // tb,), in_specs=in_specs, out_specs=[row] * len(out_shape),
        out_shape=out_shape, compiler_params=_params("parallel"),
    )(*ins)
    return (res[0], list(res[1:])) if has_y else (h, list(res))


def rms_bwd(name, x, g, dy, add=None, out_dtype=F32, tb=256, dep=None):
    has_add = add is not None
    has_dep = dep is not None
    row = pl.BlockSpec((tb, D), lambda i: (i, 0))
    vec = pl.BlockSpec((1, D), lambda i: (0, 0))

    def body(*refs):
        x_ref, g_ref, dy_ref = refs[:3]
        dx_ref, dg_ref = refs[-2:]
        xv = x_ref[...]
        dyv = dy_ref[...].astype(F32)
        r = lax.rsqrt(jnp.mean(xv * xv, axis=-1, keepdims=True) + EPS)
        gdy = dyv * g_ref[...]
        dx = r * gdy - xv * (r * r * r * jnp.mean(xv * gdy, axis=-1, keepdims=True))
        if has_add:
            dx = dx + refs[3][...]
        dx_ref[...] = dx.astype(out_dtype)
        part = jnp.sum(dyv * xv * r, axis=0, keepdims=True)

        @pl.when(pl.program_id(0) == 0)
        def _():
            dg_ref[...] = part

        @pl.when(pl.program_id(0) > 0)
        def _():
            dg_ref[...] += part

    ins = [x, g.reshape(1, D), dy] + ([add] if has_add else []) + ([dep] if has_dep else [])
    return pl.pallas_call(
        body, name=name, grid=(T // tb,),
        in_specs=[row, vec, row] + ([row] if has_add else []) + [pl.BlockSpec(memory_space=pl.ANY)] * has_dep,
        out_specs=[row, vec],
        out_shape=[jax.ShapeDtypeStruct((T, D), out_dtype), jax.ShapeDtypeStruct((1, D), F32)],
        compiler_params=_params("arbitrary"),
    )(*ins)


def _rms_cotangent(xv, g, dyv):
    r = lax.rsqrt(jnp.mean(xv * xv, axis=-1, keepdims=True) + EPS)
    gdy = dyv * g
    dx = r * gdy - xv * (r * r * r * jnp.mean(xv * gdy, axis=-1, keepdims=True))
    return dx, jnp.sum(dyv * xv * r, axis=0, keepdims=True)


def rms_bwd_chain(name, x, g, dy, add, x2, g2, tb=256, dep=None):
    has_dep = dep is not None
    row = pl.BlockSpec((tb, D), lambda i: (i, 0))
    vec = pl.BlockSpec((1, D), lambda i: (0, 0))

    def body(x_ref, g_ref, dy_ref, add_ref, x2_ref, g2_ref, *rest):
        d1_ref, dg_ref, d2_ref, dg2_ref = rest[has_dep:]
        d1, part = _rms_cotangent(x_ref[...], g_ref[...], dy_ref[...].astype(F32))
        d1 = d1 + add_ref[...]
        d1_ref[...] = d1
        d2, part2 = _rms_cotangent(x2_ref[...], g2_ref[...], d1)
        d2_ref[...] = d2.astype(BF16)

        @pl.when(pl.program_id(0) == 0)
        def _():
            dg_ref[...] = part
            dg2_ref[...] = part2

        @pl.when(pl.program_id(0) > 0)
        def _():
            dg_ref[...] += part
            dg2_ref[...] += part2

    ins = [x, g.reshape(1, D), dy, add, x2, g2.reshape(1, D)] + ([dep] if has_dep else [])
    return pl.pallas_call(
        body, name=name, grid=(T // tb,),
        in_specs=[row, vec, row, row, row, vec] + [pl.BlockSpec(memory_space=pl.ANY)] * has_dep,
        out_specs=[row, vec, row, vec],
        out_shape=[jax.ShapeDtypeStruct((T, D), F32), jax.ShapeDtypeStruct((1, D), F32),
                   jax.ShapeDtypeStruct((T, D), BF16), jax.ShapeDtypeStruct((1, D), F32)],
        compiler_params=_params("arbitrary"),
    )(*ins)


def rope_tables():
    pos = jnp.arange(T, dtype=F32)
    inv = ROPE_THETA ** (-jnp.arange(0, ROPE_DIM, 2, dtype=F32) / ROPE_DIM)
    ang = pos[:, None] * inv[None, :]
    cos, sin = jnp.cos(ang), jnp.sin(ang)
    half = ROPE_DIM // 2
    one = jnp.ones((T, HD - ROPE_DIM), F32)
    zero = jnp.zeros((T, HD - ROPE_DIM), F32)
    zh = jnp.zeros((T, half), F32)
    c = jnp.concatenate([cos, cos, one], axis=1)
    s_up = jnp.concatenate([zh, sin, zero], axis=1)
    s_dn = jnp.concatenate([-sin, zh, zero], axis=1)
    rep = LANES // HD
    return tuple(jnp.tile(t, (1, rep)) for t in (c, s_up, s_dn))


def rope(name, t, tables, inverse, out_dtype=BF16, tb=256):
    c, s_up, s_dn = tables
    n_rot = 2 * A_W // LANES
    half = ROPE_DIM // 2
    blk = pl.BlockSpec((tb, 3 * A_W), lambda i: (i, 0))
    tab = pl.BlockSpec((tb, LANES), lambda i: (i, 0))

    def body(t_ref, c_ref, su_ref, sd_ref, o_ref):
        sgn = -1.0 if inverse else 1.0
        cos, up, dn = c_ref[...], sgn * su_ref[...], sgn * sd_ref[...]
        for j in range(n_rot):
            sl = slice(j * LANES, (j + 1) * LANES)
            x = t_ref[:, sl].astype(F32)
            o_ref[:, sl] = (x * cos + pltpu.roll(x, half, 1) * up + pltpu.roll(x, LANES - half, 1) * dn).astype(out_dtype)
        o_ref[:, 2 * A_W:] = t_ref[:, 2 * A_W:].astype(out_dtype)

    return pl.pallas_call(
        body, name=name, grid=(T // tb,), in_specs=[blk, tab, tab, tab], out_specs=blk,
        out_shape=jax.ShapeDtypeStruct((T, 3 * A_W), out_dtype), compiler_params=_params("parallel"),
    )(t, c, s_up, s_dn)


GW = 4 * HD


def _band_mask(b):
    qi = lax.broadcasted_iota(jnp.int32, (BLK, 2 * BLK), 0)
    kj = lax.broadcasted_iota(jnp.int32, (BLK, 2 * BLK), 1)
    return (kj <= qi + BLK) & (kj >= qi) & ((kj >= BLK) | (b > 0))


def attn_a_fwd(name, qkvr, g, r):
    length = T // r
    nblk = length // BLK
    view = qkvr.reshape(length, r * 3 * A_W)
    ncol = 3 * A_W // GW

    def col(section, prev):
        def idx(j, b):
            return (jnp.maximum(b - 1, 0) if prev else b, j * ncol + 3 * section + g)
        return pl.BlockSpec((BLK, GW), idx)

    out = pl.BlockSpec((BLK, GW), lambda j, b: (b, j))

    def body(q_ref, kp_ref, kc_ref, vp_ref, vc_ref, o_ref, lse_ref):
        mask = _band_mask(pl.program_id(1))
        k2 = jnp.concatenate([kp_ref[...], kc_ref[...]], axis=0)
        v2 = jnp.concatenate([vp_ref[...], vc_ref[...]], axis=0)
        q = q_ref[...]
        for h in range(4):
            sl = slice(h * HD, (h + 1) * HD)
            s = lax.dot_general(q[:, sl], k2[:, sl], (((1,), (1,)), ((), ())), preferred_element_type=F32) * SCALE
            s = jnp.where(mask, s, NEG)
            m = jnp.max(s, axis=1, keepdims=True)
            p = jnp.exp(s - m)
            l = jnp.sum(p, axis=1, keepdims=True)
            o_ref[:, sl] = jnp.dot((p / l).astype(BF16), v2[:, sl], preferred_element_type=F32)
            lse_ref[:, sl] = jnp.broadcast_to(m + jnp.log(l), (BLK, HD))

    shape = jax.ShapeDtypeStruct((length, r * GW), F32)
    o, lse = pl.pallas_call(
        body, name=name, grid=(r, nblk),
        in_specs=[col(0, False), col(1, True), col(1, False), col(2, True), col(2, False)],
        out_specs=[out, out], out_shape=[shape, shape], compiler_params=_params("parallel", "parallel"),
    )(view, view, view, view, view)
    return o.reshape(T, GW), lse.reshape(T, GW)


def attn_a_bwd(name, qkvr, do, lse, dd, g, r):
    length = T // r
    nblk = length // BLK
    view = qkvr.reshape(length, r * 3 * A_W)
    ncol = 3 * A_W // GW

    def col(section, shift):
        def idx(j, b):
            return (jnp.clip(b + shift, 0, nblk - 1), j * ncol + 3 * section + g)
        return pl.BlockSpec((BLK, GW), idx)

    def tok(shift):
        return pl.BlockSpec((BLK, GW), lambda j, b: (jnp.clip(b + shift, 0, nblk - 1), j))

    def body(q_ref, qn_ref, kp_ref, kc_ref, vp_ref, vc_ref, do_ref, don_ref, lse_ref, lsen_ref, dd_ref, ddn_ref,
             dq_ref, dk_ref, dv_ref):
        b = pl.program_id(1)
        mask = _band_mask(b)
        qi = lax.broadcasted_iota(jnp.int32, (2 * BLK, BLK), 0)
        kj = lax.broadcasted_iota(jnp.int32, (2 * BLK, BLK), 1)
        kmask = ((qi < BLK) & (kj <= qi)) | ((qi >= BLK) & (kj >= qi - BLK) & (b + 1 < nblk))
        k2 = jnp.concatenate([kp_ref[...], kc_ref[...]], axis=0)
        v2 = jnp.concatenate([vp_ref[...], vc_ref[...]], axis=0)
        q2 = jnp.concatenate([q_ref[...], qn_ref[...]], axis=0)
        do2 = jnp.concatenate([do_ref[...], don_ref[...]], axis=0)
        lse2 = jnp.concatenate([lse_ref[...], lsen_ref[...]], axis=0)
        dd2 = jnp.concatenate([dd_ref[...], ddn_ref[...]], axis=0)
        nt = (((1,), (1,)), ((), ()))
        tn = (((0,), (0,)), ((), ()))
        for h in range(4):
            sl = slice(h * HD, (h + 1) * HD)
            one = slice(h * HD, h * HD + 1)
            qh, kh, vh, doh = q2[:, sl], k2[:, sl], v2[:, sl], do2[:, sl]
            s = lax.dot_general(qh[:BLK], kh, nt, preferred_element_type=F32) * SCALE
            p = jnp.where(mask, jnp.exp(s - lse2[:BLK, one]), 0.0)
            dp = lax.dot_general(doh[:BLK], vh, nt, preferred_element_type=F32)
            ds = p * (dp + dd2[:BLK, one])
            dq_ref[:, sl] = jnp.dot(ds.astype(BF16), kh, preferred_element_type=F32) * SCALE
            kc, vc = kh[BLK:], vh[BLK:]
            s = lax.dot_general(qh, kc, nt, preferred_element_type=F32) * SCALE
            p = jnp.where(kmask, jnp.exp(s - lse2[:, one]), 0.0)
            dp = lax.dot_general(doh, vc, nt, preferred_element_type=F32)
            ds = p * (dp + dd2[:, one])
            dk_ref[:, sl] = lax.dot_general(ds.astype(BF16), qh, tn, preferred_element_type=F32) * SCALE
            dv_ref[:, sl] = lax.dot_general(p.astype(BF16), doh, tn, preferred_element_type=F32)

    dov = do.reshape(length, r * GW)
    lsev = lse.reshape(length, r * GW)
    ddv = dd.reshape(length, r * GW)
    shape = jax.ShapeDtypeStruct((length, r * GW), F32)
    dq, dk, dv = pl.pallas_call(
        body, name=name, grid=(r, nblk),
        in_specs=[col(0, 0), col(0, 1), col(1, -1), col(1, 0), col(2, -1), col(2, 0),
                  tok(0), tok(1), tok(0), tok(1), tok(0), tok(1)],
        out_specs=[tok(0)] * 3, out_shape=[shape] * 3, compiler_params=_params("parallel", "parallel"),
    )(view, view, view, view, view, view, dov, dov, lsev, lsev, ddv, ddv)
    return dq.reshape(T, GW), dk.reshape(T, GW), dv.reshape(T, GW)


def _head_sum(x):
    i = lax.div(lax.broadcasted_iota(jnp.int32, (GW, GW), 0), jnp.int32(HD))
    j = lax.div(lax.broadcasted_iota(jnp.int32, (GW, GW), 1), jnp.int32(HD))
    return jnp.dot(x, (i == j).astype(F32), precision=HIGHEST, preferred_element_type=F32)


def _alphas(lses):
    m = jnp.maximum(jnp.maximum(lses[0], lses[1]), lses[2])
    e = [jnp.exp(l - m) for l in lses]
    z = e[0] + e[1] + e[2]
    return [x / z for x in e]


def combine_fwd(name, os_, lses, tb=256):
    blk = pl.BlockSpec((tb, GW), lambda i: (i, 0))

    def body(o0, o1, o2, l0, l1, l2, oc_ref):
        al = _alphas([l0[...], l1[...], l2[...]])
        for g, o_ref in enumerate((o0, o1, o2)):
            oc_ref[:, g * GW:(g + 1) * GW] = (o_ref[...] * al[g]).astype(BF16)

    return pl.pallas_call(
        body, name=name, grid=(T // tb,), in_specs=[blk] * 6, out_specs=pl.BlockSpec((tb, A_W), lambda i: (i, 0)),
        out_shape=jax.ShapeDtypeStruct((T, A_W), BF16), compiler_params=_params("parallel"),
    )(*os_, *lses)


def combine_bwd(name, doc, os_, lses, tb=256):
    blk = pl.BlockSpec((tb, GW), lambda i: (i, 0))

    def body(doc_ref, o0, o1, o2, l0, l1, l2, d0, d1, d2, e0, e1, e2):
        al = _alphas([l0[...], l1[...], l2[...]])
        dal = [_head_sum(doc_ref[:, g * GW:(g + 1) * GW] * o_ref[...]) for g, o_ref in enumerate((o0, o1, o2))]
        mean = al[0] * dal[0] + al[1] * dal[1] + al[2] * dal[2]
        for g, (do_ref, dd_ref) in enumerate(((d0, e0), (d1, e1), (d2, e2))):
            do_ref[...] = (doc_ref[:, g * GW:(g + 1) * GW] * al[g]).astype(BF16)
            dd_ref[...] = al[g] * (dal[g] - mean) - al[g] * dal[g]

    res = pl.pallas_call(
        body, name=name, grid=(T // tb,), in_specs=[pl.BlockSpec((tb, A_W), lambda i: (i, 0))] + [blk] * 6,
        out_specs=[blk] * 6,
        out_shape=[jax.ShapeDtypeStruct((T, GW), BF16)] * 3 + [jax.ShapeDtypeStruct((T, GW), F32)] * 3,
        compiler_params=_params("parallel"),
    )(doc, *os_, *lses)
    return res[:3], res[3:]


TQ = 512
TK = TQ
FOX_HEADS = 1
NT = (((1,), (1,)), ((), ()))
TN = (((0,), (0,)), ((), ()))


def _fox_scores(q, kb, cj, row0, col0, masked):
    s = lax.dot_general(q, kb, NT, preferred_element_type=F32) * SCALE - cj
    if masked:
        qi = row0 + lax.broadcasted_iota(jnp.int32, s.shape, 0)
        kj = col0 + lax.broadcasted_iota(jnp.int32, s.shape, 1)
        s = jnp.where(kj <= qi, s, NEG)
    return s


def fox_fwd(name, q, k, v, ct):
    def body(q_ref, k_ref, v_ref, c_ref, o_ref, lse_ref):
        i = pl.program_id(1)
        n_full = i

        def step(jb, carry, masked):
            start = pl.multiple_of(jb * TK, TK)
            out = []
            for h in range(FOX_HEADS):
                m, l, acc = carry[h]
                kb = k_ref[h, pl.ds(start, TK), :]
                vb = v_ref[h, pl.ds(start, TK), :]
                s = _fox_scores(q_ref[h], kb, c_ref[h, pl.ds(jb, 1), :], i * TQ, jb * TK, masked)
                m_new = jnp.maximum(m, jnp.max(s, axis=1, keepdims=True))
                a = jnp.exp(m - m_new)
                p = jnp.exp(s - m_new)
                l = a * l + jnp.sum(p, axis=1, keepdims=True)
                acc = a * acc + jnp.dot(p.astype(BF16), vb, preferred_element_type=F32)
                out.append((m_new, l, acc))
            return tuple(out)

        init = tuple((jnp.full((TQ, 1), NEG, F32), jnp.zeros((TQ, 1), F32), jnp.zeros((TQ, HD), F32))
                     for _ in range(FOX_HEADS))
        carry = lax.fori_loop(0, n_full, lambda jb, c: step(jb, c, False), init)
        carry = step(n_full, carry, True)
        for h in range(FOX_HEADS):
            m, l, acc = carry[h]
            o_ref[h] = (acc / l).astype(BF16)
            lse_ref[h] = jnp.broadcast_to(m + jnp.log(l), (TQ, LANES))

    head = lambda h, i: (h, 0, 0)
    return pl.pallas_call(
        body, name=name, grid=(B_HEADS // FOX_HEADS, T // TQ),
        in_specs=[pl.BlockSpec((FOX_HEADS, TQ, HD), lambda h, i: (h, i, 0)), pl.BlockSpec((FOX_HEADS, T, HD), head),
                  pl.BlockSpec((FOX_HEADS, T, HD), head), pl.BlockSpec((FOX_HEADS, T // TK, TK), head)],
        out_specs=[pl.BlockSpec((FOX_HEADS, TQ, HD), lambda h, i: (h, i, 0)),
                   pl.BlockSpec((FOX_HEADS, TQ, LANES), lambda h, i: (h, i, 0))],
        out_shape=[jax.ShapeDtypeStruct((B_HEADS, T, HD), BF16), jax.ShapeDtypeStruct((B_HEADS, T, LANES), F32)],
        compiler_params=_params("parallel", "parallel"),
    )(q, k, v, ct)


def fox_bwd(name, q, k, v, ct, o, do, lse, prev):
    has_prev = prev is not None
    n_prev = 3 if has_prev else 0
    nq = T // TQ
    nk = T // TK

    def body(*refs):
        q_ref, k_ref, v_ref, c_ref, o_ref, do_ref, lse_ref = refs[:7]
        dq_ref, drow_ref, dk_ref, dv_ref, dc_ref, dq_acc, drow_acc = refs[7 + n_prev:]
        jb = pl.program_id(1)

        @pl.when(jb == 0)
        def _():
            dq_acc[...] = jnp.zeros_like(dq_acc)
            drow_acc[...] = jnp.zeros_like(drow_acc)

        dk_ref[...] = jnp.zeros_like(dk_ref)
        dv_ref[...] = jnp.zeros_like(dv_ref)
        kb = k_ref[...]
        cj = c_ref[pl.ds(jb, 1), :]

        def step(i, dc, masked):
            rows = pl.ds(pl.multiple_of(i * TQ, TQ), TQ)
            qv = q_ref[rows, :]
            dov = do_ref[rows, :]
            delta = jnp.sum(dov.astype(F32) * o_ref[rows, :].astype(F32), axis=1, keepdims=True)
            s = _fox_scores(qv, kb, cj, i * TQ, jb * TK, masked)
            p = jnp.exp(s - lse_ref[rows, 0:1])
            dp = lax.dot_general(dov, v_ref[...], NT, preferred_element_type=F32)
            ds = p * (dp - delta)
            dsb = ds.astype(BF16)
            dv_ref[...] += lax.dot_general(p.astype(BF16), dov, TN, preferred_element_type=F32)
            dk_ref[...] += lax.dot_general(dsb, qv, TN, preferred_element_type=F32)
            dq_acc[rows, :] += jnp.dot(dsb, kb, preferred_element_type=F32)
            drow_acc[rows, :] += jnp.broadcast_to(jnp.sum(ds, axis=1, keepdims=True), (TQ, LANES))
            return dc - jnp.sum(ds, axis=0, keepdims=True)

        dc = step(jb, jnp.zeros((1, TK), F32), True)
        dc = lax.fori_loop(jb + 1, nq, lambda i, c: step(i, c, False), dc)
        dk = dk_ref[...] * SCALE
        if has_prev:
            dk = dk + refs[7][...]
            dv_ref[...] += refs[8][...]
            dc = dc + refs[9][pl.ds(jb, 1), :]
        dk_ref[...] = dk
        dc_ref[pl.ds(jb, 1), :] = dc

        @pl.when(jb == nk - 1)
        def _():
            dq_ref[...] = (dq_acc[...] * SCALE).astype(BF16)
            for i in range(nq):
                drow_ref[i:i + 1, :] = jnp.transpose(drow_acc[i * TQ:(i + 1) * TQ, :])[0:1, :]

    head = lambda h, j: (h, 0, 0)
    full = pl.BlockSpec((None, T, HD), head)
    blk = pl.BlockSpec((None, TK, HD), lambda h, j: (h, j, 0))
    cspec = pl.BlockSpec((None, T // TK, TK), head)
    rspec = pl.BlockSpec((None, nq, TQ), head)
    ins = [q, k, v, ct, o, do, lse] + (list(prev) if has_prev else [])
    in_specs = [full, blk, blk, cspec, full, full, pl.BlockSpec((None, T, LANES), head)] + ([blk, blk, cspec] if has_prev else [])
    dq, drow, dk, dv, dc = pl.pallas_call(
        body, name=name, grid=(B_HEADS, nk), in_specs=in_specs, out_specs=[full, rspec, blk, blk, cspec],
        out_shape=[jax.ShapeDtypeStruct((B_HEADS, T, HD), BF16), jax.ShapeDtypeStruct((B_HEADS, nq, TQ), F32)]
        + [jax.ShapeDtypeStruct((B_HEADS, T, HD), F32)] * 2 + [jax.ShapeDtypeStruct((B_HEADS, nk, TK), F32)],
        scratch_shapes=[pltpu.VMEM((T, HD), F32), pltpu.VMEM((T, LANES), F32)],
        compiler_params=_params("parallel", "arbitrary"),
    )(*ins)
    return dq, drow, (dk, dv, dc)


def _tri(upper):
    i = lax.broadcasted_iota(jnp.int32, (BLK, BLK), 0)
    j = lax.broadcasted_iota(jnp.int32, (BLK, BLK), 1)
    return ((i <= j) if upper else (i >= j)).astype(F32)


def gates_fwd(name, fgt, b_f):
    def body(f_ref, b_ref, c_ref):
        tri = _tri(True)
        carry = jnp.zeros((B_HEADS, 1), F32)
        for blk in range(T // BLK):
            sl = slice(blk * BLK, (blk + 1) * BLK)
            z = f_ref[:, sl] + b_ref[...]
            logf = jnp.minimum(z, 0.0) - jnp.log(1.0 + jnp.exp(-jnp.abs(z)))
            cs = jnp.dot(logf, tri, precision=HIGHEST, preferred_element_type=F32) + carry
            c_ref[:, sl] = cs
            carry = cs[:, BLK - 1:BLK]

    return pl.pallas_call(
        body, name=name, out_shape=jax.ShapeDtypeStruct((B_HEADS, T), F32), compiler_params=_params(),
    )(fgt, b_f)


def gates_bwd(name, fgt, b_f, dcs):
    n_dc = len(dcs)

    def body(*refs):
        f_ref, b_ref = refs[:2]
        dc_refs = refs[2:2 + n_dc]
        dz_ref, db_ref = refs[2 + n_dc:]
        tri = _tri(False)
        carry = jnp.zeros((B_HEADS, 1), F32)
        db = jnp.zeros((B_HEADS, 1), F32)
        for blk in reversed(range(T // BLK)):
            sl = slice(blk * BLK, (blk + 1) * BLK)
            dc = dc_refs[0][:, sl]
            for r in dc_refs[1:]:
                dc = dc + r[:, sl]
            rc = jnp.dot(dc, tri, precision=HIGHEST, preferred_element_type=F32) + carry
            carry = rc[:, 0:1]
            z = f_ref[:, sl] + b_ref[...]
            e = jnp.exp(-jnp.abs(z))
            dz = rc * jnp.where(z >= 0.0, e, 1.0) / (1.0 + e)
            dz_ref[:, sl] = dz
            db = db + jnp.sum(dz, axis=1, keepdims=True)
        db_ref[...] = db

    return pl.pallas_call(
        body, name=name,
        out_shape=[jax.ShapeDtypeStruct((B_HEADS, T), F32), jax.ShapeDtypeStruct((B_HEADS, 1), F32)],
        compiler_params=_params(),
    )(fgt, b_f, *dcs)


CONV_TB = 256
GELU_K = math.sqrt(2.0 / math.pi)
GELU_C = 0.044715


def _shift_down(x, halo_ref, n):
    rows = lax.broadcasted_iota(jnp.int32, x.shape, 0)
    y = pltpu.roll(x, n, 0)
    for k in range(n):
        y = jnp.where(rows == k, halo_ref[pl.ds(8 - n + k, 1), :], y)
    return y


def _conv(x, halo_ref, cw_ref, cb_ref, first):
    x1 = _shift_down(x, halo_ref, 1)
    x2 = _shift_down(x, halo_ref, 2)
    rows = lax.broadcasted_iota(jnp.int32, x.shape, 0)
    x1 = jnp.where(first & (rows < 1), 0.0, x1)
    x2 = jnp.where(first & (rows < 2), 0.0, x2)
    y = x2 * cw_ref[0:1, :] + x1 * cw_ref[1:2, :] + x * cw_ref[2:3, :] + cb_ref[...]
    return y, x1, x2


def _gelu_parts(x):
    th = jnp.tanh(GELU_K * (x + GELU_C * x * x * x))
    val = 0.5 * x * (1.0 + th)
    grad = 0.5 * (1.0 + th) + 0.5 * x * (1.0 - th * th) * GELU_K * (1.0 + 3.0 * GELU_C * x * x)
    return val, grad


def _conv_specs(tb):
    def slab(off):
        return pl.BlockSpec((None, tb, SLAB), lambda d, i: (d + off, i, 0))

    def halo(off):
        return pl.BlockSpec((None, 8, SLAB), lambda d, i: (d + off, jnp.maximum(i * (tb // 8) - 1, 0), 0))

    def par(rows, off):
        return pl.BlockSpec((None, rows, SLAB), lambda d, i: (d + off, 0, 0))

    return slab, halo, par


def convglu_fwd(name, a, cw, cb, tb=CONV_TB):
    slab, halo, par = _conv_specs(tb)

    def body(ag, hg, av, hv, cwg, cbg, cwv, cbv, u_ref):
        first = pl.program_id(1) == 0
        gate, _, _ = _conv(ag[...], hg, cwg, cbg, first)
        val, _, _ = _conv(av[...], hv, cwv, cbv, first)
        u_ref[...] = (_gelu_parts(gate)[0] * val).astype(BF16)

    return pl.pallas_call(
        body, name=name, grid=(4, T // tb),
        in_specs=[slab(0), halo(0), slab(4), halo(4), par(3, 0), par(1, 0), par(3, 4), par(1, 4)],
        out_specs=slab(0), out_shape=jax.ShapeDtypeStruct((4, T, SLAB), BF16),
        compiler_params=_params("parallel", "parallel"),
    )(a, a, a, a, cw, cb, cw, cb)


def convglu_bwd(name, du, a, cw, cb, tb=CONV_TB):
    nblk = T // tb

    def rev(i):
        return nblk - 1 - i

    def pair(rows, idx):
        return pl.BlockSpec((2, None, rows, SLAB), lambda d, i: (0, d, idx(i), 0))

    def body(du_ref, a_ref, h_ref, cw_ref, cb_ref, da_ref, dcw_ref, dcb_ref, nxt_ref):
        step = pl.program_id(1)
        first = step == nblk - 1

        @pl.when(step == 0)
        def _():
            nxt_ref[...] = jnp.zeros_like(nxt_ref)

        gate, g1, g2 = _conv(a_ref[0], h_ref.at[0], cw_ref.at[0], cb_ref.at[0], first)
        val, v1, v2 = _conv(a_ref[1], h_ref.at[1], cw_ref.at[1], cb_ref.at[1], first)
        act, dact = _gelu_parts(gate)
        duv = du_ref[...].astype(F32)
        dys = (duv * val * dact, duv * act)
        rows = lax.broadcasted_iota(jnp.int32, duv.shape, 0)
        for k, (dy, xs) in enumerate(((dys[0], (g2, g1, a_ref[0])), (dys[1], (v2, v1, a_ref[1])))):
            def up(n):
                y = pltpu.roll(dy, tb - n, 0)
                for r in range(n):
                    y = jnp.where(rows == tb - n + r, nxt_ref[k, pl.ds(r, 1), :], y)
                return y

            da_ref[k] = (dy * cw_ref[k, 2:3, :] + up(1) * cw_ref[k, 1:2, :] + up(2) * cw_ref[k, 0:1, :]).astype(BF16)
            parts = [jnp.sum(dy * x, axis=0, keepdims=True) for x in xs]
            bias = jnp.sum(dy, axis=0, keepdims=True)

            @pl.when(step == 0)
            def _():
                for r in range(3):
                    dcw_ref[k, r:r + 1, :] = parts[r]
                dcb_ref[k] = bias

            @pl.when(step > 0)
            def _():
                for r in range(3):
                    dcw_ref[k, r:r + 1, :] += parts[r]
                dcb_ref[k] += bias
        for k in range(2):
            nxt_ref[k] = dys[k][0:8]

    da, dcw, dcb = pl.pallas_call(
        body, name=name, grid=(4, nblk),
        in_specs=[pl.BlockSpec((None, tb, SLAB), lambda d, i: (d, rev(i), 0)), pair(tb, rev),
                  pair(8, lambda i: jnp.maximum(rev(i) * (tb // 8) - 1, 0)), pair(3, lambda i: 0), pair(1, lambda i: 0)],
        out_specs=[pair(tb, rev), pair(3, lambda i: 0), pair(1, lambda i: 0)],
        out_shape=[jax.ShapeDtypeStruct((2, 4, T, SLAB), BF16), jax.ShapeDtypeStruct((2, 4, 3, SLAB), F32),
                   jax.ShapeDtypeStruct((2, 4, 1, SLAB), F32)],
        scratch_shapes=[pltpu.VMEM((2, 8, SLAB), F32)],
        compiler_params=_params("parallel", "arbitrary"),
    )(du, a.reshape(2, 4, T, SLAB), a.reshape(2, 4, T, SLAB), cw.reshape(2, 4, 3, SLAB), cb.reshape(2, 4, 1, SLAB))
    return da.reshape(NDEV, T, SLAB), dcw.reshape(NDEV, 3, SLAB), dcb.reshape(NDEV, 1, SLAB)


def loss_head(name, y, target, tb=256):
    row = pl.BlockSpec((tb, D), lambda i: (i, 0))

    def body(y_ref, t_ref, dy_ref, loss_ref):
        diff = y_ref[...] - t_ref[...]
        dy_ref[...] = diff * (1.0 / D)
        part = jnp.sum(jnp.sum(diff * diff, axis=1, keepdims=True), axis=0, keepdims=True) * (0.5 / D)

        @pl.when(pl.program_id(0) == 0)
        def _():
            loss_ref[...] = part

        @pl.when(pl.program_id(0) > 0)
        def _():
            loss_ref[...] += part

    return pl.pallas_call(
        body, name=name, grid=(T // tb,), in_specs=[row, row],
        out_specs=[row, pl.BlockSpec((1, 1), lambda i: (0, 0))],
        out_shape=[jax.ShapeDtypeStruct((T, D), F32), jax.ShapeDtypeStruct((1, 1), F32)],
        compiler_params=_params("arbitrary"),
    )(y, target)


def _row_tile(rows, cols, bytes_per_elem, budget=6 * 1024 * 1024):
    for tr in (1024, 512, 256, 128, 64, 32, 16, 8):
        if rows % tr == 0 and tr * cols * bytes_per_elem <= budget:
            return tr
    return rows


def adamw(name, parts, w, m, v, row0=0, prev=None):
    n_parts, rows, cols = parts.shape
    rows_all = w.shape[0]
    tr = _row_tile(math.gcd(rows, row0) if row0 else rows, cols, n_parts * parts.dtype.itemsize + 28)
    blk = pl.BlockSpec((tr, cols), lambda i: (row0 // tr + i, 0))
    b1c = 1.0 - ADAM_B1 ** ADAM_STEP
    b2c = 1.0 - ADAM_B2 ** ADAM_STEP
    n_prev = 0 if prev is None else 4

    def body(p_ref, w_ref, m_ref, v_ref, *rest):
        g_ref, d_ref, nm_ref, nv_ref = rest[n_prev:]
        g = p_ref[0].astype(F32)
        for k in range(1, n_parts):
            g = g + p_ref[k].astype(F32)
        nm = ADAM_B1 * m_ref[...] + (1.0 - ADAM_B1) * g
        nv = ADAM_B2 * v_ref[...] + (1.0 - ADAM_B2) * (g * g)
        g_ref[...] = g
        nm_ref[...] = nm
        nv_ref[...] = nv
        d_ref[...] = -ADAM_LR * ((nm / b1c) / (jnp.sqrt(nv / b2c) + ADAM_EPS) + ADAM_WD * w_ref[...])

    return pl.pallas_call(
        body, name=name, grid=(rows // tr,),
        in_specs=[pl.BlockSpec((n_parts, tr, cols), lambda i: (0, i, 0)), blk, blk, blk]
        + [pl.BlockSpec(memory_space=pl.ANY)] * n_prev,
        out_specs=[blk] * 4, out_shape=[jax.ShapeDtypeStruct((rows_all, cols), F32)] * 4,
        input_output_aliases={4 + k: k for k in range(n_prev)}, compiler_params=_params("parallel"),
    )(parts, w, m, v, *(prev or []))


def _peer(rel, x, y, c):
    return (1 - x if rel & 4 else x, 1 - y if rel & 2 else y, 1 - c if rel & 1 else c)


def _split_copies(ins, lands, send_sems, recv_sems, scatter):
    x, y, c = lax.axis_index("x"), lax.axis_index("y"), lax.axis_index("c")
    me = 4 * x + 2 * y + c
    copies = []
    for t in range(len(ins)):
        for rel in range(1, NDEV):
            px, py, pc = _peer(rel, x, y, c)
            src = ins[t].at[4 * px + 2 * py + pc] if scatter else ins[t]
            copies.append(pltpu.make_async_remote_copy(
                src_ref=src, dst_ref=lands[t].at[me], send_sem=send_sems.at[t * (NDEV - 1) + rel - 1],
                recv_sem=recv_sems.at[t * (NDEV - 1) + rel - 1], device_id=(px, py, pc),
                device_id_type=pl.DeviceIdType.MESH))
    return me, copies


def exchange_start(name, items, scatter, dep=None):
    n = len(items)
    hbm = pl.BlockSpec(memory_space=pltpu.HBM)
    sem = pl.BlockSpec(memory_space=pltpu.SEMAPHORE)
    has_dep = dep is not None
    land_shapes = [it.shape if scatter else (NDEV,) + it.shape for it in items]

    def body(*refs):
        ins, lands = refs[:n], refs[n:2 * n]
        outs = refs[2 * n + has_dep:]
        send_sems, recv_sems, token = outs[0], outs[1], outs[2 + 2 * n]
        _, copies = _split_copies(ins, lands, send_sems, recv_sems, scatter)
        for cp in copies:
            cp.start()
        token[...] = jnp.zeros_like(token)

    sems = pltpu.SemaphoreType.DMA((n * (NDEV - 1),))
    out_shape = ([sems, sems] + [pltpu.HBM(it.shape, it.dtype) for it in items]
                 + [pltpu.HBM(sh, it.dtype) for sh, it in zip(land_shapes, items)] + [jax.ShapeDtypeStruct((8, LANES), F32)])
    operands = ([pltpu.with_memory_space_constraint(it, pltpu.HBM) for it in items]
                + [pltpu.with_memory_space_constraint(lax.empty(sh, it.dtype), pltpu.HBM) for sh, it in zip(land_shapes, items)]
                + ([dep] if has_dep else []))
    res = pl.pallas_call(
        body, name=name, in_specs=[hbm] * (2 * n) + [pl.BlockSpec(memory_space=pl.ANY)] * has_dep,
        out_specs=[sem, sem] + [hbm] * (2 * n) + [pl.BlockSpec(memory_space=pltpu.VMEM)], out_shape=out_shape,
        input_output_aliases={t: 2 + t for t in range(2 * n)},
        compiler_params=pltpu.CompilerParams(has_side_effects=pltpu.SideEffectType.DATAFLOW_SIDE_EFFECTING),
    )(*operands)
    return (res[0], res[1], list(res[2:2 + n]), list(res[2 + n:2 + 2 * n]), scatter), res[2 + 2 * n]


def exchange_wait(name, handle, after):
    send_sems, recv_sems, ins, lands, scatter = handle
    n = len(ins)
    after = list(after) if isinstance(after, (list, tuple)) else [after]
    hbm = pl.BlockSpec(memory_space=pltpu.HBM)
    sem = pl.BlockSpec(memory_space=pltpu.SEMAPHORE)

    def body(*refs):
        _, copies = _split_copies(refs[:n], refs[n:2 * n], refs[2 * n], refs[2 * n + 1], scatter)
        for cp in copies:
            cp.wait_send()
            cp.wait_recv()

    res = pl.pallas_call(
        body, name=name, in_specs=[hbm] * (2 * n) + [sem, sem] + [pl.BlockSpec(memory_space=pl.ANY)] * len(after),
        out_specs=[hbm] * (2 * n), out_shape=[pltpu.HBM(a.shape, a.dtype) for a in ins + lands],
        input_output_aliases={t: t for t in range(2 * n)},
        compiler_params=pltpu.CompilerParams(has_side_effects=pltpu.SideEffectType.DATAFLOW_SIDE_EFFECTING),
    )(*ins, *lands, send_sems, recv_sems, *after)
    me = 4 * lax.axis_index("x") + 2 * lax.axis_index("y") + lax.axis_index("c")
    out = []
    for src, landed in zip(res[:n], res[n:]):
        own = lax.dynamic_index_in_dim(src, me, axis=0, keepdims=True) if scatter else src[None]
        out.append(lax.dynamic_update_slice_in_dim(landed, own, me, axis=0))
    return out


def sum_slots(name, parts):
    _, rows, cols = parts.shape

    def body(p_ref, o_ref):
        s = p_ref[0]
        for k in range(1, NDEV):
            s = s + p_ref[k]
        o_ref[...] = s

    return pl.pallas_call(body, name=name, out_shape=jax.ShapeDtypeStruct((rows, cols), F32), compiler_params=_params())(parts)


def _heads(t):
    return t.reshape(T, B_HEADS, HD).transpose(1, 0, 2)


def _unheads(t):
    return t.transpose(1, 0, 2).reshape(T, B_HEADS * HD)


def _cols_from_slots(g):
    return g.transpose(1, 0, 2).reshape(g.shape[1], NDEV * g.shape[2])


def _slots_from_cols(w):
    return w.reshape(w.shape[0], NDEV, w.shape[1] // NDEV).transpose(1, 0, 2)


def _pack(arrays, rows):
    flat = jnp.concatenate([a.reshape(-1).astype(F32) for a in arrays])
    return jnp.pad(flat, (0, rows * LANES - flat.shape[0])).reshape(rows, LANES)


def _unpack(buf, shapes):
    flat = buf.reshape(-1)
    out, pos = [], 0
    for sh in shapes:
        size = math.prod(sh)
        out.append(flat[pos:pos + size].reshape(sh))
        pos += size
    return out


def kernel(x, norm_gains, w_qkv_a, w_o_a, w_q_b, w_o_b, kv_norm, w_kvf, b_f, w_up, conv_w, conv_b, w_down, loss_target, m_norm_gains, m_w_qkv_a, m_w_o_a, m_w_q_b, m_w_o_b, m_kv_norm, m_w_kvf, m_b_f, m_w_up, m_conv_w, m_conv_b, m_w_down, v_norm_gains, v_w_qkv_a, v_w_o_a, v_w_q_b, v_w_o_b, v_kv_norm, v_w_kvf, v_b_f, v_w_up, v_conv_w, v_conv_b, v_w_down):
    me = 4 * lax.axis_index("x") + 2 * lax.axis_index("y") + lax.axis_index("c")
    n_b = DEPTH - N_A

    def bf(a):
        return a.astype(BF16)

    mixer_w = [[bf(w_qkv_a[l]), bf(w_o_a[l])] if l < N_A else [bf(w_q_b[l - N_A]), bf(w_o_b[l - N_A])] for l in range(DEPTH)]
    ffn_w = [[bf(w_up[l]), bf(w_down[l])] for l in range(DEPTH)]
    mixer_w[0] += [norm_gains, conv_w]
    mixer_w[N_A] += [bf(w_kvf)]
    handles, tok = [], None
    for l in range(DEPTH):
        for part, items in (("mixer", mixer_w[l]), ("ffn", ffn_w[l])):
            hd, tok = exchange_start(f"gather_start_{part}{l}", items, False, dep=tok)
            handles.append(hd)

    wqkv, woa, wqb, wob, wup, wdown = {}, {}, {}, {}, {}, {}

    def take_mixer(l, arrived):
        if l < N_A:
            wqkv[l] = _cols_from_slots(arrived[0])
            woa[l] = _cols_from_slots(arrived[1])
        else:
            wqb[l - N_A] = arrived[0].reshape(D, D)
            wob[l - N_A] = arrived[1].reshape(D, D)

    def take_ffn(l, arrived):
        wup[l] = arrived[0]
        wdown[l] = arrived[1].reshape(4, SLAB, D)

    arrived = exchange_wait("gather_wait_mixer0", handles[0], tok)
    take_mixer(0, arrived)
    gains = arrived[2].transpose(1, 2, 0, 3).reshape(DEPTH, 4, D)
    cws = [arrived[3][:, l] for l in range(DEPTH)]
    cbs = [conv_b[l].reshape(NDEV, 1, SLAB) for l in range(DEPTH)]
    tables = rope_tables()
    b_col = b_f.reshape(B_HEADS, 1)

    h = x.reshape(T, D)
    _, (xn,) = resid_norm("norm_in", h, None, None, [gains[0, 0]])
    saved = []
    shared = None
    for l in range(DEPTH):
        s = {"h": h, "xn": xn}
        if l >= 1:
            arrived = exchange_wait(f"gather_wait_mixer{l}", handles[2 * l], h)
            take_mixer(l, arrived[:2])
            if l == N_A:
                wkvf = jnp.pad(_cols_from_slots(arrived[2]), ((0, 0), (0, KVF_PAD - KVF)))
        if l < N_A:
            qkv = matmul(f"qkv{l}", xn, wqkv[l], tm=T, tn=768)
            s["qkvr"] = rope(f"rope{l}", qkv, tables, inverse=False)
            os_, lses = [], []
            for g, (_, r) in enumerate(A_GROUPS):
                o, lse = attn_a_fwd(f"attn_a{l}_{g}", s["qkvr"], g, r)
                os_.append(o)
                lses.append(lse)
            s["o"], s["lse"] = os_, lses
            s["oc"] = combine_fwd(f"combine{l}", os_, lses)
            mix = matmul(f"wo_a{l}", s["oc"], woa[l], tm=1024, tn=512)
        else:
            if l == N_A:
                kvf = matmul("kvf", xkv, wkvf, tm=1024, tn=768)
                k_h = _heads(kvf[:, :D].astype(BF16))
                v_h = _heads(kvf[:, D:2 * D].astype(BF16))
                fgt = kvf[:, 2 * D:KVF].T
                ct = gates_fwd("gates", fgt, b_col).reshape(B_HEADS, T // TK, TK)
                shared = {"xkv": xkv, "k": k_h, "v": v_h, "fgt": fgt, "ct": ct, "h": h}
            j = l - N_A
            s["q"] = _heads(matmul(f"wq_b{j}", xn, wqb[j], out_dtype=BF16, tm=1024, tn=512))
            s["o"], s["lse"] = fox_fwd(f"fox{j}", s["q"], shared["k"], shared["v"], shared["ct"])
            s["oc"] = _unheads(s["o"])
            mix = matmul(f"wo_b{j}", s["oc"], wob[j], tm=1024, tn=512)
        s["mix"] = mix
        take_ffn(l, exchange_wait(f"gather_wait_ffn{l}", handles[2 * l + 1], mix))
        s["h1"], (s["xn2"],) = resid_norm(f"norm_mid{l}", h, mix, gains[l, 1], [gains[l, 2]])
        s["a"] = matmul(f"up{l}", s["xn2"], wup[l], tm=T, tn=SLAB, batch="b_out")
        s["u"] = convglu_fwd(f"convglu{l}", s["a"], cws[l], cbs[l])
        s["f"] = matmul(f"down{l}", s["u"], wdown[l], tm=1024, tn=1024, batch="reduce")
        nxt = [gains[l + 1, 0]] if l + 1 < DEPTH else []
        if l == N_A - 1:
            nxt.append(kv_norm)
        h, normed = resid_norm(f"norm_out{l}", s["h1"], s["f"], gains[l, 3], nxt)
        if l + 1 < DEPTH:
            xn = normed[0]
        if l == N_A - 1:
            xkv = normed[1]
        saved.append(s)

    dh, loss_part = loss_head("loss", h, loss_target.reshape(T, D))

    d_gains = [[None] * 4 for _ in range(DEPTH)]
    d_cw, d_cb = [None] * DEPTH, [None] * DEPTH
    gw = {"qkv": [None] * N_A, "oa": [None] * N_A, "qb": [None] * n_b, "ob": [None] * n_b, "up": [None] * DEPTH,
          "down": [None] * DEPTH}
    kv_acc = None
    d_rows = []
    sent = []
    tok = None

    def slots_rows(g):
        return g.reshape(NDEV, g.shape[0] // NDEV, g.shape[1])

    for l in reversed(range(DEPTH)):
        s = saved[l]
        df, d_gains[l][3] = rms_bwd(f"bwd_norm_out{l}", s["f"], gains[l, 3], dh, out_dtype=BF16, dep=tok)
        du = matmul(f"bwd_down_x{l}", df, wdown[l], tb=True, out_dtype=BF16, tm=T, tn=SLAB, batch="b_out")
        gw["down"][l] = matmul(f"bwd_down_w{l}", s["u"], df, ta=True, out_dtype=BF16, tm=SLAB, tn=1024, batch="a_out")
        da, d_cw[l], d_cb[l] = convglu_bwd(f"bwd_convglu{l}", du, s["a"], cws[l], cbs[l])
        dxn2 = matmul(f"bwd_up_x{l}", da, wup[l], tb=True, tm=1024, tn=1024, batch="reduce")
        gw["up"][l] = matmul(f"bwd_up_w{l}", s["xn2"], da, ta=True, out_dtype=BF16, tm=1024, tn=SLAB, batch="b_out")
        ffn_items = [gw["up"][l], slots_rows(gw["down"][l].reshape(DFF, D))]
        ffn_what = [("up", l), ("down", l)]
        tok = None
        if l == 0:
            hd, tok = exchange_start("scatter_start_ffn0", ffn_items, True)
            sent.append((hd, ffn_what))
            ffn_items, ffn_what = [], []
        dh1, d_gains[l][2], dmix, d_gains[l][1] = rms_bwd_chain(
            f"bwd_norm_mid_mix{l}", s["h1"], gains[l, 2], dxn2, dh, s["mix"], gains[l, 1], dep=tok)
        if l < N_A:
            doc = matmul(f"bwd_wo_a_x{l}", dmix, woa[l], tb=True, tm=1024, tn=A_W)
            gw["oa"][l] = matmul(f"bwd_wo_a_w{l}", s["oc"], dmix, ta=True, out_dtype=BF16, tm=A_W, tn=512)
            dos, dds = combine_bwd(f"bwd_combine{l}", doc, s["o"], s["lse"])
            cols = [None] * 9
            for g, (_, r) in enumerate(A_GROUPS):
                dq, dk, dv = attn_a_bwd(f"bwd_attn_a{l}_{g}", s["qkvr"], dos[g], s["lse"][g], dds[g], g, r)
                cols[g], cols[3 + g], cols[6 + g] = dq, dk, dv
            dqkv = rope(f"bwd_rope{l}", jnp.concatenate(cols, axis=1), tables, inverse=True)
            dxn = matmul(f"bwd_qkv_x{l}", dqkv, wqkv[l], tb=True, tm=1024, tn=1024)
            gw["qkv"][l] = matmul(f"bwd_qkv_w{l}", s["xn"], dqkv, ta=True, out_dtype=BF16, tm=512, tn=768)
        else:
            j = l - N_A
            do = _heads(matmul(f"bwd_wo_b_x{j}", dmix, wob[j], tb=True, out_dtype=BF16, tm=1024, tn=512))
            gw["ob"][j] = matmul(f"bwd_wo_b_w{j}", s["oc"], dmix, ta=True, out_dtype=BF16, tm=512, tn=512)
            args = (s["q"], shared["k"], shared["v"], shared["ct"], s["o"], do, s["lse"])
            dq_h, drow, kv_acc = fox_bwd(f"bwd_fox{j}", *args, kv_acc)
            dq = _unheads(dq_h)
            d_rows.append(drow.reshape(B_HEADS, T))
            dxn = matmul(f"bwd_wq_b_x{j}", dq, wqb[j], tb=True, tm=1024, tn=512)
            gw["qb"][j] = matmul(f"bwd_wq_b_w{j}", s["xn"], dq, ta=True, out_dtype=BF16, tm=512, tn=512)
        dh, d_gains[l][0] = rms_bwd(f"bwd_norm_in{l}", s["h"], gains[l, 0], dxn, add=dh1)
        if l == N_A:
            dk_h, dv_h, dct = kv_acc
            dfgt, d_bf = gates_bwd("bwd_gates", shared["fgt"], b_col, [dct.reshape(B_HEADS, T)] + d_rows)
            dkvf = jnp.concatenate(
                [_unheads(dk_h).astype(BF16), _unheads(dv_h).astype(BF16), dfgt.T.astype(BF16),
                 jnp.zeros((T, KVF_PAD - KVF), BF16)], axis=1)
            dxkv = matmul("bwd_kvf_x", dkvf, wkvf, tb=True, tm=1024, tn=512)
            g_kvf_full = matmul("bwd_kvf_w", shared["xkv"], dkvf, ta=True, out_dtype=BF16, tm=512, tn=768)
            dh, d_kvn = rms_bwd("bwd_norm_kv", shared["h"], kv_norm, dxkv, add=dh)
        if l < N_A:
            items = [_slots_from_cols(gw["qkv"][l]), _slots_from_cols(gw["oa"][l])]
            what = [("qkv", l), ("oa", l)]
        else:
            items = [slots_rows(gw["qb"][l - N_A]), slots_rows(gw["ob"][l - N_A])]
            what = [("qb", l - N_A), ("ob", l - N_A)]
        if l == N_A:
            items.append(_slots_from_cols(g_kvf_full[:, :KVF]))
            what.append(("kvf", 0))
        if l > 0:
            hd, tok = exchange_start(f"scatter_start{l}", items + ffn_items, True)
            sent.append((hd, what + ffn_what))

    small_shapes = [(DEPTH, 4, D), (D,), (B_HEADS,), (DEPTH, 3, NDEV * SLAB), (DEPTH, NDEV * SLAB), (1,)]
    small = [
        jnp.stack([jnp.concatenate(row, axis=0) for row in d_gains]),
        d_kvn, d_bf,
        jnp.stack([d.transpose(1, 0, 2).reshape(3, NDEV * SLAB) for d in d_cw]),
        jnp.stack([d.reshape(NDEV * SLAB) for d in d_cb]),
        loss_part,
    ]
    small_rows = 848
    small_handle, tok = exchange_start("small_start", [_pack(small, small_rows)], False)
    last_handle, tok = exchange_start("scatter_start0", items, True, dep=tok)
    last_what = what

    big = {"qkv": (w_qkv_a, m_w_qkv_a, v_w_qkv_a), "oa": (w_o_a, m_w_o_a, v_w_o_a), "qb": (w_q_b, m_w_q_b, v_w_q_b),
           "ob": (w_o_b, m_w_o_b, v_w_o_b), "kvf": (w_kvf, m_w_kvf, v_w_kvf), "up": (w_up, m_w_up, v_w_up),
           "down": (w_down, m_w_down, v_w_down)}
    updated = {name: None for name in big}

    def collect(handle, what, after):
        received = exchange_wait(f"scatter_wait_{what[0][0]}{what[0][1]}", handle, after)
        for (name, layer), rec in zip(what, received):
            cols = rec.shape[-1]
            rows = rec.size // (NDEV * cols)
            w, m, v = (a.reshape(-1, cols) for a in big[name])
            updated[name] = adamw(f"adamw_{name}{layer}", rec.reshape(NDEV, rows, cols), w, m, v, row0=layer * rows,
                                  prev=updated[name])
        return [updated[name][0] for name, _ in what]

    after = [tok]
    for handle, what in sent:
        after = collect(handle, what, after)

    (small_all,) = exchange_wait("small_wait", small_handle, after)
    g_gains_full, g_kvn, g_bf, g_cw_full, g_cb, loss = _unpack(sum_slots("sum_small", small_all), small_shapes)
    g_gains_mine = lax.dynamic_slice_in_dim(g_gains_full, me * (D // NDEV), D // NDEV, axis=2)
    g_cw_mine = lax.dynamic_slice_in_dim(g_cw_full, me * SLAB, SLAB, axis=2)

    small_w = [norm_gains, kv_norm, b_f, conv_w, conv_b]
    small_m = [m_norm_gains, m_kv_norm, m_b_f, m_conv_w, m_conv_b]
    small_v = [v_norm_gains, v_kv_norm, v_b_f, v_conv_w, v_conv_b]
    small_g = [g_gains_mine, g_kvn, g_bf, g_cw_mine, g_cb]
    shapes = [w.shape for w in small_w]
    rows = 320
    res = adamw("adamw_small", _pack(small_g, rows)[None], _pack(small_w, rows), _pack(small_m, rows), _pack(small_v, rows))
    _, s_delta, s_m, s_v = [_unpack(r, shapes) for r in res]

    collect(last_handle, last_what, [s_delta[0]])
    big_out =[[r.reshape(big[name][0].shape) for r in updated[name]] for name in ("qkv", "oa", "qb", "ob", "kvf", "up", "down")]

    def pick(k):
        b = [o[k] for o in big_out]
        sm = {0: small_g, 1: s_delta, 2: s_m, 3: s_v}[k]
        return [sm[0], b[0], b[1], b[2], b[3], sm[1], b[4], sm[2], b[5], sm[3], sm[4], b[6]]

    return (loss.reshape(()), dh.reshape(1, T, D), *pick(0), *pick(1), *pick(2), *pick(3))
```
